```python
import jax, jax.numpy as jnp
from jax import lax
import numpy as np

D_MODEL = 2048
BATCH = 1
SEQ = 16384
DEPTH = 1

GLA_HEADS = 4
GLA_DK = D_MODEL // 16
GLA_DV = D_MODEL // 8
GLA_GATE_RANK = 16
GLA_GATE_TAU = 16.0
GLA_CHUNK = 64
CONV_WIDTH = 4

NSA_HEADS = 8
NSA_KV_HEADS = 2
NSA_GROUP = NSA_HEADS // NSA_KV_HEADS
NSA_HEAD_DIM = 128
CMP_BLOCK = 32
CMP_STRIDE = 16
CMP_HIDDEN = 128
SEL_BLOCK = 64
SEL_TOPK = 16
WINDOW = 512
Q_BLOCK = 128

D_FF = -(-8 * D_MODEL // (3 * 256)) * 256
EPS = 1e-6
NEG = -1e30

GLA_QK = GLA_HEADS * GLA_DK
GLA_VW = GLA_HEADS * GLA_DV
GLA_QKV = 2 * GLA_QK + GLA_VW
NSA_QW = NSA_HEADS * NSA_HEAD_DIM
NSA_KVW = NSA_KV_HEADS * NSA_HEAD_DIM
MIX_WIDTH = GLA_VW + NSA_QW
PROJ_SIZES = (GLA_QKV, GLA_GATE_RANK, GLA_VW, NSA_QW, NSA_KVW, NSA_KVW, NSA_KVW, NSA_KVW, NSA_KVW, NSA_KVW, 3 * NSA_HEADS)
PROJ_WIDTH = sum(PROJ_SIZES)

kernel_name = 'hybrid_gla_nsa_block'


def rms_norm(u, g):
    uf = u.astype(jnp.float32)
    y = uf * lax.rsqrt(jnp.mean(uf * uf, axis=-1, keepdims=True) + EPS)
    return (y * g.astype(jnp.float32)).astype(u.dtype)


def causal_short_conv(u, w):
    s = u.shape[1]
    up = jnp.pad(u, ((0, 0), (CONV_WIDTH - 1, 0), (0, 0)))
    return sum(up[:, i:i + s] * w[i] for i in range(CONV_WIDTH))


def gla_mixer(q, k, v, gate_lr, out_gate, gate_w2, gate_b, norm_g):
    b_, s, _ = q.shape
    h, c = GLA_HEADS, GLA_CHUNK
    n_chunks = s // c
    log_a = jax.nn.log_sigmoid((gate_lr @ gate_w2 + gate_b).astype(jnp.float32)) / GLA_GATE_TAU

    def chunks(u, d):
        return u.reshape(b_, n_chunks, c, h, d).transpose(1, 0, 3, 2, 4)

    qc = chunks(q.astype(jnp.float32) * GLA_DK ** -0.5, GLA_DK)
    kc = chunks(k.astype(jnp.float32), GLA_DK)
    vc = chunks(v.astype(jnp.float32), GLA_DV)
    gc = chunks(log_a, GLA_DK)
    causal = jnp.tril(jnp.ones((c, c), dtype=bool))[:, :, None]

    def step(state, inp):
        qi, ki, vi, gi = inp
        b = jnp.cumsum(gi, axis=2)
        o_inter = jnp.einsum('bhtd,bhde->bhte', qi * jnp.exp(b), state)
        decay = jnp.exp(jnp.where(causal, b[:, :, :, None, :] - b[:, :, None, :, :], -jnp.inf))
        scores = jnp.einsum('bhtd,bhsd,bhtsd->bhts', qi, ki, decay)
        o_intra = jnp.einsum('bhts,bhse->bhte', scores, vi)
        b_last = b[:, :, -1:, :]
        state = (jnp.exp(b_last[:, :, 0, :])[..., None] * state
                 + jnp.einsum('bhsd,bhse->bhde', ki * jnp.exp(b_last - b), vi))
        return state, o_inter + o_intra

    state0 = jnp.zeros((b_, h, GLA_DK, GLA_DV), jnp.float32)
    _, o = lax.scan(step, state0, (qc, kc, vc, gc))
    o = o.transpose(1, 0, 3, 2, 4).reshape(b_, s, h, GLA_DV)
    o = rms_norm(o, norm_g) * jax.nn.silu(out_gate.astype(jnp.float32)).reshape(b_, s, h, GLA_DV)
    return o.reshape(b_, s, GLA_VW).astype(q.dtype)


def nsa_mixer(q, kc_raw, vc_raw, ks_raw, vs_raw, kw_raw, vw_raw, gate_logits,
              q_g, kc_g, ks_g, kw_g, k_pos, k_w1, k_w2, v_pos, v_w1, v_w2):
    b_, s, _ = q.shape
    hk, g, dh = NSA_KV_HEADS, NSA_GROUP, NSA_HEAD_DIM
    n_cmp = (s - CMP_BLOCK) // CMP_STRIDE + 1
    n_sel_blocks = s // SEL_BLOCK
    n_sel = min(SEL_TOPK, n_sel_blocks)
    scale = dh ** -0.5

    def heads(u, n):
        return u.reshape(b_, s, n, dh).transpose(0, 2, 1, 3)

    qh = rms_norm(heads(q, NSA_HEADS), q_g).reshape(b_, hk, g, s, dh)

    cmp_idx = jnp.arange(n_cmp)[:, None] * CMP_STRIDE + jnp.arange(CMP_BLOCK)[None, :]

    def compress(u, pos, w1, w2):
        blocks = u[:, :, cmp_idx] + pos
        flat = blocks.reshape(b_, hk, n_cmp, CMP_BLOCK * dh)
        return jax.nn.silu(flat @ w1) @ w2

    k_cmp = rms_norm(compress(heads(kc_raw, hk), k_pos, k_w1, k_w2), kc_g)
    v_cmp = compress(heads(vc_raw, hk), v_pos, v_w1, v_w2)
    cmp_end = jnp.arange(n_cmp) * CMP_STRIDE + CMP_BLOCK - 1

    c_start = jnp.arange(n_cmp)[:, None] * CMP_STRIDE
    s_start = jnp.arange(n_sel_blocks)[None, :] * SEL_BLOCK
    overlap = jnp.clip(jnp.minimum(c_start + CMP_BLOCK, s_start + SEL_BLOCK)
                       - jnp.maximum(c_start, s_start), 0, None).astype(jnp.float32) / CMP_STRIDE

    k_sel = rms_norm(heads(ks_raw, hk), ks_g).reshape(b_, hk, n_sel_blocks, SEL_BLOCK, dh)
    v_sel = heads(vs_raw, hk).reshape(b_, hk, n_sel_blocks, SEL_BLOCK, dh)
    win_pad = ((0, 0), (0, 0), (WINDOW, 0), (0, 0))
    k_win = jnp.pad(rms_norm(heads(kw_raw, hk), kw_g), win_pad)
    v_win = jnp.pad(heads(vw_raw, hk), win_pad)
    gates = jax.nn.sigmoid(gate_logits.astype(jnp.float32)).reshape(b_, s, hk, g, 3).transpose(0, 2, 3, 1, 4)

    gather_blocks = jax.vmap(jax.vmap(lambda blk, ix: blk[ix]))
    sel_offsets = jnp.arange(SEL_BLOCK)
    win_offsets = jnp.arange(WINDOW + Q_BLOCK)
    sel_ids = jnp.arange(n_sel_blocks)

    def query_block(s0):
        t = s0 + jnp.arange(Q_BLOCK)
        qb = lax.dynamic_slice_in_dim(qh, s0, Q_BLOCK, axis=3)

        sc = jnp.einsum('bhgqd,bhnd->bhgqn', qb, k_cmp).astype(jnp.float32) * scale
        mc = cmp_end[None, :] <= t[:, None]
        pc = jax.nn.softmax(jnp.where(mc, sc, NEG), axis=-1) * mc
        o_cmp = jnp.einsum('bhgqn,bhnd->bhgqd', pc.astype(v_cmp.dtype), v_cmp)

        imp = jnp.einsum('bhgqn,nj->bhqj', pc, overlap)
        jt = (t // SEL_BLOCK)[:, None]
        forced = (sel_ids == 0) | (sel_ids == jt) | (sel_ids == jt - 1)
        imp = jnp.where(sel_ids > jt, -jnp.inf, jnp.where(forced, jnp.inf, imp))
        _, idx = lax.top_k(imp, n_sel)
        ks = gather_blocks(k_sel, idx).reshape(b_, hk, Q_BLOCK, n_sel * SEL_BLOCK, dh)
        vs = gather_blocks(v_sel, idx).reshape(b_, hk, Q_BLOCK, n_sel * SEL_BLOCK, dh)
        kpos = (idx[..., None] * SEL_BLOCK + sel_offsets).reshape(b_, hk, Q_BLOCK, n_sel * SEL_BLOCK)
        ms = (kpos <= t[:, None])[:, :, None]
        ss = jnp.einsum('bhgqd,bhqkd->bhgqk', qb, ks).astype(jnp.float32) * scale
        ps = jax.nn.softmax(jnp.where(ms, ss, NEG), axis=-1)
        o_sel = jnp.einsum('bhgqk,bhqkd->bhgqd', ps.astype(vs.dtype), vs)

        kwb = lax.dynamic_slice_in_dim(k_win, s0, WINDOW + Q_BLOCK, axis=2)
        vwb = lax.dynamic_slice_in_dim(v_win, s0, WINDOW + Q_BLOCK, axis=2)
        wpos = s0 - WINDOW + win_offsets
        dist = t[:, None] - wpos[None, :]
        mw = (wpos[None, :] >= 0) & (dist >= 0) & (dist < WINDOW)
        sw = jnp.einsum('bhgqd,bhkd->bhgqk', qb, kwb).astype(jnp.float32) * scale
        pw = jax.nn.softmax(jnp.where(mw, sw, NEG), axis=-1)
        o_win = jnp.einsum('bhgqk,bhkd->bhgqd', pw.astype(vwb.dtype), vwb)

        gb = lax.dynamic_slice_in_dim(gates, s0, Q_BLOCK, axis=3)
        return gb[..., 0:1] * o_cmp + gb[..., 1:2] * o_sel + gb[..., 2:3] * o_win

    n_blocks = s // Q_BLOCK
    out = lax.map(query_block, jnp.arange(n_blocks) * Q_BLOCK)
    out = out.transpose(1, 0, 4, 2, 3, 5).reshape(b_, s, NSA_QW)
    return out.astype(q.dtype)


def hybrid_layer(x, attn_norm_g, w_in, gla_conv_w, gla_gate_w2, gla_gate_b, gla_norm_g,
                 nsa_q_norm_g, nsa_kc_norm_g, nsa_ks_norm_g, nsa_kw_norm_g,
                 cmp_k_pos, cmp_k_w1, cmp_k_w2, cmp_v_pos, cmp_v_w1, cmp_v_w2,
                 w_out, ffn_norm_g, w_gate, w_up, w_down):
    n = rms_norm(x, attn_norm_g)
    proj = n @ w_in
    offsets = np.cumsum(PROJ_SIZES)[:-1].tolist()
    (g_qkv, g_lr, g_out, nq, kc, vc, ks, vs, kw, vw, ng) = jnp.split(proj, offsets, axis=-1)
    g_qkv = jax.nn.silu(causal_short_conv(g_qkv, gla_conv_w))
    gq, gk, gv = jnp.split(g_qkv, [GLA_QK, 2 * GLA_QK], axis=-1)
    gla_out = gla_mixer(gq, gk, gv, g_lr, g_out, gla_gate_w2, gla_gate_b, gla_norm_g)
    nsa_out = nsa_mixer(nq, kc, vc, ks, vs, kw, vw, ng,
                        nsa_q_norm_g, nsa_kc_norm_g, nsa_ks_norm_g, nsa_kw_norm_g,
                        cmp_k_pos, cmp_k_w1, cmp_k_w2, cmp_v_pos, cmp_v_w1, cmp_v_w2)
    h = x + jnp.concatenate([gla_out, nsa_out], axis=-1) @ w_out
    m = rms_norm(h, ffn_norm_g)
    return h + (jax.nn.silu(m @ w_gate) * (m @ w_up)) @ w_down


def setup_inputs(seed: int = 0) -> dict:
    key = jax.random.key(seed)
    ks = jax.random.split(key, 22)

    def dense(k, shape, fan_in):
        return jax.random.normal(k, shape, jnp.float32) * fan_in ** -0.5

    def gain(k, d):
        return 1.0 + 0.02 * jax.random.normal(k, (DEPTH, d), jnp.float32)

    dh = NSA_HEAD_DIM
    return {
        'x': jax.random.normal(ks[0], (BATCH, SEQ, D_MODEL), jnp.float32),
        'attn_norm_g': gain(ks[1], D_MODEL),
        'w_in': dense(ks[2], (DEPTH, D_MODEL, PROJ_WIDTH), D_MODEL),
        'gla_conv_w': dense(ks[3], (DEPTH, CONV_WIDTH, GLA_QKV), CONV_WIDTH),
        'gla_gate_w2': dense(ks[4], (DEPTH, GLA_GATE_RANK, GLA_QK), GLA_GATE_RANK),
        'gla_gate_b': 0.02 * jax.random.normal(ks[5], (DEPTH, GLA_QK), jnp.float32),
        'gla_norm_g': gain(ks[6], GLA_DV),
        'nsa_q_norm_g': gain(ks[7], dh),
        'nsa_kc_norm_g': gain(ks[8], dh),
        'nsa_ks_norm_g': gain(ks[9], dh),
        'nsa_kw_norm_g': gain(ks[10], dh),
        'cmp_k_pos': 0.1 * jax.random.normal(ks[11], (DEPTH, CMP_BLOCK, dh), jnp.float32),
        'cmp_k_w1': dense(ks[12], (DEPTH, CMP_BLOCK * dh, CMP_HIDDEN), CMP_BLOCK * dh),
        'cmp_k_w2': dense(ks[13], (DEPTH, CMP_HIDDEN, dh), CMP_HIDDEN),
        'cmp_v_pos': 0.1 * jax.random.normal(ks[14], (DEPTH, CMP_BLOCK, dh), jnp.float32),
        'cmp_v_w1': dense(ks[15], (DEPTH, CMP_BLOCK * dh, CMP_HIDDEN), CMP_BLOCK * dh),
        'cmp_v_w2': dense(ks[16], (DEPTH, CMP_HIDDEN, dh), CMP_HIDDEN),
        'w_out': dense(ks[17], (DEPTH, MIX_WIDTH, D_MODEL), MIX_WIDTH),
        'ffn_norm_g': gain(ks[18], D_MODEL),
        'w_gate': dense(ks[19], (DEPTH, D_MODEL, D_FF), D_MODEL),
        'w_up': dense(ks[20], (DEPTH, D_MODEL, D_FF), D_MODEL),
        'w_down': dense(ks[21], (DEPTH, D_FF, D_MODEL), D_FF),
    }


def reference(x, attn_norm_g, w_in, gla_conv_w, gla_gate_w2, gla_gate_b, gla_norm_g,
              nsa_q_norm_g, nsa_kc_norm_g, nsa_ks_norm_g, nsa_kw_norm_g,
              cmp_k_pos, cmp_k_w1, cmp_k_w2, cmp_v_pos, cmp_v_w1, cmp_v_w2,
              w_out, ffn_norm_g, w_gate, w_up, w_down):
    for l in range(DEPTH):
        x = hybrid_layer(x, attn_norm_g[l], w_in[l], gla_conv_w[l], gla_gate_w2[l], gla_gate_b[l],
                         gla_norm_g[l], nsa_q_norm_g[l], nsa_kc_norm_g[l], nsa_ks_norm_g[l],
                         nsa_kw_norm_g[l], cmp_k_pos[l], cmp_k_w1[l], cmp_k_w2[l], cmp_v_pos[l],
                         cmp_v_w1[l], cmp_v_w2[l], w_out[l], ffn_norm_g[l], w_gate[l], w_up[l],
                         w_down[l])
    return x
```

```python
import functools

import jax
import jax.numpy as jnp
import numpy as np
from jax import lax
from jax.experimental import pallas as pl
from jax.experimental.pallas import tpu as pltpu

D_MODEL = 2048
GLA_HEADS = 4
GLA_DK = 128
GLA_DV = 256
GLA_GATE_RANK = 16
GLA_GATE_TAU = 16.0
GLA_CHUNK = 64
GLA_SUB = 16
CONV_WIDTH = 4

NSA_HEADS = 8
NSA_KV_HEADS = 2
NSA_GROUP = 4
NSA_HEAD_DIM = 128
CMP_BLOCK = 32
CMP_STRIDE = 16
CMP_HIDDEN = 128
SEL_BLOCK = 64
SEL_TOPK = 16
WINDOW = 512
D_FF = 5632
EPS = 1e-6
NEG = -1e30

GLA_QK = GLA_HEADS * GLA_DK
GLA_VW = GLA_HEADS * GLA_DV
GLA_QKV = 2 * GLA_QK + GLA_VW
NSA_QW = NSA_HEADS * NSA_HEAD_DIM
NSA_KVW = NSA_KV_HEADS * NSA_HEAD_DIM

LANES = 128
VMEM_LIMIT = 56 * 1024 * 1024

C_GQ = 0
C_GK = GLA_QK
C_GV = 2 * GLA_QK
C_GO = GLA_QKV
C_NQ = C_GO + GLA_VW
C_KC = C_NQ + NSA_QW
C_VC = C_KC + NSA_KVW
C_KS = C_VC + NSA_KVW
C_VS = C_KS + NSA_KVW
C_KW = C_VS + NSA_KVW
C_VW = C_KW + NSA_KVW
C_MISC = C_VW + NSA_KVW
PROJ_PAD = C_MISC + 2 * LANES
GATE_LANE = GLA_GATE_RANK

F32 = jnp.float32
BF16 = jnp.bfloat16


def _params(sem):
    return pltpu.CompilerParams(dimension_semantics=sem, vmem_limit_bytes=VMEM_LIMIT)


def _nt_dot(a, b):
    return lax.dot_general(a, b, (((1,), (1,)), ((), ())), preferred_element_type=F32)


def _sigmoid(x):
    return 1.0 / (1.0 + jnp.exp(-x))


def _rms(x, g):
    return x * lax.rsqrt(jnp.mean(x * x, axis=-1, keepdims=True) + EPS) * g


def _norm_matmul_kernel(x_ref, g_ref, w_ref, o_ref, xn_ref):
    @pl.when(pl.program_id(1) == 0)
    def _():
        xn_ref[...] = _rms(x_ref[...], g_ref[...]).astype(BF16)

    o_ref[...] = jnp.dot(xn_ref[...], w_ref[...], preferred_element_type=F32)


def _norm_matmul(x, g, w, tm, tn):
    s, d = x.shape
    n = w.shape[1]
    return pl.pallas_call(
        _norm_matmul_kernel,
        grid=(s // tm, n // tn),
        in_specs=[pl.BlockSpec((tm, d), lambda i, j: (i, 0)),
                  pl.BlockSpec((1, d), lambda i, j: (0, 0)),
                  pl.BlockSpec((d, tn), lambda i, j: (0, j))],
        out_specs=pl.BlockSpec((tm, tn), lambda i, j: (i, j)),
        out_shape=jax.ShapeDtypeStruct((s, n), F32),
        scratch_shapes=[pltpu.VMEM((tm, d), BF16)],
        compiler_params=_params(("arbitrary", "arbitrary")),
        name="in_proj",
    )(x, g, w)


def _gla_kernel(q_ref, qp_ref, k_ref, kp_ref, v_ref, vp_ref, lr_ref, go_ref,
                cwq_ref, cwk_ref, cwv_ref, w2_ref, gb_ref, ng_ref, o_ref, st_ref, acc_ref):
    t_rows = q_ref.shape[0]
    i = pl.program_id(1)
    first = i == 0

    @pl.when(first)
    def _():
        st_ref[...] = jnp.zeros_like(st_ref)

    def conv_silu(u_ref, p_ref, w_ref):
        u = u_ref[...]
        tail = jnp.where(first, 0.0, p_ref[...])
        w = w_ref[...]
        rows8 = lax.broadcasted_iota(jnp.int32, tail.shape, 0)
        acc = u * w[CONV_WIDTH - 1:CONV_WIDTH]
        for d in range(1, CONV_WIDTH):
            r = pltpu.roll(u, d, 0)
            head = jnp.where(rows8 < d, pltpu.roll(tail, d, 0), r[:8])
            sh = jnp.concatenate([head, r[8:]], axis=0)
            acc = acc + sh * w[CONV_WIDTH - 1 - d:CONV_WIDTH - d]
        return acc * _sigmoid(acc)

    q = conv_silu(q_ref, qp_ref, cwq_ref) * (GLA_DK ** -0.5)
    k = conv_silu(k_ref, kp_ref, cwk_ref)
    v = conv_silu(v_ref, vp_ref, cwv_ref)

    z = jnp.dot(lr_ref[:, :GLA_GATE_RANK], w2_ref[...], preferred_element_type=F32,
                precision=lax.Precision.HIGHEST) + gb_ref[...]
    log_a = (jnp.minimum(z, 0.0) - jnp.log(1.0 + jnp.exp(-jnp.abs(z)))) * (1.0 / GLA_GATE_TAU)
    row = lax.broadcasted_iota(jnp.int32, log_a.shape, 0)
    rc = row & (GLA_CHUNK - 1)
    b = log_a
    d = 1
    while d < GLA_CHUNK:
        b = b + jnp.where(rc >= d, pltpu.roll(b, d, 0), 0.0)
        d *= 2

    rs = row & (GLA_SUB - 1)
    od = jnp.sum(q * k, axis=-1, keepdims=True) * v
    for delta in range(1, GLA_SUB):
        ok = rs >= delta
        e = jnp.exp(jnp.where(ok, b - pltpu.roll(b, delta, 0), -jnp.inf))
        sc = jnp.sum(q * pltpu.roll(k, delta, 0) * e, axis=-1, keepdims=True)
        od = od + sc * pltpu.roll(v, delta, 0)
    acc_ref[...] = od

    srow = lax.broadcasted_iota(jnp.int32, (GLA_CHUNK, GLA_DK), 0)
    for c in range(t_rows // GLA_CHUNK):
        lo = c * GLA_CHUNK
        bc = b[lo:lo + GLA_CHUNK]
        qc = q[lo:lo + GLA_CHUNK]
        kc = k[lo:lo + GLA_CHUNK]
        vc = v[lo:lo + GLA_CHUNK].astype(BF16)
        st = st_ref[...]
        b_last = bc[GLA_CHUNK - 1:GLA_CHUNK]
        o_inter = _nt_dot((qc * jnp.exp(bc)).astype(BF16), st.astype(BF16))
        acc_ref[lo:lo + GLA_CHUNK, :] += o_inter
        for sub in range(1, GLA_CHUNK // GLA_SUB):
            r0 = sub * GLA_SUB
            ref_b = bc[r0:r0 + 1]
            qq = (qc[r0:r0 + GLA_SUB] * jnp.exp(bc[r0:r0 + GLA_SUB] - ref_b)).astype(BF16)
            kk = (kc * jnp.exp(jnp.where(srow < r0, ref_b - bc, -jnp.inf))).astype(BF16)
            a = _nt_dot(qq, kk)
            acc_ref[lo + r0:lo + r0 + GLA_SUB, :] += jnp.dot(
                a.astype(BF16), vc, preferred_element_type=F32)
        kt = (kc * jnp.exp(b_last - bc)).astype(BF16)
        upd = lax.dot_general(vc, kt, (((0,), (0,)), ((), ())), preferred_element_type=F32)
        st_ref[...] = st * jnp.exp(b_last) + upd

    o = acc_ref[...]
    go = go_ref[...]
    o_ref[...] = _rms(o, ng_ref[...]) * (go * _sigmoid(go))


def _gla(p, conv_w, w2, gb, ng, tile):
    s = p.shape[0]
    t8 = tile // 8

    def prev(col):
        return lambda h, i: (jnp.maximum(i * t8 - 1, 0), col(h))

    qcol = lambda h: C_GQ // GLA_DK + h
    kcol = lambda h: C_GK // GLA_DK + h
    vcol = lambda h: C_GV // GLA_DV + h
    return pl.pallas_call(
        _gla_kernel,
        grid=(GLA_HEADS, s // tile),
        in_specs=[
            pl.BlockSpec((tile, GLA_DK), lambda h, i: (i, qcol(h))),
            pl.BlockSpec((8, GLA_DK), prev(qcol)),
            pl.BlockSpec((tile, GLA_DK), lambda h, i: (i, kcol(h))),
            pl.BlockSpec((8, GLA_DK), prev(kcol)),
            pl.BlockSpec((tile, GLA_DV), lambda h, i: (i, vcol(h))),
            pl.BlockSpec((8, GLA_DV), prev(vcol)),
            pl.BlockSpec((tile, LANES), lambda h, i: (i, C_MISC // LANES)),
            pl.BlockSpec((tile, GLA_DV), lambda h, i: (i, C_GO // GLA_DV + h)),
            pl.BlockSpec((CONV_WIDTH, GLA_DK), lambda h, i: (0, qcol(h))),
            pl.BlockSpec((CONV_WIDTH, GLA_DK), lambda h, i: (0, kcol(h))),
            pl.BlockSpec((CONV_WIDTH, GLA_DV), lambda h, i: (0, vcol(h))),
            pl.BlockSpec((GLA_GATE_RANK, GLA_DK), lambda h, i: (0, h)),
            pl.BlockSpec((1, GLA_DK), lambda h, i: (0, h)),
            pl.BlockSpec((1, GLA_DV), lambda h, i: (0, 0)),
        ],
        out_specs=pl.BlockSpec((tile, GLA_DV), lambda h, i: (i, h)),
        out_shape=jax.ShapeDtypeStruct((s, GLA_VW), F32),
        scratch_shapes=[pltpu.VMEM((GLA_DV, GLA_DK), F32), pltpu.VMEM((tile, GLA_DV), F32)],
        compiler_params=_params(("arbitrary", "arbitrary")),
        name="gla",
    )(p, p, p, p, p, p, p, p, conv_w, conv_w, conv_w, w2, gb, ng)


def _nsa_prep_kernel(q_ref, ks_ref, vs_ref, kw_ref, vw_ref, qg_ref, ksg_ref, kwg_ref,
                     qn_ref, ksa_ref, vsb_ref, kwn_ref, vwb_ref):
    t_rows = q_ref.shape[0]
    dh = NSA_HEAD_DIM
    scale = dh ** -0.5
    for h in range(NSA_HEADS):
        sl = slice(h * dh, (h + 1) * dh)
        qn_ref[:, sl] = (_rms(q_ref[:, sl], qg_ref[...]) * scale).astype(BF16)
    pos = pl.program_id(0) * t_rows + lax.broadcasted_iota(jnp.int32, (t_rows, LANES), 0)
    lane = lax.broadcasted_iota(jnp.int32, (t_rows, LANES), 1)
    onehot = jnp.where(lane == ((pos // SEL_BLOCK) & (LANES - 1)), 1.0, 0.0).astype(BF16)
    for h in range(NSA_KV_HEADS):
        sl = slice(h * dh, (h + 1) * dh)
        ksa_ref[:, 2 * h * dh:(2 * h + 1) * dh] = _rms(ks_ref[:, sl], ksg_ref[...]).astype(BF16)
        ksa_ref[:, (2 * h + 1) * dh:(2 * h + 2) * dh] = onehot
        kwn_ref[:, sl] = _rms(kw_ref[:, sl], kwg_ref[...]).astype(BF16)
    vsb_ref[...] = vs_ref[...].astype(BF16)
    vwb_ref[...] = vw_ref[...].astype(BF16)


def _nsa_prep(p, qg, ksg, kwg, tile):
    s = p.shape[0]
    kv = NSA_KVW
    col = lambda c, w: (lambda i: (i, c // w))
    row = lambda w: pl.BlockSpec((tile, w), lambda i: (i, 0))
    gain = pl.BlockSpec((1, NSA_HEAD_DIM), lambda i: (0, 0))
    return pl.pallas_call(
        _nsa_prep_kernel,
        grid=(s // tile,),
        in_specs=[pl.BlockSpec((tile, NSA_QW), col(C_NQ, NSA_QW)),
                  pl.BlockSpec((tile, kv), col(C_KS, kv)),
                  pl.BlockSpec((tile, kv), col(C_VS, kv)),
                  pl.BlockSpec((tile, kv), col(C_KW, kv)),
                  pl.BlockSpec((tile, kv), col(C_VW, kv)),
                  gain, gain, gain],
        out_specs=[row(NSA_QW), row(2 * kv), row(kv), row(kv), row(kv)],
        out_shape=[jax.ShapeDtypeStruct((s, NSA_QW), BF16),
                   jax.ShapeDtypeStruct((s, 2 * kv), BF16),
                   jax.ShapeDtypeStruct((s, kv), BF16),
                   jax.ShapeDtypeStruct((s, kv), BF16),
                   jax.ShapeDtypeStruct((s, kv), BF16)],
        compiler_params=_params(("arbitrary",)),
        name="nsa_prep",
    )(p, p, p, p, p, qg, ksg, kwg)


def _compress_kernel(u_ref, w1_ref, pos_ref, w1f_ref, w2_ref, g_ref, o_ref):
    n = o_ref.shape[1]
    half = CMP_BLOCK // CMP_STRIDE
    assert half == 2
    acc = jnp.zeros((n, 2 * CMP_HIDDEN), F32)
    for l in range(CMP_STRIDE):
        x = u_ref[pl.ds(l, n, stride=CMP_STRIDE), :].astype(BF16)
        acc = acc + jnp.dot(x, w1_ref[0, l], preferred_element_type=F32)
    posb = jnp.dot(jnp.broadcast_to(pos_ref[0], (8, pos_ref.shape[2])), w1f_ref[0],
                   preferred_element_type=F32, precision=lax.Precision.HIGHEST)[0:1]
    hid = acc[:, :CMP_HIDDEN] + pltpu.roll(acc[:, CMP_HIDDEN:], n - 1, 0) + posb
    hid = hid * _sigmoid(hid)
    out = jnp.dot(hid.astype(BF16), w2_ref[0], preferred_element_type=F32)
    is_k = pl.program_id(0) < NSA_KV_HEADS
    out = jnp.where(is_k, _rms(out, g_ref[...]), out)
    row = lax.broadcasted_iota(jnp.int32, out.shape, 0)
    o_ref[0] = jnp.where(row < n - 1, out, 0.0).astype(BF16)


def _compress(p, w1s, pos, w1f, w2, g):
    s = p.shape[0]
    n = s // CMP_STRIDE
    dh = NSA_HEAD_DIM
    return pl.pallas_call(
        _compress_kernel,
        grid=(2 * NSA_KV_HEADS,),
        in_specs=[pl.BlockSpec((s, dh), lambda j: (0, C_KC // dh + j)),
                  pl.BlockSpec((1, CMP_STRIDE, dh, 2 * CMP_HIDDEN), lambda j: (j // 2, 0, 0, 0)),
                  pl.BlockSpec((1, 1, CMP_BLOCK * dh), lambda j: (j // 2, 0, 0)),
                  pl.BlockSpec((1, CMP_BLOCK * dh, CMP_HIDDEN), lambda j: (j // 2, 0, 0)),
                  pl.BlockSpec((1, CMP_HIDDEN, dh), lambda j: (j // 2, 0, 0)),
                  pl.BlockSpec((1, dh), lambda j: (0, 0))],
        out_specs=pl.BlockSpec((1, n, dh), lambda j: (j, 0, 0)),
        out_shape=jax.ShapeDtypeStruct((2 * NSA_KV_HEADS, n, dh), BF16),
        compiler_params=_params(("arbitrary",)),
        name="compress",
    )(p, w1s, pos, w1f, w2, g)


def _stack_heads(q):
    dh = NSA_HEAD_DIM
    return jnp.concatenate([q[:, g * dh:(g + 1) * dh] for g in range(NSA_GROUP)], axis=0)


def _unstack_heads(o, tq):
    return jnp.concatenate([o[g * tq:(g + 1) * tq] for g in range(NSA_GROUP)], axis=1)


def _branch_gate(misc, branch, tq):
    sg = _sigmoid(misc)
    base = GATE_LANE + branch * NSA_GROUP
    return jnp.concatenate([sg[:, base + g:base + g + 1] for g in range(NSA_GROUP)], axis=0)


def _cmp_attn_kernel(q_ref, kc_ref, vc_ref, ovl_ref, misc_ref, o_ref, bias_ref, *, n_sel):
    tq = q_ref.shape[0]
    nc = kc_ref.shape[1]
    nbp = ovl_ref.shape[0]
    s0 = pl.program_id(1) * tq
    qs = _stack_heads(q_ref[...])
    sc = _nt_dot(qs, kc_ref[0])
    row = lax.broadcasted_iota(jnp.int32, sc.shape, 0)
    col = lax.broadcasted_iota(jnp.int32, sc.shape, 1)
    t = s0 + (row & (tq - 1))
    valid = col * CMP_STRIDE + (CMP_BLOCK - 1) <= t
    sm = jnp.where(valid, sc, NEG)
    m = jnp.max(sm, axis=-1, keepdims=True)
    e = jnp.where(valid, jnp.exp(sm - m), 0.0)
    pc = e * (1.0 / jnp.maximum(jnp.sum(e, axis=-1, keepdims=True), 1e-30))
    o = jnp.dot(pc.astype(BF16), vc_ref[0], preferred_element_type=F32)
    o_ref[...] = _unstack_heads(o * _branch_gate(misc_ref[...], 0, tq), tq)

    ps = pc[0:tq]
    for g in range(1, NSA_GROUP):
        ps = ps + pc[g * tq:(g + 1) * tq]
    ovl = ovl_ref[...]
    hi = ps.astype(BF16)
    r1 = ps - hi.astype(F32)
    mid = r1.astype(BF16)
    lo = (r1 - mid.astype(F32)).astype(BF16)
    imp = _nt_dot(ovl, hi) + _nt_dot(ovl, mid) + _nt_dot(ovl, lo)

    jb = lax.broadcasted_iota(jnp.int32, (nbp, tq), 0)
    jt = (s0 + lax.broadcasted_iota(jnp.int32, (nbp, tq), 1)) // SEL_BLOCK
    forced = (jb == 0) | (jb == jt) | (jb == jt - 1)
    cur = jnp.where(jb > jt, -jnp.inf, jnp.where(forced, jnp.inf, imp))
    sel = jnp.zeros((nbp, tq), F32)
    for _ in range(n_sel):
        mx = jnp.max(cur, axis=0, keepdims=True)
        idx = jnp.min(jnp.where(cur == mx, jb, nbp), axis=0, keepdims=True)
        pick = jb == idx
        sel = jnp.where(pick, 1.0, sel)
        cur = jnp.where(pick, -jnp.inf, cur)
    bias_t = jnp.where((sel > 0.0) & (jb <= jt), 0.0, NEG)
    bias_ref[...] = bias_t.T.astype(BF16)


def _cmp_attn(qn, cmp, ovl_t, p, tq, n_sel):
    s = qn.shape[0]
    nc = cmp.shape[1]
    nbp = ovl_t.shape[0]
    gw = NSA_GROUP * NSA_HEAD_DIM
    return pl.pallas_call(
        functools.partial(_cmp_attn_kernel, n_sel=n_sel),
        grid=(NSA_KV_HEADS, s // tq),
        in_specs=[pl.BlockSpec((tq, gw), lambda h, i: (i, h)),
                  pl.BlockSpec((1, nc, NSA_HEAD_DIM), lambda h, i: (h, 0, 0)),
                  pl.BlockSpec((1, nc, NSA_HEAD_DIM), lambda h, i: (NSA_KV_HEADS + h, 0, 0)),
                  pl.BlockSpec((nbp, nc), lambda h, i: (0, 0)),
                  pl.BlockSpec((tq, LANES), lambda h, i: (i, C_MISC // LANES + h))],
        out_specs=[pl.BlockSpec((tq, gw), lambda h, i: (i, h)),
                   pl.BlockSpec((tq, nbp), lambda h, i: (i, h))],
        out_shape=[jax.ShapeDtypeStruct((s, NSA_QW), F32),
                   jax.ShapeDtypeStruct((s, NSA_KV_HEADS * nbp), BF16)],
        compiler_params=_params(("arbitrary", "arbitrary")),
        name="cmp_attn",
    )(qn, cmp, cmp, ovl_t, p)


def _win_attn_kernel(q_ref, k_ref, v_ref, misc_ref, prev_ref, o_ref, *, span):
    tq = q_ref.shape[0]
    s0 = pl.program_id(1) * tq
    start = pl.multiple_of(jnp.maximum(s0 - WINDOW, 0), tq)
    qs = _stack_heads(q_ref[...])
    sc = _nt_dot(qs, k_ref[pl.ds(start, span), :])
    row = lax.broadcasted_iota(jnp.int32, sc.shape, 0)
    col = lax.broadcasted_iota(jnp.int32, sc.shape, 1)
    dist = s0 + (row & (tq - 1)) - (start + col)
    sm = jnp.where((dist >= 0) & (dist < WINDOW), sc, NEG)
    m = jnp.max(sm, axis=-1, keepdims=True)
    e = jnp.exp(sm - m)
    o = jnp.dot(e.astype(BF16), v_ref[pl.ds(start, span), :], preferred_element_type=F32)
    o = o * (_branch_gate(misc_ref[...], 2, tq) / jnp.sum(e, axis=-1, keepdims=True))
    o_ref[...] = prev_ref[...] + _unstack_heads(o, tq)


def _win_attn(qn, kwn, vwb, p, prev, tq):
    s = qn.shape[0]
    gw = NSA_GROUP * NSA_HEAD_DIM
    span = min(s, WINDOW + tq)
    return pl.pallas_call(
        functools.partial(_win_attn_kernel, span=span),
        grid=(NSA_KV_HEADS, s // tq),
        in_specs=[pl.BlockSpec((tq, gw), lambda h, i: (i, h)),
                  pl.BlockSpec((s, NSA_HEAD_DIM), lambda h, i: (0, h)),
                  pl.BlockSpec((s, NSA_HEAD_DIM), lambda h, i: (0, h)),
                  pl.BlockSpec((tq, LANES), lambda h, i: (i, C_MISC // LANES + h)),
                  pl.BlockSpec((tq, gw), lambda h, i: (i, h))],
        out_specs=pl.BlockSpec((tq, gw), lambda h, i: (i, h)),
        out_shape=jax.ShapeDtypeStruct((s, NSA_QW), F32),
        compiler_params=_params(("arbitrary", "arbitrary")),
        name="win_attn",
    )(qn, kwn, vwb, p, prev)


def _sel_attn_kernel(q_ref, bias_ref, ka_ref, v_ref, misc_ref, prev_ref, o_ref,
                     lhs_ref, m_ref, l_ref, acc_ref, *, tk):
    tq = q_ref.shape[0]
    dh = NSA_HEAD_DIM
    n_super = lhs_ref.shape[0]
    tiles_per_super = LANES * SEL_BLOCK // tk
    qi = pl.program_id(1)
    s0 = qi * tq
    for sup in range(n_super):
        for g in range(NSA_GROUP):
            lhs_ref[sup, g * tq:(g + 1) * tq, 0:dh] = q_ref[:, g * dh:(g + 1) * dh]
            lhs_ref[sup, g * tq:(g + 1) * tq, dh:2 * dh] = bias_ref[:, sup * LANES:(sup + 1) * LANES]
    m_ref[...] = jnp.full_like(m_ref, -jnp.inf)
    l_ref[...] = jnp.zeros_like(l_ref)
    acc_ref[...] = jnp.zeros_like(acc_ref)

    def step(kt, diagonal):
        k0 = pl.multiple_of(kt * tk, tk)
        sc = _nt_dot(lhs_ref[kt // tiles_per_super], ka_ref[pl.ds(k0, tk), :])
        if diagonal:
            row = lax.broadcasted_iota(jnp.int32, sc.shape, 0)
            col = lax.broadcasted_iota(jnp.int32, sc.shape, 1)
            sc = jnp.where(k0 + col <= s0 + (row & (tq - 1)), sc, NEG)
        m_old = m_ref[...]
        m_new = jnp.maximum(m_old, jnp.max(sc, axis=-1, keepdims=True))
        alpha = jnp.exp(m_old - m_new)
        pr = jnp.exp(sc - m_new)
        l_ref[...] = alpha * l_ref[...] + jnp.sum(pr, axis=-1, keepdims=True)
        acc_ref[...] = alpha * acc_ref[...] + jnp.dot(
            pr.astype(BF16), v_ref[pl.ds(k0, tk), :], preferred_element_type=F32)
        m_ref[...] = m_new

    n_full = s0 // tk

    def body(kt, carry):
        step(kt, False)
        return carry

    lax.fori_loop(0, n_full, body, 0)
    step(n_full, True)
    o = acc_ref[...] * (_branch_gate(misc_ref[...], 1, tq) / l_ref[...])
    o_ref[...] = prev_ref[...] + _unstack_heads(o, tq)


def _sel_attn(qn, bias, ksa, vsb, p, prev, tq, tk):
    s = qn.shape[0]
    dh = NSA_HEAD_DIM
    gw = NSA_GROUP * dh
    nbp = bias.shape[1] // NSA_KV_HEADS
    n_super = nbp // LANES
    assert tk % tq == 0 and (LANES * SEL_BLOCK) % tk == 0 and s % tk == 0
    return pl.pallas_call(
        functools.partial(_sel_attn_kernel, tk=tk),
        grid=(NSA_KV_HEADS, s // tq),
        in_specs=[pl.BlockSpec((tq, gw), lambda h, i: (i, h)),
                  pl.BlockSpec((tq, nbp), lambda h, i: (i, h)),
                  pl.BlockSpec((s, 2 * dh), lambda h, i: (0, h)),
                  pl.BlockSpec((s, dh), lambda h, i: (0, h)),
                  pl.BlockSpec((tq, LANES), lambda h, i: (i, C_MISC // LANES + h)),
                  pl.BlockSpec((tq, gw), lambda h, i: (i, h))],
        out_specs=pl.BlockSpec((tq, gw), lambda h, i: (i, h)),
        out_shape=jax.ShapeDtypeStruct((s, NSA_QW), F32),
        scratch_shapes=[pltpu.VMEM((n_super, NSA_GROUP * tq, 2 * dh), BF16),
                        pltpu.VMEM((NSA_GROUP * tq, 1), F32),
                        pltpu.VMEM((NSA_GROUP * tq, 1), F32),
                        pltpu.VMEM((NSA_GROUP * tq, dh), F32)],
        compiler_params=_params(("arbitrary", "arbitrary")),
        name="sel_attn",
    )(qn, bias, ksa, vsb, p, prev)


def _out_proj_kernel(x_ref, a_ref, b_ref, wa_ref, wb_ref, o_ref):
    o_ref[...] = (x_ref[...]
                  + jnp.dot(a_ref[...].astype(BF16), wa_ref[...], preferred_element_type=F32)
                  + jnp.dot(b_ref[...].astype(BF16), wb_ref[...], preferred_element_type=F32))


def _out_proj(x, a, b, w, tm):
    s, d = x.shape
    ka, kb = a.shape[1], b.shape[1]
    assert ka == kb
    return pl.pallas_call(
        _out_proj_kernel,
        grid=(s // tm,),
        in_specs=[pl.BlockSpec((tm, d), lambda i: (i, 0)),
                  pl.BlockSpec((tm, ka), lambda i: (i, 0)),
                  pl.BlockSpec((tm, kb), lambda i: (i, 0)),
                  pl.BlockSpec((ka, d), lambda i: (0, 0)),
                  pl.BlockSpec((kb, d), lambda i: (1, 0))],
        out_specs=pl.BlockSpec((tm, d), lambda i: (i, 0)),
        out_shape=jax.ShapeDtypeStruct((s, d), F32),
        compiler_params=_params(("arbitrary",)),
        name="out_proj",
    )(x, a, b, w, w)


def _ffn_kernel(h_ref, g_ref, wg_ref, wu_ref, wd_ref, o_ref, hn_ref):
    @pl.when(pl.program_id(1) == 0)
    def _():
        h = h_ref[...]
        hn_ref[...] = _rms(h, g_ref[...]).astype(BF16)
        o_ref[...] = h

    hn = hn_ref[...]
    a = jnp.dot(hn, wg_ref[...], preferred_element_type=F32)
    u = jnp.dot(hn, wu_ref[...], preferred_element_type=F32)
    z = (a * _sigmoid(a) * u).astype(BF16)
    o_ref[...] += jnp.dot(z, wd_ref[...], preferred_element_type=F32)


def _ffn(h, g, wg, wu, wd, tm, tf):
    s, d = h.shape
    f = wg.shape[1]
    return pl.pallas_call(
        _ffn_kernel,
        grid=(s // tm, f // tf),
        in_specs=[pl.BlockSpec((tm, d), lambda i, j: (i, 0)),
                  pl.BlockSpec((1, d), lambda i, j: (0, 0)),
                  pl.BlockSpec((d, tf), lambda i, j: (0, j)),
                  pl.BlockSpec((d, tf), lambda i, j: (0, j)),
                  pl.BlockSpec((tf, d), lambda i, j: (j, 0))],
        out_specs=pl.BlockSpec((tm, d), lambda i, j: (i, 0)),
        out_shape=jax.ShapeDtypeStruct((s, d), F32),
        scratch_shapes=[pltpu.VMEM((tm, d), BF16)],
        compiler_params=_params(("arbitrary", "arbitrary")),
        name="ffn",
    )(h, g, wg, wu, wd)


def _regroup_w_in(w_in):
    o_lr = GLA_QKV
    o_go = o_lr + GLA_GATE_RANK
    o_ng = o_go + GLA_VW + NSA_QW + 6 * NSA_KVW
    d = w_in.shape[0]
    lr = w_in[:, o_lr:o_go]
    ng = w_in[:, o_ng:o_ng + 3 * NSA_HEADS].reshape(d, NSA_KV_HEADS, NSA_GROUP, 3)
    ng = ng.transpose(0, 1, 3, 2).reshape(d, NSA_KV_HEADS, 3 * NSA_GROUP)
    pad = lambda n: jnp.zeros((d, n), w_in.dtype)
    misc0 = jnp.concatenate([lr, ng[:, 0], pad(LANES - GLA_GATE_RANK - 3 * NSA_GROUP)], axis=1)
    misc1 = jnp.concatenate([pad(GLA_GATE_RANK), ng[:, 1], pad(LANES - GLA_GATE_RANK - 3 * NSA_GROUP)], axis=1)
    return jnp.concatenate([w_in[:, :o_lr], w_in[:, o_go:o_ng], misc0, misc1], axis=1).astype(BF16)


def _overlap_t(s, nbp):
    n = s // CMP_STRIDE
    c0 = np.arange(n)[None, :] * CMP_STRIDE
    s0 = np.arange(nbp)[:, None] * SEL_BLOCK
    ov = np.clip(np.minimum(c0 + CMP_BLOCK, s0 + SEL_BLOCK) - np.maximum(c0, s0), 0, None) / CMP_STRIDE
    ov[:, n - 1] = 0.0
    ov[s // SEL_BLOCK:] = 0.0
    return jnp.asarray(ov, BF16)


def _layer(x, attn_norm_g, w_in, gla_conv_w, gla_gate_w2, gla_gate_b, gla_norm_g,
           nsa_q_norm_g, nsa_kc_norm_g, nsa_ks_norm_g, nsa_kw_norm_g,
           cmp_k_pos, cmp_k_w1, cmp_k_w2, cmp_v_pos, cmp_v_w1, cmp_v_w2,
           w_out, ffn_norm_g, w_gate, w_up, w_down):
    s = x.shape[0]
    dh = NSA_HEAD_DIM
    row = lambda v: v.reshape(1, -1)
    big = s >= 4096
    tm = 1024 if big else 256

    p = _norm_matmul(x, row(attn_norm_g), _regroup_w_in(w_in), 512 if big else 256,
                     PROJ_PAD // 2 if big else LANES)

    gla_out = _gla(p, gla_conv_w, gla_gate_w2, row(gla_gate_b), row(gla_norm_g), 256)

    qn, ksa, vsb, kwn, vwb = _nsa_prep(p, row(nsa_q_norm_g), row(nsa_ks_norm_g), row(nsa_kw_norm_g), 512)

    w1 = jnp.stack([cmp_k_w1, cmp_v_w1])
    w1s = (w1.reshape(2, 2, CMP_STRIDE, dh, CMP_HIDDEN).transpose(0, 2, 3, 1, 4)
           .reshape(2, CMP_STRIDE, dh, 2 * CMP_HIDDEN).astype(BF16))
    pos = jnp.stack([cmp_k_pos, cmp_v_pos]).reshape(2, 1, CMP_BLOCK * dh)
    w2 = jnp.stack([cmp_k_w2, cmp_v_w2]).astype(BF16)
    cmp = _compress(p, w1s, pos, w1, w2, row(nsa_kc_norm_g))

    nb = s // SEL_BLOCK
    nbp = -(-nb // LANES) * LANES
    o_cmp, bias = _cmp_attn(qn, cmp, _overlap_t(s, nbp), p, 128, min(SEL_TOPK, nb))
    o_cw = _win_attn(qn, kwn, vwb, p, o_cmp, 256)
    nsa_out = _sel_attn(qn, bias, ksa, vsb, p, o_cw, 256, 512)

    h = _out_proj(x, gla_out, nsa_out, w_out.astype(BF16), 512 if big else 256)
    return _ffn(h, row(ffn_norm_g), w_gate.astype(BF16), w_up.astype(BF16), w_down.astype(BF16),
                512 if big else 256, 512)


def kernel(x, attn_norm_g, w_in, gla_conv_w, gla_gate_w2, gla_gate_b, gla_norm_g, nsa_q_norm_g, nsa_kc_norm_g, nsa_ks_norm_g, nsa_kw_norm_g, cmp_k_pos, cmp_k_w1, cmp_k_w2, cmp_v_pos, cmp_v_w1, cmp_v_w2, w_out, ffn_norm_g, w_gate, w_up, w_down):
    assert x.shape[0] == 1 and attn_norm_g.shape[0] == 1
    y = _layer(x[0], attn_norm_g[0], w_in[0], gla_conv_w[0], gla_gate_w2[0], gla_gate_b[0],
               gla_norm_g[0], nsa_q_norm_g[0], nsa_kc_norm_g[0], nsa_ks_norm_g[0], nsa_kw_norm_g[0],
               cmp_k_pos[0], cmp_k_w1[0], cmp_k_w2[0], cmp_v_pos[0], cmp_v_w1[0], cmp_v_w2[0],
               w_out[0], ffn_norm_g[0], w_gate[0], w_up[0], w_down[0])
    return y[None]
```

```python
import functools

import jax
import jax.numpy as jnp
import numpy as np
from jax import lax
from jax.experimental import pallas as pl
from jax.experimental.pallas import tpu as pltpu

D_MODEL = 2048
GLA_HEADS = 4
GLA_DK = 128
GLA_DV = 256
GLA_GATE_RANK = 16
GLA_GATE_TAU = 16.0
GLA_CHUNK = 64
GLA_SUB = 16
CONV_WIDTH = 4

NSA_HEADS = 8
NSA_KV_HEADS = 2
NSA_GROUP = 4
NSA_HEAD_DIM = 128
CMP_BLOCK = 32
CMP_STRIDE = 16
CMP_HIDDEN = 128
SEL_BLOCK = 64
SEL_TOPK = 16
WINDOW = 512
D_FF = 5632
EPS = 1e-6
NEG = -1e30

GLA_QK = GLA_HEADS * GLA_DK
GLA_VW = GLA_HEADS * GLA_DV
GLA_QKV = 2 * GLA_QK + GLA_VW
NSA_QW = NSA_HEADS * NSA_HEAD_DIM
NSA_KVW = NSA_KV_HEADS * NSA_HEAD_DIM

LANES = 128
VMEM_LIMIT = 56 * 1024 * 1024

C_GQ = 0
C_GK = GLA_QK
C_GV = 2 * GLA_QK
C_GO = GLA_QKV
C_NQ = C_GO + GLA_VW
C_KC = C_NQ + NSA_QW
C_VC = C_KC + NSA_KVW
C_KS = C_VC + NSA_KVW
C_VS = C_KS + NSA_KVW
C_KW = C_VS + NSA_KVW
C_VW = C_KW + NSA_KVW
C_MISC = C_VW + NSA_KVW
PROJ_PAD = C_MISC + 2 * LANES
GATE_LANE = GLA_GATE_RANK

F32 = jnp.float32
BF16 = jnp.bfloat16


def _params(sem):
    return pltpu.CompilerParams(dimension_semantics=sem, vmem_limit_bytes=VMEM_LIMIT)


def _nt_dot(a, b):
    return lax.dot_general(a, b, (((1,), (1,)), ((), ())), preferred_element_type=F32)


def _sigmoid(x):
    return 1.0 / (1.0 + jnp.exp(-x))


def _rms(x, g):
    return x * lax.rsqrt(jnp.mean(x * x, axis=-1, keepdims=True) + EPS) * g


def _norm_matmul_kernel(x_ref, g_ref, w_ref, o_ref, xn_ref):
    @pl.when(pl.program_id(1) == 0)
    def _():
        xn_ref[...] = _rms(x_ref[...], g_ref[...]).astype(BF16)

    o_ref[...] = jnp.dot(xn_ref[...], w_ref[...], preferred_element_type=F32)


def _norm_matmul(x, g, w, tm, tn):
    s, d = x.shape
    n = w.shape[1]
    return pl.pallas_call(
        _norm_matmul_kernel,
        grid=(s // tm, n // tn),
        in_specs=[pl.BlockSpec((tm, d), lambda i, j: (i, 0)),
                  pl.BlockSpec((1, d), lambda i, j: (0, 0)),
                  pl.BlockSpec((d, tn), lambda i, j: (0, j))],
        out_specs=pl.BlockSpec((tm, tn), lambda i, j: (i, j)),
        out_shape=jax.ShapeDtypeStruct((s, n), F32),
        scratch_shapes=[pltpu.VMEM((tm, d), BF16)],
        compiler_params=_params(("arbitrary", "arbitrary")),
        name="in_proj",
    )(x, g, w)


def _gla_kernel(q_ref, qp_ref, k_ref, kp_ref, v_ref, vp_ref, lr_ref, go_ref,
                cwq_ref, cwk_ref, cwv_ref, w2_ref, gb_ref, ng_ref, o_ref, st_ref, acc_ref):
    t_rows = q_ref.shape[0]
    i = pl.program_id(1)
    first = i == 0

    @pl.when(first)
    def _():
        st_ref[...] = jnp.zeros_like(st_ref)

    def conv_silu(u_ref, p_ref, w_ref):
        u = u_ref[...]
        tail = jnp.where(first, 0.0, p_ref[...])
        w = w_ref[...]
        rows8 = lax.broadcasted_iota(jnp.int32, tail.shape, 0)
        acc = u * w[CONV_WIDTH - 1:CONV_WIDTH]
        for d in range(1, CONV_WIDTH):
            r = pltpu.roll(u, d, 0)
            head = jnp.where(rows8 < d, pltpu.roll(tail, d, 0), r[:8])
            sh = jnp.concatenate([head, r[8:]], axis=0)
            acc = acc + sh * w[CONV_WIDTH - 1 - d:CONV_WIDTH - d]
        return acc * _sigmoid(acc)

    q = conv_silu(q_ref, qp_ref, cwq_ref) * (GLA_DK ** -0.5)
    k = conv_silu(k_ref, kp_ref, cwk_ref)
    v = conv_silu(v_ref, vp_ref, cwv_ref)

    z = jnp.dot(lr_ref[:, :GLA_GATE_RANK], w2_ref[...], preferred_element_type=F32,
                precision=lax.Precision.HIGHEST) + gb_ref[...]
    log_a = (jnp.minimum(z, 0.0) - jnp.log(1.0 + jnp.exp(-jnp.abs(z)))) * (1.0 / GLA_GATE_TAU)
    row = lax.broadcasted_iota(jnp.int32, log_a.shape, 0)
    rc = row & (GLA_CHUNK - 1)
    b = log_a
    d = 1
    while d < GLA_CHUNK:
        b = b + jnp.where(rc >= d, pltpu.roll(b, d, 0), 0.0)
        d *= 2

    rs = row & (GLA_SUB - 1)
    od = jnp.sum(q * k, axis=-1, keepdims=True) * v
    for delta in range(1, GLA_SUB):
        ok = rs >= delta
        e = jnp.exp(jnp.where(ok, b - pltpu.roll(b, delta, 0), -jnp.inf))
        sc = jnp.sum(q * pltpu.roll(k, delta, 0) * e, axis=-1, keepdims=True)
        od = od + sc * pltpu.roll(v, delta, 0)
    acc_ref[...] = od

    srow = lax.broadcasted_iota(jnp.int32, (GLA_CHUNK, GLA_DK), 0)
    for c in range(t_rows // GLA_CHUNK):
        lo = c * GLA_CHUNK
        bc = b[lo:lo + GLA_CHUNK]
        qc = q[lo:lo + GLA_CHUNK]
        kc = k[lo:lo + GLA_CHUNK]
        vc = v[lo:lo + GLA_CHUNK].astype(BF16)
        st = st_ref[...]
        b_last = bc[GLA_CHUNK - 1:GLA_CHUNK]
        o_inter = _nt_dot((qc * jnp.exp(bc)).astype(BF16), st.astype(BF16))
        acc_ref[lo:lo + GLA_CHUNK, :] += o_inter
        for sub in range(1, GLA_CHUNK // GLA_SUB):
            r0 = sub * GLA_SUB
            ref_b = bc[r0:r0 + 1]
            qq = (qc[r0:r0 + GLA_SUB] * jnp.exp(bc[r0:r0 + GLA_SUB] - ref_b)).astype(BF16)
            kk = (kc * jnp.exp(jnp.where(srow < r0, ref_b - bc, -jnp.inf))).astype(BF16)
            a = _nt_dot(qq, kk)
            acc_ref[lo + r0:lo + r0 + GLA_SUB, :] += jnp.dot(
                a.astype(BF16), vc, preferred_element_type=F32)
        kt = (kc * jnp.exp(b_last - bc)).astype(BF16)
        upd = lax.dot_general(vc, kt, (((0,), (0,)), ((), ())), preferred_element_type=F32)
        st_ref[...] = st * jnp.exp(b_last) + upd

    o = acc_ref[...]
    go = go_ref[...]
    o_ref[...] = _rms(o, ng_ref[...]) * (go * _sigmoid(go))


def _gla(p, conv_w, w2, gb, ng, tile):
    s = p.shape[0]
    t8 = tile // 8

    def prev(col):
        return lambda h, i: (jnp.maximum(i * t8 - 1, 0), col(h))

    qcol = lambda h: C_GQ // GLA_DK + h
    kcol = lambda h: C_GK // GLA_DK + h
    vcol = lambda h: C_GV // GLA_DV + h
    return pl.pallas_call(
        _gla_kernel,
        grid=(GLA_HEADS, s // tile),
        in_specs=[
            pl.BlockSpec((tile, GLA_DK), lambda h, i: (i, qcol(h))),
            pl.BlockSpec((8, GLA_DK), prev(qcol)),
            pl.BlockSpec((tile, GLA_DK), lambda h, i: (i, kcol(h))),
            pl.BlockSpec((8, GLA_DK), prev(kcol)),
            pl.BlockSpec((tile, GLA_DV), lambda h, i: (i, vcol(h))),
            pl.BlockSpec((8, GLA_DV), prev(vcol)),
            pl.BlockSpec((tile, LANES), lambda h, i: (i, C_MISC // LANES)),
            pl.BlockSpec((tile, GLA_DV), lambda h, i: (i, C_GO // GLA_DV + h)),
            pl.BlockSpec((CONV_WIDTH, GLA_DK), lambda h, i: (0, qcol(h))),
            pl.BlockSpec((CONV_WIDTH, GLA_DK), lambda h, i: (0, kcol(h))),
            pl.BlockSpec((CONV_WIDTH, GLA_DV), lambda h, i: (0, vcol(h))),
            pl.BlockSpec((GLA_GATE_RANK, GLA_DK), lambda h, i: (0, h)),
            pl.BlockSpec((1, GLA_DK), lambda h, i: (0, h)),
            pl.BlockSpec((1, GLA_DV), lambda h, i: (0, 0)),
        ],
        out_specs=pl.BlockSpec((tile, GLA_DV), lambda h, i: (i, h)),
        out_shape=jax.ShapeDtypeStruct((s, GLA_VW), F32),
        scratch_shapes=[pltpu.VMEM((GLA_DV, GLA_DK), F32), pltpu.VMEM((tile, GLA_DV), F32)],
        compiler_params=_params(("arbitrary", "arbitrary")),
        name="gla",
    )(p, p, p, p, p, p, p, p, conv_w, conv_w, conv_w, w2, gb, ng)


def _nsa_prep_kernel(q_ref, ks_ref, vs_ref, kw_ref, vw_ref, qg_ref, ksg_ref, kwg_ref,
                     qn_ref, ksa_ref, vsa_ref, kwn_ref, vwb_ref):
    t_rows = q_ref.shape[0]
    dh = NSA_HEAD_DIM
    scale = dh ** -0.5
    for h in range(NSA_HEADS):
        sl = slice(h * dh, (h + 1) * dh)
        qn_ref[:, sl] = (_rms(q_ref[:, sl], qg_ref[...]) * scale).astype(BF16)
    pos = pl.program_id(0) * t_rows + lax.broadcasted_iota(jnp.int32, (t_rows, LANES), 0)
    lane = lax.broadcasted_iota(jnp.int32, (t_rows, LANES), 1)
    onehot = jnp.where(lane == ((pos // SEL_BLOCK) & (LANES - 1)), 1.0, 0.0).astype(BF16)
    for h in range(NSA_KV_HEADS):
        sl = slice(h * dh, (h + 1) * dh)
        ksa_ref[:, 2 * h * dh:(2 * h + 1) * dh] = _rms(ks_ref[:, sl], ksg_ref[...]).astype(BF16)
        ksa_ref[:, (2 * h + 1) * dh:(2 * h + 2) * dh] = onehot
        kwn_ref[:, sl] = _rms(kw_ref[:, sl], kwg_ref[...]).astype(BF16)
        vsa_ref[:, 2 * h * dh:(2 * h + 1) * dh] = vs_ref[:, sl].astype(BF16)
        vsa_ref[:, (2 * h + 1) * dh:(2 * h + 2) * dh] = jnp.ones((t_rows, dh), BF16)
    vwb_ref[...] = vw_ref[...].astype(BF16)


def _nsa_prep(p, qg, ksg, kwg, tile):
    s = p.shape[0]
    kv = NSA_KVW
    col = lambda c, w: (lambda i: (i, c // w))
    row = lambda w: pl.BlockSpec((tile, w), lambda i: (i, 0))
    gain = pl.BlockSpec((1, NSA_HEAD_DIM), lambda i: (0, 0))
    return pl.pallas_call(
        _nsa_prep_kernel,
        grid=(s // tile,),
        in_specs=[pl.BlockSpec((tile, NSA_QW), col(C_NQ, NSA_QW)),
                  pl.BlockSpec((tile, kv), col(C_KS, kv)),
                  pl.BlockSpec((tile, kv), col(C_VS, kv)),
                  pl.BlockSpec((tile, kv), col(C_KW, kv)),
                  pl.BlockSpec((tile, kv), col(C_VW, kv)),
                  gain, gain, gain],
        out_specs=[row(NSA_QW), row(2 * kv), row(2 * kv), row(kv), row(kv)],
        out_shape=[jax.ShapeDtypeStruct((s, NSA_QW), BF16),
                   jax.ShapeDtypeStruct((s, 2 * kv), BF16),
                   jax.ShapeDtypeStruct((s, 2 * kv), BF16),
                   jax.ShapeDtypeStruct((s, kv), BF16),
                   jax.ShapeDtypeStruct((s, kv), BF16)],
        compiler_params=_params(("arbitrary",)),
        name="nsa_prep",
    )(p, p, p, p, p, qg, ksg, kwg)


def _compress_kernel(u_ref, w1_ref, pos_ref, w1f_ref, w2_ref, g_ref, o_ref):
    n = o_ref.shape[1]
    half = CMP_BLOCK // CMP_STRIDE
    assert half == 2
    acc = jnp.zeros((n, 2 * CMP_HIDDEN), F32)
    for l in range(CMP_STRIDE):
        x = u_ref[pl.ds(l, n, stride=CMP_STRIDE), :].astype(BF16)
        acc = acc + jnp.dot(x, w1_ref[0, l], preferred_element_type=F32)
    posb = jnp.dot(jnp.broadcast_to(pos_ref[0], (8, pos_ref.shape[2])), w1f_ref[0],
                   preferred_element_type=F32, precision=lax.Precision.HIGHEST)[0:1]
    hid = acc[:, :CMP_HIDDEN] + pltpu.roll(acc[:, CMP_HIDDEN:], n - 1, 0) + posb
    hid = hid * _sigmoid(hid)
    out = jnp.dot(hid.astype(BF16), w2_ref[0], preferred_element_type=F32)
    is_k = pl.program_id(0) < NSA_KV_HEADS
    out = jnp.where(is_k, _rms(out, g_ref[...]), out)
    row = lax.broadcasted_iota(jnp.int32, out.shape, 0)
    o_ref[0] = jnp.where(row < n - 1, out, 0.0).astype(BF16)


def _compress(p, w1s, pos, w1f, w2, g):
    s = p.shape[0]
    n = s // CMP_STRIDE
    dh = NSA_HEAD_DIM
    return pl.pallas_call(
        _compress_kernel,
        grid=(2 * NSA_KV_HEADS,),
        in_specs=[pl.BlockSpec((s, dh), lambda j: (0, C_KC // dh + j)),
                  pl.BlockSpec((1, CMP_STRIDE, dh, 2 * CMP_HIDDEN), lambda j: (j // 2, 0, 0, 0)),
                  pl.BlockSpec((1, 1, CMP_BLOCK * dh), lambda j: (j // 2, 0, 0)),
                  pl.BlockSpec((1, CMP_BLOCK * dh, CMP_HIDDEN), lambda j: (j // 2, 0, 0)),
                  pl.BlockSpec((1, CMP_HIDDEN, dh), lambda j: (j // 2, 0, 0)),
                  pl.BlockSpec((1, dh), lambda j: (0, 0))],
        out_specs=pl.BlockSpec((1, n, dh), lambda j: (j, 0, 0)),
        out_shape=jax.ShapeDtypeStruct((2 * NSA_KV_HEADS, n, dh), BF16),
        compiler_params=_params(("arbitrary",)),
        name="compress",
    )(p, w1s, pos, w1f, w2, g)


def _stack_heads(q):
    dh = NSA_HEAD_DIM
    return jnp.concatenate([q[:, g * dh:(g + 1) * dh] for g in range(NSA_GROUP)], axis=0)


def _unstack_heads(o, tq):
    return jnp.concatenate([o[g * tq:(g + 1) * tq] for g in range(NSA_GROUP)], axis=1)


def _branch_gate(misc, branch, tq):
    sg = _sigmoid(misc)
    base = GATE_LANE + branch * NSA_GROUP
    return jnp.concatenate([sg[:, base + g:base + g + 1] for g in range(NSA_GROUP)], axis=0)


def _cmp_attn_kernel(q_ref, kc_ref, vc_ref, ovl_ref, misc_ref, o_ref, bias_ref, *, n_sel):
    tq = q_ref.shape[0]
    nc = kc_ref.shape[1]
    nbp = ovl_ref.shape[0]
    s0 = pl.program_id(1) * tq
    qs = _stack_heads(q_ref[...])
    sc = _nt_dot(qs, kc_ref[0])
    row = lax.broadcasted_iota(jnp.int32, sc.shape, 0)
    col = lax.broadcasted_iota(jnp.int32, sc.shape, 1)
    t = s0 + (row & (tq - 1))
    valid = col * CMP_STRIDE + (CMP_BLOCK - 1) <= t
    sm = jnp.where(valid, sc, NEG)
    m = jnp.max(sm, axis=-1, keepdims=True)
    e = jnp.where(valid, jnp.exp(sm - m), 0.0)
    pc = e * (1.0 / jnp.maximum(jnp.sum(e, axis=-1, keepdims=True), 1e-30))
    o = jnp.dot(pc.astype(BF16), vc_ref[0], preferred_element_type=F32)
    o_ref[...] = _unstack_heads(o * _branch_gate(misc_ref[...], 0, tq), tq)

    ps = pc[0:tq]
    for g in range(1, NSA_GROUP):
        ps = ps + pc[g * tq:(g + 1) * tq]
    ovl = ovl_ref[...]
    hi = ps.astype(BF16)
    r1 = ps - hi.astype(F32)
    mid = r1.astype(BF16)
    lo = (r1 - mid.astype(F32)).astype(BF16)
    imp = _nt_dot(ovl, hi) + _nt_dot(ovl, mid) + _nt_dot(ovl, lo)

    jb = lax.broadcasted_iota(jnp.int32, (nbp, tq), 0)
    jt = (s0 + lax.broadcasted_iota(jnp.int32, (nbp, tq), 1)) // SEL_BLOCK
    forced = (jb == 0) | (jb == jt) | (jb == jt - 1)
    cur = jnp.where(jb > jt, -jnp.inf, jnp.where(forced, jnp.inf, imp))
    sel = jnp.zeros((nbp, tq), F32)
    for _ in range(n_sel):
        mx = jnp.max(cur, axis=0, keepdims=True)
        idx = jnp.min(jnp.where(cur == mx, jb, nbp), axis=0, keepdims=True)
        pick = jb == idx
        sel = jnp.where(pick, 1.0, sel)
        cur = jnp.where(pick, -jnp.inf, cur)
    bias_t = jnp.where((sel > 0.0) & (jb <= jt), 0.0, NEG)
    bias_ref[...] = bias_t.T.astype(BF16)


def _cmp_attn(qn, cmp, ovl_t, p, tq, n_sel):
    s = qn.shape[0]
    nc = cmp.shape[1]
    nbp = ovl_t.shape[0]
    gw = NSA_GROUP * NSA_HEAD_DIM
    return pl.pallas_call(
        functools.partial(_cmp_attn_kernel, n_sel=n_sel),
        grid=(NSA_KV_HEADS, s // tq),
        in_specs=[pl.BlockSpec((tq, gw), lambda h, i: (i, h)),
                  pl.BlockSpec((1, nc, NSA_HEAD_DIM), lambda h, i: (h, 0, 0)),
                  pl.BlockSpec((1, nc, NSA_HEAD_DIM), lambda h, i: (NSA_KV_HEADS + h, 0, 0)),
                  pl.BlockSpec((nbp, nc), lambda h, i: (0, 0)),
                  pl.BlockSpec((tq, LANES), lambda h, i: (i, C_MISC // LANES + h))],
        out_specs=[pl.BlockSpec((tq, gw), lambda h, i: (i, h)),
                   pl.BlockSpec((tq, nbp), lambda h, i: (i, h))],
        out_shape=[jax.ShapeDtypeStruct((s, NSA_QW), F32),
                   jax.ShapeDtypeStruct((s, NSA_KV_HEADS * nbp), BF16)],
        compiler_params=_params(("arbitrary", "arbitrary")),
        name="cmp_attn",
    )(qn, cmp, cmp, ovl_t, p)


def _win_attn_kernel(q_ref, k_ref, v_ref, misc_ref, prev_ref, o_ref, *, span):
    tq = q_ref.shape[0]
    s0 = pl.program_id(1) * tq
    start = pl.multiple_of(jnp.maximum(s0 - WINDOW, 0), tq)
    qs = _stack_heads(q_ref[...])
    sc = _nt_dot(qs, k_ref[pl.ds(start, span), :])
    row = lax.broadcasted_iota(jnp.int32, sc.shape, 0)
    col = lax.broadcasted_iota(jnp.int32, sc.shape, 1)
    dist = s0 + (row & (tq - 1)) - (start + col)
    sm = jnp.where((dist >= 0) & (dist < WINDOW), sc, NEG)
    m = jnp.max(sm, axis=-1, keepdims=True)
    e = jnp.exp(sm - m)
    o = jnp.dot(e.astype(BF16), v_ref[pl.ds(start, span), :], preferred_element_type=F32)
    o = o * (_branch_gate(misc_ref[...], 2, tq) / jnp.sum(e, axis=-1, keepdims=True))
    o_ref[...] = prev_ref[...] + _unstack_heads(o, tq)


def _win_attn(qn, kwn, vwb, p, prev, tq):
    s = qn.shape[0]
    gw = NSA_GROUP * NSA_HEAD_DIM
    span = min(s, WINDOW + tq)
    return pl.pallas_call(
        functools.partial(_win_attn_kernel, span=span),
        grid=(NSA_KV_HEADS, s // tq),
        in_specs=[pl.BlockSpec((tq, gw), lambda h, i: (i, h)),
                  pl.BlockSpec((s, NSA_HEAD_DIM), lambda h, i: (0, h)),
                  pl.BlockSpec((s, NSA_HEAD_DIM), lambda h, i: (0, h)),
                  pl.BlockSpec((tq, LANES), lambda h, i: (i, C_MISC // LANES + h)),
                  pl.BlockSpec((tq, gw), lambda h, i: (i, h))],
        out_specs=pl.BlockSpec((tq, gw), lambda h, i: (i, h)),
        out_shape=jax.ShapeDtypeStruct((s, NSA_QW), F32),
        compiler_params=_params(("arbitrary", "arbitrary")),
        name="win_attn",
    )(qn, kwn, vwb, p, prev)


def _sel_attn_kernel(q_ref, bias_ref, ka_ref, va_ref, misc_ref, prev_ref, o_ref,
                     lhs_ref, m_ref, acc_ref, sa_ref, sb_ref, *, tk):
    tq = q_ref.shape[0]
    dh = NSA_HEAD_DIM
    n_super = lhs_ref.shape[0]
    tiles_per_super = LANES * SEL_BLOCK // tk
    s0 = pl.program_id(1) * tq
    for sup in range(n_super):
        for g in range(NSA_GROUP):
            lhs_ref[sup, g * tq:(g + 1) * tq, 0:dh] = q_ref[:, g * dh:(g + 1) * dh]
            lhs_ref[sup, g * tq:(g + 1) * tq, dh:2 * dh] = bias_ref[:, sup * LANES:(sup + 1) * LANES]
    m_ref[...] = jnp.full_like(m_ref, -jnp.inf)
    acc_ref[...] = jnp.zeros_like(acc_ref)

    def scores(kt, sc_ref):
        k0 = pl.multiple_of(kt * tk, tk)
        k = ka_ref[pl.ds(k0, tk), :]
        sup = kt // tiles_per_super
        for g in range(NSA_GROUP):
            rows = slice(g * tq, (g + 1) * tq)
            sc_ref[rows, :] = _nt_dot(lhs_ref[sup, rows, :], k)

    def accumulate(kt, sc_ref, diagonal):
        k0 = pl.multiple_of(kt * tk, tk)
        v = va_ref[pl.ds(k0, tk), :]
        for g in range(NSA_GROUP):
            rows = slice(g * tq, (g + 1) * tq)
            sc = sc_ref[rows, :]
            if diagonal:
                row = lax.broadcasted_iota(jnp.int32, sc.shape, 0)
                col = lax.broadcasted_iota(jnp.int32, sc.shape, 1)
                sc = jnp.where(k0 + col <= s0 + row, sc, NEG)
            m_old = m_ref[rows, :]
            m_new = jnp.maximum(m_old, jnp.max(sc, axis=-1, keepdims=True))
            alpha = jnp.exp(m_old - m_new)
            pr = jnp.exp(sc - jnp.tile(m_new, (1, tk // LANES)))
            acc_ref[rows, :] = (jnp.tile(alpha, (1, 2)) * acc_ref[rows, :]
                                + jnp.dot(pr.astype(BF16), v, preferred_element_type=F32))
            m_ref[rows, :] = m_new

    n_full = s0 // tk

    def body(j, carry):
        scores(2 * j + 1, sb_ref)
        accumulate(2 * j, sa_ref, False)
        scores(2 * j + 2, sa_ref)
        accumulate(2 * j + 1, sb_ref, False)
        return carry

    scores(0, sa_ref)
    lax.fori_loop(0, n_full // 2, body, 0)

    @pl.when(n_full % 2 == 0)
    def _():
        accumulate(n_full, sa_ref, True)

    @pl.when(n_full % 2 == 1)
    def _():
        scores(n_full, sb_ref)
        accumulate(n_full - 1, sa_ref, False)
        accumulate(n_full, sb_ref, True)

    acc = acc_ref[...]
    o = acc[:, :dh] * (_branch_gate(misc_ref[...], 1, tq) / acc[:, dh:])
    o_ref[...] = prev_ref[...] + _unstack_heads(o, tq)


def _sel_attn(qn, bias, ksa, vsa, p, prev, tq, tk):
    s = qn.shape[0]
    dh = NSA_HEAD_DIM
    gw = NSA_GROUP * dh
    nbp = bias.shape[1] // NSA_KV_HEADS
    n_super = nbp // LANES
    assert tk % tq == 0 and (LANES * SEL_BLOCK) % tk == 0 and s % tk == 0
    resident = lambda: pl.BlockSpec((s, 2 * dh), lambda h, i: (0, h), pipeline_mode=pl.Buffered(1))
    return pl.pallas_call(
        functools.partial(_sel_attn_kernel, tk=tk),
        grid=(NSA_KV_HEADS, s // tq),
        in_specs=[pl.BlockSpec((tq, gw), lambda h, i: (i, h)),
                  pl.BlockSpec((tq, nbp), lambda h, i: (i, h)),
                  resident(), resident(),
                  pl.BlockSpec((tq, LANES), lambda h, i: (i, C_MISC // LANES + h)),
                  pl.BlockSpec((tq, gw), lambda h, i: (i, h))],
        out_specs=pl.BlockSpec((tq, gw), lambda h, i: (i, h)),
        out_shape=jax.ShapeDtypeStruct((s, NSA_QW), F32),
        scratch_shapes=[pltpu.VMEM((n_super, NSA_GROUP * tq, 2 * dh), BF16),
                        pltpu.VMEM((NSA_GROUP * tq, LANES), F32),
                        pltpu.VMEM((NSA_GROUP * tq, 2 * dh), F32),
                        pltpu.VMEM((NSA_GROUP * tq, tk), F32),
                        pltpu.VMEM((NSA_GROUP * tq, tk), F32)],
        compiler_params=_params(("arbitrary", "arbitrary")),
        name="sel_attn",
    )(qn, bias, ksa, vsa, p, prev)


def _out_proj_kernel(x_ref, a_ref, b_ref, wa_ref, wb_ref, o_ref):
    o_ref[...] = (x_ref[...]
                  + jnp.dot(a_ref[...].astype(BF16), wa_ref[...], preferred_element_type=F32)
                  + jnp.dot(b_ref[...].astype(BF16), wb_ref[...], preferred_element_type=F32))


def _out_proj(x, a, b, w, tm):
    s, d = x.shape
    ka, kb = a.shape[1], b.shape[1]
    assert ka == kb
    return pl.pallas_call(
        _out_proj_kernel,
        grid=(s // tm,),
        in_specs=[pl.BlockSpec((tm, d), lambda i: (i, 0)),
                  pl.BlockSpec((tm, ka), lambda i: (i, 0)),
                  pl.BlockSpec((tm, kb), lambda i: (i, 0)),
                  pl.BlockSpec((ka, d), lambda i: (0, 0)),
                  pl.BlockSpec((kb, d), lambda i: (1, 0))],
        out_specs=pl.BlockSpec((tm, d), lambda i: (i, 0)),
        out_shape=jax.ShapeDtypeStruct((s, d), F32),
        compiler_params=_params(("arbitrary",)),
        name="out_proj",
    )(x, a, b, w, w)


def _ffn_kernel(h_ref, g_ref, wg_ref, wu_ref, wd_ref, o_ref, hn_ref):
    @pl.when(pl.program_id(1) == 0)
    def _():
        h = h_ref[...]
        hn_ref[...] = _rms(h, g_ref[...]).astype(BF16)
        o_ref[...] = h

    hn = hn_ref[...]
    a = jnp.dot(hn, wg_ref[...], preferred_element_type=F32)
    u = jnp.dot(hn, wu_ref[...], preferred_element_type=F32)
    z = (a * _sigmoid(a) * u).astype(BF16)
    o_ref[...] += jnp.dot(z, wd_ref[...], preferred_element_type=F32)


def _ffn(h, g, wg, wu, wd, tm, tf):
    s, d = h.shape
    f = wg.shape[1]
    return pl.pallas_call(
        _ffn_kernel,
        grid=(s // tm, f // tf),
        in_specs=[pl.BlockSpec((tm, d), lambda i, j: (i, 0)),
                  pl.BlockSpec((1, d), lambda i, j: (0, 0)),
                  pl.BlockSpec((d, tf), lambda i, j: (0, j)),
                  pl.BlockSpec((d, tf), lambda i, j: (0, j)),
                  pl.BlockSpec((tf, d), lambda i, j: (j, 0))],
        out_specs=pl.BlockSpec((tm, d), lambda i, j: (i, 0)),
        out_shape=jax.ShapeDtypeStruct((s, d), F32),
        scratch_shapes=[pltpu.VMEM((tm, d), BF16)],
        compiler_params=_params(("arbitrary", "arbitrary")),
        name="ffn",
    )(h, g, wg, wu, wd)


def _regroup_w_in(w_in):
    o_lr = GLA_QKV
    o_go = o_lr + GLA_GATE_RANK
    o_ng = o_go + GLA_VW + NSA_QW + 6 * NSA_KVW
    d = w_in.shape[0]
    lr = w_in[:, o_lr:o_go]
    ng = w_in[:, o_ng:o_ng + 3 * NSA_HEADS].reshape(d, NSA_KV_HEADS, NSA_GROUP, 3)
    ng = ng.transpose(0, 1, 3, 2).reshape(d, NSA_KV_HEADS, 3 * NSA_GROUP)
    pad = lambda n: jnp.zeros((d, n), w_in.dtype)
    misc0 = jnp.concatenate([lr, ng[:, 0], pad(LANES - GLA_GATE_RANK - 3 * NSA_GROUP)], axis=1)
    misc1 = jnp.concatenate([pad(GLA_GATE_RANK), ng[:, 1], pad(LANES - GLA_GATE_RANK - 3 * NSA_GROUP)], axis=1)
    return jnp.concatenate([w_in[:, :o_lr], w_in[:, o_go:o_ng], misc0, misc1], axis=1).astype(BF16)


def _overlap_t(s, nbp):
    n = s // CMP_STRIDE
    c0 = np.arange(n)[None, :] * CMP_STRIDE
    s0 = np.arange(nbp)[:, None] * SEL_BLOCK
    ov = np.clip(np.minimum(c0 + CMP_BLOCK, s0 + SEL_BLOCK) - np.maximum(c0, s0), 0, None) / CMP_STRIDE
    ov[:, n - 1] = 0.0
    ov[s // SEL_BLOCK:] = 0.0
    return jnp.asarray(ov, BF16)


def _layer(x, attn_norm_g, w_in, gla_conv_w, gla_gate_w2, gla_gate_b, gla_norm_g,
           nsa_q_norm_g, nsa_kc_norm_g, nsa_ks_norm_g, nsa_kw_norm_g,
           cmp_k_pos, cmp_k_w1, cmp_k_w2, cmp_v_pos, cmp_v_w1, cmp_v_w2,
           w_out, ffn_norm_g, w_gate, w_up, w_down):
    s = x.shape[0]
    dh = NSA_HEAD_DIM
    row = lambda v: v.reshape(1, -1)
    big = s >= 4096
    tm = 1024 if big else 256

    p = _norm_matmul(x, row(attn_norm_g), _regroup_w_in(w_in), 512 if big else 256,
                     PROJ_PAD // 2 if big else LANES)

    gla_out = _gla(p, gla_conv_w, gla_gate_w2, row(gla_gate_b), row(gla_norm_g), 256)

    qn, ksa, vsb, kwn, vwb = _nsa_prep(p, row(nsa_q_norm_g), row(nsa_ks_norm_g), row(nsa_kw_norm_g), 512)

    w1 = jnp.stack([cmp_k_w1, cmp_v_w1])
    w1s = (w1.reshape(2, 2, CMP_STRIDE, dh, CMP_HIDDEN).transpose(0, 2, 3, 1, 4)
           .reshape(2, CMP_STRIDE, dh, 2 * CMP_HIDDEN).astype(BF16))
    pos = jnp.stack([cmp_k_pos, cmp_v_pos]).reshape(2, 1, CMP_BLOCK * dh)
    w2 = jnp.stack([cmp_k_w2, cmp_v_w2]).astype(BF16)
    cmp = _compress(p, w1s, pos, w1, w2, row(nsa_kc_norm_g))

    nb = s // SEL_BLOCK
    nbp = -(-nb // LANES) * LANES
    o_cmp, bias = _cmp_attn(qn, cmp, _overlap_t(s, nbp), p, 128, min(SEL_TOPK, nb))
    o_cw = _win_attn(qn, kwn, vwb, p, o_cmp, 256)
    nsa_out = _sel_attn(qn, bias, ksa, vsb, p, o_cw, 256, 512)

    h = _out_proj(x, gla_out, nsa_out, w_out.astype(BF16), 512 if big else 256)
    return _ffn(h, row(ffn_norm_g), w_gate.astype(BF16), w_up.astype(BF16), w_down.astype(BF16),
                512 if big else 256, 512)


def kernel(x, attn_norm_g, w_in, gla_conv_w, gla_gate_w2, gla_gate_b, gla_norm_g, nsa_q_norm_g, nsa_kc_norm_g, nsa_ks_norm_g, nsa_kw_norm_g, cmp_k_pos, cmp_k_w1, cmp_k_w2, cmp_v_pos, cmp_v_w1, cmp_v_w2, w_out, ffn_norm_g, w_gate, w_up, w_down):
    assert x.shape[0] == 1 and attn_norm_g.shape[0] == 1
    y = _layer(x[0], attn_norm_g[0], w_in[0], gla_conv_w[0], gla_gate_w2[0], gla_gate_b[0],
               gla_norm_g[0], nsa_q_norm_g[0], nsa_kc_norm_g[0], nsa_ks_norm_g[0], nsa_kw_norm_g[0],
               cmp_k_pos[0], cmp_k_w1[0], cmp_k_w2[0], cmp_v_pos[0], cmp_v_w1[0], cmp_v_w2[0],
               w_out[0], ffn_norm_g[0], w_gate[0], w_up[0], w_down[0])
    return y[None]
```

```python
import functools

import jax
import jax.numpy as jnp
import numpy as np
from jax import lax
from jax.experimental import pallas as pl
from jax.experimental.pallas import tpu as pltpu

D_MODEL = 2048
GLA_HEADS = 4
GLA_DK = 128
GLA_DV = 256
GLA_GATE_RANK = 16
GLA_GATE_TAU = 16.0
GLA_CHUNK = 64
GLA_SUB = 16
CONV_WIDTH = 4

NSA_HEADS = 8
NSA_KV_HEADS = 2
NSA_GROUP = 4
NSA_HEAD_DIM = 128
CMP_BLOCK = 32
CMP_STRIDE = 16
CMP_HIDDEN = 128
SEL_BLOCK = 64
SEL_TOPK = 16
WINDOW = 512
D_FF = 5632
EPS = 1e-6
NEG = -1e30

GLA_QK = GLA_HEADS * GLA_DK
GLA_VW = GLA_HEADS * GLA_DV
GLA_QKV = 2 * GLA_QK + GLA_VW
NSA_QW = NSA_HEADS * NSA_HEAD_DIM
NSA_KVW = NSA_KV_HEADS * NSA_HEAD_DIM

LANES = 128
VMEM_LIMIT = 56 * 1024 * 1024

C_GQ = 0
C_GK = GLA_QK
C_GV = 2 * GLA_QK
C_GO = GLA_QKV
C_NQ = C_GO + GLA_VW
C_KC = C_NQ + NSA_QW
C_VC = C_KC + NSA_KVW
C_KS = C_VC + NSA_KVW
C_VS = C_KS + NSA_KVW
C_KW = C_VS + NSA_KVW
C_VW = C_KW + NSA_KVW
C_MISC = C_VW + NSA_KVW
PROJ_PAD = C_MISC + 2 * LANES
GATE_LANE = GLA_GATE_RANK

F32 = jnp.float32
BF16 = jnp.bfloat16


def _params(sem):
    return pltpu.CompilerParams(dimension_semantics=sem, vmem_limit_bytes=VMEM_LIMIT)


def _nt_dot(a, b):
    return lax.dot_general(a, b, (((1,), (1,)), ((), ())), preferred_element_type=F32)


def _sigmoid(x):
    return 1.0 / (1.0 + jnp.exp(-x))


def _rms(x, g):
    return x * lax.rsqrt(jnp.mean(x * x, axis=-1, keepdims=True) + EPS) * g


def _norm_matmul_kernel(x_ref, g_ref, w_ref, o_ref, xn_ref):
    @pl.when(pl.program_id(1) == 0)
    def _():
        xn_ref[...] = _rms(x_ref[...], g_ref[...]).astype(BF16)

    o_ref[...] = jnp.dot(xn_ref[...], w_ref[...], preferred_element_type=F32)


def _norm_matmul(x, g, w, tm, tn):
    s, d = x.shape
    n = w.shape[1]
    return pl.pallas_call(
        _norm_matmul_kernel,
        grid=(s // tm, n // tn),
        in_specs=[pl.BlockSpec((tm, d), lambda i, j: (i, 0)),
                  pl.BlockSpec((1, d), lambda i, j: (0, 0)),
                  pl.BlockSpec((d, tn), lambda i, j: (0, j))],
        out_specs=pl.BlockSpec((tm, tn), lambda i, j: (i, j)),
        out_shape=jax.ShapeDtypeStruct((s, n), F32),
        scratch_shapes=[pltpu.VMEM((tm, d), BF16)],
        compiler_params=_params(("arbitrary", "arbitrary")),
        name="in_proj",
    )(x, g, w)


def _gla_kernel(q_ref, qp_ref, k_ref, kp_ref, v_ref, vp_ref, lr_ref, go_ref,
                cwq_ref, cwk_ref, cwv_ref, w2_ref, gb_ref, ng_ref, o_ref,
                st_ref, sq_ref, sk_ref, sv_ref):
    t_rows = q_ref.shape[0]
    pad = GLA_SUB
    first = pl.program_id(1) == 0

    @pl.when(first)
    def _():
        st_ref[...] = jnp.zeros_like(st_ref)

    def conv_silu(u_ref, p_ref, w_ref, s_ref):
        u = u_ref[...]
        s_ref[pad - 8:pad, :] = jnp.where(first, 0.0, p_ref[...])
        s_ref[pad:, :] = u
        w = w_ref[...]
        acc = u * w[CONV_WIDTH - 1:CONV_WIDTH]
        for d in range(1, CONV_WIDTH):
            acc = acc + s_ref[pl.ds(pad - d, t_rows), :] * w[CONV_WIDTH - 1 - d:CONV_WIDTH - d]
        return acc * _sigmoid(acc)

    q = conv_silu(q_ref, qp_ref, cwq_ref, sq_ref) * (GLA_DK ** -0.5)
    k = conv_silu(k_ref, kp_ref, cwk_ref, sk_ref)
    v = conv_silu(v_ref, vp_ref, cwv_ref, sv_ref)

    z = jnp.dot(lr_ref[:, :GLA_GATE_RANK], w2_ref[...], preferred_element_type=F32,
                precision=lax.Precision.HIGHEST) + gb_ref[...]
    log_a = (jnp.minimum(z, 0.0) - jnp.log(1.0 + jnp.exp(-jnp.abs(z)))) * (1.0 / GLA_GATE_TAU)
    row = lax.broadcasted_iota(jnp.int32, log_a.shape, 0)
    lane = lax.broadcasted_iota(jnp.int32, log_a.shape, 1)
    rc = row & (GLA_CHUNK - 1)
    b = log_a
    d = 1
    while d < GLA_CHUNK:
        b = b + jnp.where(rc >= d, pltpu.roll(b, d, 0), 0.0)
        d *= 2

    zero_pad = jnp.zeros((pad, GLA_DK), F32)
    sq_ref[0:pad, :] = zero_pad
    sk_ref[0:pad, :] = zero_pad
    sq_ref[pad:, :] = b
    sk_ref[pad:, :] = k
    rs = row & (GLA_SUB - 1)
    dl = rc - lane
    band = jnp.where(dl == 0, jnp.sum(q * k, axis=-1, keepdims=True), 0.0)
    for delta in range(1, GLA_SUB):
        e = jnp.exp(jnp.where(rs >= delta, b - sq_ref[pl.ds(pad - delta, t_rows), :], -jnp.inf))
        sc = jnp.sum(q * sk_ref[pl.ds(pad - delta, t_rows), :] * e, axis=-1, keepdims=True)
        band = jnp.where(dl == delta, sc, band)

    srow = lax.broadcasted_iota(jnp.int32, (GLA_CHUNK, GLA_DK), 0)
    zk = jnp.zeros((LANES - GLA_CHUNK, GLA_DK), BF16)
    zv = jnp.zeros((LANES - GLA_CHUNK, GLA_DV), BF16)
    za = jnp.zeros((GLA_SUB, LANES), F32)
    gain = ng_ref[...]
    for c in range(t_rows // GLA_CHUNK):
        lo = c * GLA_CHUNK
        bc = b[lo:lo + GLA_CHUNK]
        qc = q[lo:lo + GLA_CHUNK]
        kc = k[lo:lo + GLA_CHUNK]
        vc = v[lo:lo + GLA_CHUNK].astype(BF16)
        st = st_ref[...]
        b_last = bc[GLA_CHUNK - 1:GLA_CHUNK]
        o_inter = _nt_dot((qc * jnp.exp(bc)).astype(BF16), st.astype(BF16))
        blocks = [za]
        for sub in range(1, GLA_CHUNK // GLA_SUB):
            r0 = sub * GLA_SUB
            ref_b = bc[r0:r0 + 1]
            qq = (qc[r0:r0 + GLA_SUB] * jnp.exp(bc[r0:r0 + GLA_SUB] - ref_b)).astype(BF16)
            kk = (kc * jnp.exp(jnp.where(srow < r0, ref_b - bc, -jnp.inf))).astype(BF16)
            blocks.append(_nt_dot(qq, jnp.concatenate([kk, zk], axis=0)))
        scores = band[lo:lo + GLA_CHUNK] + jnp.concatenate(blocks, axis=0)
        o = o_inter + jnp.dot(scores.astype(BF16), jnp.concatenate([vc, zv], axis=0),
                              preferred_element_type=F32)
        go = go_ref[lo:lo + GLA_CHUNK, :]
        o_ref[lo:lo + GLA_CHUNK, :] = _rms(o, gain) * (go * _sigmoid(go))
        kt = (kc * jnp.exp(b_last - bc)).astype(BF16)
        upd = lax.dot_general(vc, kt, (((0,), (0,)), ((), ())), preferred_element_type=F32)
        st_ref[...] = st * jnp.exp(b_last) + upd


def _gla(p, conv_w, w2, gb, ng, tile):
    s = p.shape[0]
    t8 = tile // 8

    def prev(col):
        return lambda h, i: (jnp.maximum(i * t8 - 1, 0), col(h))

    qcol = lambda h: C_GQ // GLA_DK + h
    kcol = lambda h: C_GK // GLA_DK + h
    vcol = lambda h: C_GV // GLA_DV + h
    return pl.pallas_call(
        _gla_kernel,
        grid=(GLA_HEADS, s // tile),
        in_specs=[
            pl.BlockSpec((tile, GLA_DK), lambda h, i: (i, qcol(h))),
            pl.BlockSpec((8, GLA_DK), prev(qcol)),
            pl.BlockSpec((tile, GLA_DK), lambda h, i: (i, kcol(h))),
            pl.BlockSpec((8, GLA_DK), prev(kcol)),
            pl.BlockSpec((tile, GLA_DV), lambda h, i: (i, vcol(h))),
            pl.BlockSpec((8, GLA_DV), prev(vcol)),
            pl.BlockSpec((tile, LANES), lambda h, i: (i, C_MISC // LANES)),
            pl.BlockSpec((tile, GLA_DV), lambda h, i: (i, C_GO // GLA_DV + h)),
            pl.BlockSpec((CONV_WIDTH, GLA_DK), lambda h, i: (0, qcol(h))),
            pl.BlockSpec((CONV_WIDTH, GLA_DK), lambda h, i: (0, kcol(h))),
            pl.BlockSpec((CONV_WIDTH, GLA_DV), lambda h, i: (0, vcol(h))),
            pl.BlockSpec((GLA_GATE_RANK, GLA_DK), lambda h, i: (0, h)),
            pl.BlockSpec((1, GLA_DK), lambda h, i: (0, h)),
            pl.BlockSpec((1, GLA_DV), lambda h, i: (0, 0)),
        ],
        out_specs=pl.BlockSpec((tile, GLA_DV), lambda h, i: (i, h)),
        out_shape=jax.ShapeDtypeStruct((s, GLA_VW), F32),
        scratch_shapes=[pltpu.VMEM((GLA_DV, GLA_DK), F32),
                        pltpu.VMEM((tile + GLA_SUB, GLA_DK), F32),
                        pltpu.VMEM((tile + GLA_SUB, GLA_DK), F32),
                        pltpu.VMEM((tile + GLA_SUB, GLA_DV), F32)],
        compiler_params=_params(("arbitrary", "arbitrary")),
        name="gla",
    )(p, p, p, p, p, p, p, p, conv_w, conv_w, conv_w, w2, gb, ng)


def _nsa_prep_kernel(q_ref, ks_ref, vs_ref, kw_ref, vw_ref, qg_ref, ksg_ref, kwg_ref,
                     qn_ref, ksa_ref, vsa_ref, kwn_ref, vwa_ref):
    t_rows = q_ref.shape[0]
    dh = NSA_HEAD_DIM
    scale = dh ** -0.5
    ones = jnp.ones((t_rows, dh), BF16)
    for h in range(NSA_HEADS):
        sl = slice(h * dh, (h + 1) * dh)
        qn_ref[:, sl] = (_rms(q_ref[:, sl], qg_ref[...]) * scale).astype(BF16)
    pos = pl.program_id(0) * t_rows + lax.broadcasted_iota(jnp.int32, (t_rows, LANES), 0)
    lane = lax.broadcasted_iota(jnp.int32, (t_rows, LANES), 1)
    onehot = jnp.where(lane == ((pos // SEL_BLOCK) & (LANES - 1)), 1.0, 0.0).astype(BF16)
    for h in range(NSA_KV_HEADS):
        sl = slice(h * dh, (h + 1) * dh)
        ksa_ref[:, 2 * h * dh:(2 * h + 1) * dh] = _rms(ks_ref[:, sl], ksg_ref[...]).astype(BF16)
        ksa_ref[:, (2 * h + 1) * dh:(2 * h + 2) * dh] = onehot
        kwn_ref[:, sl] = _rms(kw_ref[:, sl], kwg_ref[...]).astype(BF16)
        vsa_ref[:, 2 * h * dh:(2 * h + 1) * dh] = vs_ref[:, sl].astype(BF16)
        vsa_ref[:, (2 * h + 1) * dh:(2 * h + 2) * dh] = ones
        vwa_ref[:, 2 * h * dh:(2 * h + 1) * dh] = vw_ref[:, sl].astype(BF16)
        vwa_ref[:, (2 * h + 1) * dh:(2 * h + 2) * dh] = ones


def _nsa_prep(p, qg, ksg, kwg, tile):
    s = p.shape[0]
    kv = NSA_KVW
    col = lambda c, w: (lambda i: (i, c // w))
    row = lambda w: pl.BlockSpec((tile, w), lambda i: (i, 0))
    gain = pl.BlockSpec((1, NSA_HEAD_DIM), lambda i: (0, 0))
    return pl.pallas_call(
        _nsa_prep_kernel,
        grid=(s // tile,),
        in_specs=[pl.BlockSpec((tile, NSA_QW), col(C_NQ, NSA_QW)),
                  pl.BlockSpec((tile, kv), col(C_KS, kv)),
                  pl.BlockSpec((tile, kv), col(C_VS, kv)),
                  pl.BlockSpec((tile, kv), col(C_KW, kv)),
                  pl.BlockSpec((tile, kv), col(C_VW, kv)),
                  gain, gain, gain],
        out_specs=[row(NSA_QW), row(2 * kv), row(2 * kv), row(kv), row(2 * kv)],
        out_shape=[jax.ShapeDtypeStruct((s, NSA_QW), BF16),
                   jax.ShapeDtypeStruct((s, 2 * kv), BF16),
                   jax.ShapeDtypeStruct((s, 2 * kv), BF16),
                   jax.ShapeDtypeStruct((s, kv), BF16),
                   jax.ShapeDtypeStruct((s, 2 * kv), BF16)],
        compiler_params=_params(("arbitrary",)),
        name="nsa_prep",
    )(p, p, p, p, p, qg, ksg, kwg)


def _compress_kernel(u_ref, w1_ref, pos_ref, w1f_ref, w2_ref, g_ref, o_ref):
    n = o_ref.shape[1]
    half = CMP_BLOCK // CMP_STRIDE
    assert half == 2
    acc = jnp.zeros((n, 2 * CMP_HIDDEN), F32)
    for l in range(CMP_STRIDE):
        x = u_ref[pl.ds(l, n, stride=CMP_STRIDE), :].astype(BF16)
        acc = acc + jnp.dot(x, w1_ref[0, l], preferred_element_type=F32)
    posb = jnp.dot(jnp.broadcast_to(pos_ref[0], (8, pos_ref.shape[2])), w1f_ref[0],
                   preferred_element_type=F32, precision=lax.Precision.HIGHEST)[0:1]
    hid = acc[:, :CMP_HIDDEN] + pltpu.roll(acc[:, CMP_HIDDEN:], n - 1, 0) + posb
    hid = hid * _sigmoid(hid)
    out = jnp.dot(hid.astype(BF16), w2_ref[0], preferred_element_type=F32)
    is_k = pl.program_id(0) < NSA_KV_HEADS
    out = jnp.where(is_k, _rms(out, g_ref[...]), out)
    row = lax.broadcasted_iota(jnp.int32, out.shape, 0)
    o_ref[0] = jnp.where(row < n - 1, out, 0.0).astype(BF16)


def _compress(p, w1s, pos, w1f, w2, g):
    s = p.shape[0]
    n = s // CMP_STRIDE
    dh = NSA_HEAD_DIM
    return pl.pallas_call(
        _compress_kernel,
        grid=(2 * NSA_KV_HEADS,),
        in_specs=[pl.BlockSpec((s, dh), lambda j: (0, C_KC // dh + j)),
                  pl.BlockSpec((1, CMP_STRIDE, dh, 2 * CMP_HIDDEN), lambda j: (j // 2, 0, 0, 0)),
                  pl.BlockSpec((1, 1, CMP_BLOCK * dh), lambda j: (j // 2, 0, 0)),
                  pl.BlockSpec((1, CMP_BLOCK * dh, CMP_HIDDEN), lambda j: (j // 2, 0, 0)),
                  pl.BlockSpec((1, CMP_HIDDEN, dh), lambda j: (j // 2, 0, 0)),
                  pl.BlockSpec((1, dh), lambda j: (0, 0))],
        out_specs=pl.BlockSpec((1, n, dh), lambda j: (j, 0, 0)),
        out_shape=jax.ShapeDtypeStruct((2 * NSA_KV_HEADS, n, dh), BF16),
        compiler_params=_params(("arbitrary",)),
        name="compress",
    )(p, w1s, pos, w1f, w2, g)


def _unstack_heads(o, tq):
    return jnp.concatenate([o[g * tq:(g + 1) * tq] for g in range(NSA_GROUP)], axis=1)


def _branch_gate(misc, branch, tq):
    sg = _sigmoid(misc)
    base = GATE_LANE + branch * NSA_GROUP
    return jnp.concatenate([sg[:, base + g:base + g + 1] for g in range(NSA_GROUP)], axis=0)


def _cmp_attn_kernel(q_ref, kc_ref, vc_ref, ovl_ref, misc_ref, o_ref, bias_ref, *, n_sel):
    tq = q_ref.shape[0]
    dh = NSA_HEAD_DIM
    nc = kc_ref.shape[1]
    nbp = ovl_ref.shape[0]
    i = pl.program_id(1)
    s0 = i * tq
    col_step = max(LANES, nc // 4)
    n_var = nc // col_step
    tiles_per_var = (nc * CMP_STRIDE // tq) // n_var
    assert tiles_per_var * tq == col_step * CMP_STRIDE

    def variant(ncv, nbv):
        q = q_ref[...]
        row = lax.broadcasted_iota(jnp.int32, (tq, ncv), 0)
        col = lax.broadcasted_iota(jnp.int32, (tq, ncv), 1)
        maskb = jnp.where(col * CMP_STRIDE + (CMP_BLOCK - 1) <= s0 + row, 0.0, NEG)
        has_valid = (s0 + lax.broadcasted_iota(jnp.int32, (tq, 1), 0)) >= CMP_BLOCK - 1
        sg = _sigmoid(misc_ref[...])
        kc = kc_ref[0, 0:ncv, :]
        vc = vc_ref[0, 0:ncv, :]
        ps = None
        for g in range(NSA_GROUP):
            sm = _nt_dot(q[:, g * dh:(g + 1) * dh], kc) + maskb
            e = jnp.exp(sm - jnp.max(sm, axis=-1, keepdims=True))
            inv = jnp.where(has_valid, 1.0 / jnp.sum(e, axis=-1, keepdims=True), 0.0)
            pc = e * inv
            o = jnp.dot(pc.astype(BF16), vc, preferred_element_type=F32)
            o_ref[:, g * dh:(g + 1) * dh] = o * sg[:, GATE_LANE + g:GATE_LANE + g + 1]
            ps = pc if ps is None else ps + pc

        ovl = ovl_ref[0:nbv, 0:ncv]
        hi = ps.astype(BF16)
        r1 = ps - hi.astype(F32)
        mid = r1.astype(BF16)
        lo = (r1 - mid.astype(F32)).astype(BF16)
        imp = _nt_dot(ovl, hi) + _nt_dot(ovl, mid) + _nt_dot(ovl, lo)

        jb = lax.broadcasted_iota(jnp.int32, (nbv, tq), 0)
        jt = (s0 + lax.broadcasted_iota(jnp.int32, (nbv, tq), 1)) // SEL_BLOCK
        forced = (jb == 0) | (jb == jt) | (jb == jt - 1)
        future = jb > jt
        cur = jnp.where(future, -jnp.inf, jnp.where(forced, jnp.inf, imp))
        for _ in range(n_sel):
            mx = jnp.max(cur, axis=0, keepdims=True)
            idx = jnp.min(jnp.where(cur == mx, jb, nbv), axis=0, keepdims=True)
            cur = jnp.where(jb == idx, -jnp.inf, cur)
        bias_t = jnp.where((cur == -jnp.inf) & jnp.logical_not(future), 0.0, NEG)
        bias_ref[:, 0:nbv] = bias_t.T.astype(BF16)
        if nbv < nbp:
            bias_ref[:, nbv:nbp] = jnp.full((tq, nbp - nbv), NEG, BF16)

    for v in range(n_var):
        ncv = col_step * (v + 1)
        nbv = min(nbp, -(-(ncv // (SEL_BLOCK // CMP_STRIDE)) // LANES) * LANES)
        pl.when(i // tiles_per_var == v)(functools.partial(variant, ncv, nbv))


def _cmp_attn(qn, cmp, ovl_t, p, tq, n_sel):
    s = qn.shape[0]
    nc = cmp.shape[1]
    nbp = ovl_t.shape[0]
    gw = NSA_GROUP * NSA_HEAD_DIM
    return pl.pallas_call(
        functools.partial(_cmp_attn_kernel, n_sel=n_sel),
        grid=(NSA_KV_HEADS, s // tq),
        in_specs=[pl.BlockSpec((tq, gw), lambda h, i: (i, h)),
                  pl.BlockSpec((1, nc, NSA_HEAD_DIM), lambda h, i: (h, 0, 0)),
                  pl.BlockSpec((1, nc, NSA_HEAD_DIM), lambda h, i: (NSA_KV_HEADS + h, 0, 0)),
                  pl.BlockSpec((nbp, nc), lambda h, i: (0, 0)),
                  pl.BlockSpec((tq, LANES), lambda h, i: (i, C_MISC // LANES + h))],
        out_specs=[pl.BlockSpec((tq, gw), lambda h, i: (i, h)),
                   pl.BlockSpec((tq, nbp), lambda h, i: (i, h))],
        out_shape=[jax.ShapeDtypeStruct((s, NSA_QW), F32),
                   jax.ShapeDtypeStruct((s, NSA_KV_HEADS * nbp), BF16)],
        compiler_params=_params(("arbitrary", "arbitrary")),
        name="cmp_attn",
    )(qn, cmp, cmp, ovl_t, p)


def _win_attn_kernel(q_ref, k_ref, v_ref, misc_ref, prev_ref, o_ref, *, span):
    tq = q_ref.shape[0]
    dh = NSA_HEAD_DIM
    s0 = pl.program_id(1) * tq
    start = pl.multiple_of(jnp.maximum(s0 - WINDOW, 0), tq)
    q = q_ref[...]
    k = k_ref[pl.ds(start, span), :]
    v = v_ref[pl.ds(start, span), :]
    row = lax.broadcasted_iota(jnp.int32, (tq, span), 0)
    col = lax.broadcasted_iota(jnp.int32, (tq, span), 1)
    dist = (s0 - start) + row - col
    maskb = jnp.where((dist >= 0) & (dist < WINDOW), 0.0, NEG)
    sg = _sigmoid(misc_ref[...])
    base = GATE_LANE + 2 * NSA_GROUP
    for g in range(NSA_GROUP):
        sm = _nt_dot(q[:, g * dh:(g + 1) * dh], k) + maskb
        e = jnp.exp(sm - jnp.max(sm, axis=-1, keepdims=True))
        acc = jnp.dot(e.astype(BF16), v, preferred_element_type=F32)
        o = acc[:, :dh] * (sg[:, base + g:base + g + 1] / acc[:, dh:])
        o_ref[:, g * dh:(g + 1) * dh] = prev_ref[:, g * dh:(g + 1) * dh] + o


def _win_attn(qn, kwn, vwa, p, prev, tq):
    s = qn.shape[0]
    gw = NSA_GROUP * NSA_HEAD_DIM
    span = min(s, WINDOW + tq)
    return pl.pallas_call(
        functools.partial(_win_attn_kernel, span=span),
        grid=(NSA_KV_HEADS, s // tq),
        in_specs=[pl.BlockSpec((tq, gw), lambda h, i: (i, h)),
                  pl.BlockSpec((s, NSA_HEAD_DIM), lambda h, i: (0, h)),
                  pl.BlockSpec((s, 2 * NSA_HEAD_DIM), lambda h, i: (0, h)),
                  pl.BlockSpec((tq, LANES), lambda h, i: (i, C_MISC // LANES + h)),
                  pl.BlockSpec((tq, gw), lambda h, i: (i, h))],
        out_specs=pl.BlockSpec((tq, gw), lambda h, i: (i, h)),
        out_shape=jax.ShapeDtypeStruct((s, NSA_QW), F32),
        compiler_params=_params(("arbitrary", "arbitrary")),
        name="win_attn",
    )(qn, kwn, vwa, p, prev)


def _sel_attn_kernel(q_ref, bias_ref, ka_ref, va_ref, misc_ref, prev_ref, o_ref,
                     lhs_ref, m_ref, acc_ref, sa_ref, sb_ref, *, tk):
    tq = q_ref.shape[0]
    dh = NSA_HEAD_DIM
    n_super = lhs_ref.shape[0]
    tiles_per_super = LANES * SEL_BLOCK // tk
    s0 = pl.program_id(1) * tq
    for sup in range(n_super):
        for g in range(NSA_GROUP):
            lhs_ref[sup, g * tq:(g + 1) * tq, 0:dh] = q_ref[:, g * dh:(g + 1) * dh]
            lhs_ref[sup, g * tq:(g + 1) * tq, dh:2 * dh] = bias_ref[:, sup * LANES:(sup + 1) * LANES]
    m_ref[...] = jnp.full_like(m_ref, -jnp.inf)
    acc_ref[...] = jnp.zeros_like(acc_ref)

    def scores(kt, sc_ref):
        k0 = pl.multiple_of(kt * tk, tk)
        k = ka_ref[pl.ds(k0, tk), :]
        sup = kt // tiles_per_super
        for g in range(NSA_GROUP):
            rows = slice(g * tq, (g + 1) * tq)
            sc_ref[rows, :] = _nt_dot(lhs_ref[sup, rows, :], k)

    def accumulate(kt, sc_ref, diagonal):
        k0 = pl.multiple_of(kt * tk, tk)
        v = va_ref[pl.ds(k0, tk), :]
        for g in range(NSA_GROUP):
            rows = slice(g * tq, (g + 1) * tq)
            sc = sc_ref[rows, :]
            if diagonal:
                row = lax.broadcasted_iota(jnp.int32, sc.shape, 0)
                col = lax.broadcasted_iota(jnp.int32, sc.shape, 1)
                sc = jnp.where(k0 + col <= s0 + row, sc, NEG)
            m_old = m_ref[rows, :]
            m_new = jnp.maximum(m_old, jnp.max(sc, axis=-1, keepdims=True))
            alpha = jnp.exp(m_old - m_new)
            pr = jnp.exp(sc - jnp.tile(m_new, (1, tk // LANES)))
            acc_ref[rows, :] = (jnp.tile(alpha, (1, 2)) * acc_ref[rows, :]
                                + jnp.dot(pr.astype(BF16), v, preferred_element_type=F32))
            m_ref[rows, :] = m_new

    n_full = s0 // tk

    def body(j, carry):
        scores(2 * j + 1, sb_ref)
        accumulate(2 * j, sa_ref, False)
        scores(2 * j + 2, sa_ref)
        accumulate(2 * j + 1, sb_ref, False)
        return carry

    scores(0, sa_ref)
    lax.fori_loop(0, n_full // 2, body, 0)

    @pl.when(n_full % 2 == 0)
    def _():
        accumulate(n_full, sa_ref, True)

    @pl.when(n_full % 2 == 1)
    def _():
        scores(n_full, sb_ref)
        accumulate(n_full - 1, sa_ref, False)
        accumulate(n_full, sb_ref, True)

    acc = acc_ref[...]
    o = acc[:, :dh] * (_branch_gate(misc_ref[...], 1, tq) / acc[:, dh:])
    o_ref[...] = prev_ref[...] + _unstack_heads(o, tq)


def _sel_attn(qn, bias, ksa, vsa, p, prev, tq, tk):
    s = qn.shape[0]
    dh = NSA_HEAD_DIM
    gw = NSA_GROUP * dh
    nbp = bias.shape[1] // NSA_KV_HEADS
    n_super = nbp // LANES
    assert tk % tq == 0 and (LANES * SEL_BLOCK) % tk == 0 and s % tk == 0
    resident = lambda: pl.BlockSpec((s, 2 * dh), lambda h, i: (0, h), pipeline_mode=pl.Buffered(1))
    return pl.pallas_call(
        functools.partial(_sel_attn_kernel, tk=tk),
        grid=(NSA_KV_HEADS, s // tq),
        in_specs=[pl.BlockSpec((tq, gw), lambda h, i: (i, h)),
                  pl.BlockSpec((tq, nbp), lambda h, i: (i, h)),
                  resident(), resident(),
                  pl.BlockSpec((tq, LANES), lambda h, i: (i, C_MISC // LANES + h)),
                  pl.BlockSpec((tq, gw), lambda h, i: (i, h))],
        out_specs=pl.BlockSpec((tq, gw), lambda h, i: (i, h)),
        out_shape=jax.ShapeDtypeStruct((s, NSA_QW), F32),
        scratch_shapes=[pltpu.VMEM((n_super, NSA_GROUP * tq, 2 * dh), BF16),
                        pltpu.VMEM((NSA_GROUP * tq, LANES), F32),
                        pltpu.VMEM((NSA_GROUP * tq, 2 * dh), F32),
                        pltpu.VMEM((NSA_GROUP * tq, tk), F32),
                        pltpu.VMEM((NSA_GROUP * tq, tk), F32)],
        compiler_params=_params(("arbitrary", "arbitrary")),
        name="sel_attn",
    )(qn, bias, ksa, vsa, p, prev)


def _out_proj_kernel(x_ref, a_ref, b_ref, wa_ref, wb_ref, o_ref):
    o_ref[...] = (x_ref[...]
                  + jnp.dot(a_ref[...].astype(BF16), wa_ref[...], preferred_element_type=F32)
                  + jnp.dot(b_ref[...].astype(BF16), wb_ref[...], preferred_element_type=F32))


def _out_proj(x, a, b, w, tm):
    s, d = x.shape
    ka, kb = a.shape[1], b.shape[1]
    assert ka == kb
    return pl.pallas_call(
        _out_proj_kernel,
        grid=(s // tm,),
        in_specs=[pl.BlockSpec((tm, d), lambda i: (i, 0)),
                  pl.BlockSpec((tm, ka), lambda i: (i, 0)),
                  pl.BlockSpec((tm, kb), lambda i: (i, 0)),
                  pl.BlockSpec((ka, d), lambda i: (0, 0)),
                  pl.BlockSpec((kb, d), lambda i: (1, 0))],
        out_specs=pl.BlockSpec((tm, d), lambda i: (i, 0)),
        out_shape=jax.ShapeDtypeStruct((s, d), F32),
        compiler_params=_params(("arbitrary",)),
        name="out_proj",
    )(x, a, b, w, w)


def _ffn_kernel(h_ref, g_ref, wg_ref, wu_ref, wd_ref, o_ref, hn_ref):
    @pl.when(pl.program_id(1) == 0)
    def _():
        h = h_ref[...]
        hn_ref[...] = _rms(h, g_ref[...]).astype(BF16)
        o_ref[...] = h

    hn = hn_ref[...]
    a = jnp.dot(hn, wg_ref[...], preferred_element_type=F32)
    u = jnp.dot(hn, wu_ref[...], preferred_element_type=F32)
    z = (a * _sigmoid(a) * u).astype(BF16)
    o_ref[...] += jnp.dot(z, wd_ref[...], preferred_element_type=F32)


def _ffn(h, g, wg, wu, wd, tm, tf):
    s, d = h.shape
    f = wg.shape[1]
    return pl.pallas_call(
        _ffn_kernel,
        grid=(s // tm, f // tf),
        in_specs=[pl.BlockSpec((tm, d), lambda i, j: (i, 0)),
                  pl.BlockSpec((1, d), lambda i, j: (0, 0)),
                  pl.BlockSpec((d, tf), lambda i, j: (0, j)),
                  pl.BlockSpec((d, tf), lambda i, j: (0, j)),
                  pl.BlockSpec((tf, d), lambda i, j: (j, 0))],
        out_specs=pl.BlockSpec((tm, d), lambda i, j: (i, 0)),
        out_shape=jax.ShapeDtypeStruct((s, d), F32),
        scratch_shapes=[pltpu.VMEM((tm, d), BF16)],
        compiler_params=_params(("arbitrary", "arbitrary")),
        name="ffn",
    )(h, g, wg, wu, wd)


def _regroup_w_in(w_in):
    o_lr = GLA_QKV
    o_go = o_lr + GLA_GATE_RANK
    o_ng = o_go + GLA_VW + NSA_QW + 6 * NSA_KVW
    d = w_in.shape[0]
    lr = w_in[:, o_lr:o_go]
    ng = w_in[:, o_ng:o_ng + 3 * NSA_HEADS].reshape(d, NSA_KV_HEADS, NSA_GROUP, 3)
    ng = ng.transpose(0, 1, 3, 2).reshape(d, NSA_KV_HEADS, 3 * NSA_GROUP)
    pad = lambda n: jnp.zeros((d, n), w_in.dtype)
    misc0 = jnp.concatenate([lr, ng[:, 0], pad(LANES - GLA_GATE_RANK - 3 * NSA_GROUP)], axis=1)
    misc1 = jnp.concatenate([pad(GLA_GATE_RANK), ng[:, 1], pad(LANES - GLA_GATE_RANK - 3 * NSA_GROUP)], axis=1)
    return jnp.concatenate([w_in[:, :o_lr], w_in[:, o_go:o_ng], misc0, misc1], axis=1).astype(BF16)


def _overlap_t(s, nbp):
    n = s // CMP_STRIDE
    c0 = np.arange(n)[None, :] * CMP_STRIDE
    s0 = np.arange(nbp)[:, None] * SEL_BLOCK
    ov = np.clip(np.minimum(c0 + CMP_BLOCK, s0 + SEL_BLOCK) - np.maximum(c0, s0), 0, None) / CMP_STRIDE
    ov[:, n - 1] = 0.0
    ov[s // SEL_BLOCK:] = 0.0
    return jnp.asarray(ov, BF16)


def _layer(x, attn_norm_g, w_in, gla_conv_w, gla_gate_w2, gla_gate_b, gla_norm_g,
           nsa_q_norm_g, nsa_kc_norm_g, nsa_ks_norm_g, nsa_kw_norm_g,
           cmp_k_pos, cmp_k_w1, cmp_k_w2, cmp_v_pos, cmp_v_w1, cmp_v_w2,
           w_out, ffn_norm_g, w_gate, w_up, w_down):
    s = x.shape[0]
    dh = NSA_HEAD_DIM
    row = lambda v: v.reshape(1, -1)
    big = s >= 4096
    tm = 1024 if big else 256

    p = _norm_matmul(x, row(attn_norm_g), _regroup_w_in(w_in), 512 if big else 256,
                     PROJ_PAD // 2 if big else LANES)

    gla_out = _gla(p, gla_conv_w, gla_gate_w2, row(gla_gate_b), row(gla_norm_g), 512)

    qn, ksa, vsa, kwn, vwa = _nsa_prep(p, row(nsa_q_norm_g), row(nsa_ks_norm_g), row(nsa_kw_norm_g), 512)

    w1 = jnp.stack([cmp_k_w1, cmp_v_w1])
    w1s = (w1.reshape(2, 2, CMP_STRIDE, dh, CMP_HIDDEN).transpose(0, 2, 3, 1, 4)
           .reshape(2, CMP_STRIDE, dh, 2 * CMP_HIDDEN).astype(BF16))
    pos = jnp.stack([cmp_k_pos, cmp_v_pos]).reshape(2, 1, CMP_BLOCK * dh)
    w2 = jnp.stack([cmp_k_w2, cmp_v_w2]).astype(BF16)
    cmp = _compress(p, w1s, pos, w1, w2, row(nsa_kc_norm_g))

    nb = s // SEL_BLOCK
    nbp = -(-nb // LANES) * LANES
    o_cmp, bias = _cmp_attn(qn, cmp, _overlap_t(s, nbp), p, 128, min(SEL_TOPK, nb))
    o_cw = _win_attn(qn, kwn, vwa, p, o_cmp, 256)
    nsa_out = _sel_attn(qn, bias, ksa, vsa, p, o_cw, 256, 1024 if big else 512)

    h = _out_proj(x, gla_out, nsa_out, w_out.astype(BF16), 512 if big else 256)
    return _ffn(h, row(ffn_norm_g), w_gate.astype(BF16), w_up.astype(BF16), w_down.astype(BF16),
                512 if big else 256, 512)


def kernel(x, attn_norm_g, w_in, gla_conv_w, gla_gate_w2, gla_gate_b, gla_norm_g, nsa_q_norm_g, nsa_kc_norm_g, nsa_ks_norm_g, nsa_kw_norm_g, cmp_k_pos, cmp_k_w1, cmp_k_w2, cmp_v_pos, cmp_v_w1, cmp_v_w2, w_out, ffn_norm_g, w_gate, w_up, w_down):
    assert x.shape[0] == 1 and attn_norm_g.shape[0] == 1
    y = _layer(x[0], attn_norm_g[0], w_in[0], gla_conv_w[0], gla_gate_w2[0], gla_gate_b[0],
               gla_norm_g[0], nsa_q_norm_g[0], nsa_kc_norm_g[0], nsa_ks_norm_g[0], nsa_kw_norm_g[0],
               cmp_k_pos[0], cmp_k_w1[0], cmp_k_w2[0], cmp_v_pos[0], cmp_v_w1[0], cmp_v_w2[0],
               w_out[0], ffn_norm_g[0], w_gate[0], w_up[0], w_down[0])
    return y[None]
```

```python
import functools

import jax
import jax.numpy as jnp
import numpy as np
from jax import lax
from jax.experimental import pallas as pl
from jax.experimental.pallas import tpu as pltpu

D_MODEL = 2048
GLA_HEADS = 4
GLA_DK = 128
GLA_DV = 256
GLA_GATE_RANK = 16
GLA_GATE_TAU = 16.0
GLA_CHUNK = 64
GLA_SUB = 16
GLA_SAFE_LOG_DECAY = 60.0
CONV_WIDTH = 4

NSA_HEADS = 8
NSA_KV_HEADS = 2
NSA_GROUP = 4
NSA_HEAD_DIM = 128
CMP_BLOCK = 32
CMP_STRIDE = 16
CMP_HIDDEN = 128
SEL_BLOCK = 64
SEL_TOPK = 16
WINDOW = 512
D_FF = 5632
EPS = 1e-6
NEG = -1e30

GLA_QK = GLA_HEADS * GLA_DK
GLA_VW = GLA_HEADS * GLA_DV
GLA_QKV = 2 * GLA_QK + GLA_VW
NSA_QW = NSA_HEADS * NSA_HEAD_DIM
NSA_KVW = NSA_KV_HEADS * NSA_HEAD_DIM

LANES = 128
VMEM_LIMIT = 56 * 1024 * 1024

C_GQ = 0
C_GK = GLA_QK
C_GV = 2 * GLA_QK
C_GO = GLA_QKV
C_NQ = C_GO + GLA_VW
C_KC = C_NQ + NSA_QW
C_VC = C_KC + NSA_KVW
C_KS = C_VC + NSA_KVW
C_VS = C_KS + NSA_KVW
C_KW = C_VS + NSA_KVW
C_VW = C_KW + NSA_KVW
C_MISC = C_VW + NSA_KVW
PROJ_PAD = C_MISC + 2 * LANES
GATE_LANE = GLA_GATE_RANK

F32 = jnp.float32
BF16 = jnp.bfloat16


def _params(sem):
    return pltpu.CompilerParams(dimension_semantics=sem, vmem_limit_bytes=VMEM_LIMIT)


def _nt_dot(a, b):
    return lax.dot_general(a, b, (((1,), (1,)), ((), ())), preferred_element_type=F32)


def _sigmoid(x):
    return 1.0 / (1.0 + jnp.exp(-x))


def _rms(x, g):
    return x * lax.rsqrt(jnp.mean(x * x, axis=-1, keepdims=True) + EPS) * g


def _norm_matmul_kernel(x_ref, g_ref, w_ref, o_ref, xn_ref):
    @pl.when(pl.program_id(1) == 0)
    def _():
        xn_ref[...] = _rms(x_ref[...], g_ref[...]).astype(BF16)

    o_ref[...] = jnp.dot(xn_ref[...], w_ref[...], preferred_element_type=F32)


def _norm_matmul(x, g, w, tm, tn):
    s, d = x.shape
    n = w.shape[1]
    return pl.pallas_call(
        _norm_matmul_kernel,
        grid=(s // tm, n // tn),
        in_specs=[pl.BlockSpec((tm, d), lambda i, j: (i, 0)),
                  pl.BlockSpec((1, d), lambda i, j: (0, 0)),
                  pl.BlockSpec((d, tn), lambda i, j: (0, j))],
        out_specs=pl.BlockSpec((tm, tn), lambda i, j: (i, j)),
        out_shape=jax.ShapeDtypeStruct((s, n), F32),
        scratch_shapes=[pltpu.VMEM((tm, d), BF16)],
        compiler_params=_params(("arbitrary", "arbitrary")),
        name="in_proj",
    )(x, g, w)


def _gla_kernel(q_ref, qp_ref, k_ref, kp_ref, v_ref, vp_ref, lr_ref, go_ref,
                cwq_ref, cwk_ref, cwv_ref, w2_ref, gb_ref, ng_ref, o_ref,
                st_ref, sq_ref, sk_ref, sv_ref):
    t_rows = q_ref.shape[0]
    pad = GLA_SUB
    first = pl.program_id(1) == 0

    @pl.when(first)
    def _():
        st_ref[...] = jnp.zeros_like(st_ref)

    def conv_silu(u_ref, p_ref, w_ref, s_ref):
        w = w_ref[...]
        s_ref[0:8, :] = jnp.where(first, 0.0, p_ref[...])
        s_ref[8:16, :] = u_ref[0:8, :]
        head = u_ref[0:8, :] * w[CONV_WIDTH - 1:CONV_WIDTH]
        body = u_ref[8:, :] * w[CONV_WIDTH - 1:CONV_WIDTH]
        for d in range(1, CONV_WIDTH):
            wd = w[CONV_WIDTH - 1 - d:CONV_WIDTH - d]
            head = head + s_ref[pl.ds(8 - d, 8), :] * wd
            body = body + u_ref[pl.ds(8 - d, t_rows - 8), :] * wd
        acc = jnp.concatenate([head, body], axis=0)
        return acc * _sigmoid(acc)

    q = conv_silu(q_ref, qp_ref, cwq_ref, sq_ref) * (GLA_DK ** -0.5)
    k = conv_silu(k_ref, kp_ref, cwk_ref, sk_ref)
    v = conv_silu(v_ref, vp_ref, cwv_ref, sv_ref)

    z = jnp.dot(lr_ref[:, :GLA_GATE_RANK], w2_ref[...], preferred_element_type=F32,
                precision=lax.Precision.HIGHEST) + gb_ref[...]
    log_a = (jnp.minimum(z, 0.0) - jnp.log(1.0 + jnp.exp(-jnp.abs(z)))) * (1.0 / GLA_GATE_TAU)
    row = lax.broadcasted_iota(jnp.int32, log_a.shape, 0)
    lane = lax.broadcasted_iota(jnp.int32, log_a.shape, 1)
    rc = row & (GLA_CHUNK - 1)
    b = log_a
    d = 1
    while d < GLA_CHUNK:
        b = b + jnp.where(rc >= d, pltpu.roll(b, d, 0), 0.0)
        d *= 2

    srow = lax.broadcasted_iota(jnp.int32, (GLA_CHUNK, GLA_DK), 0)
    scol = lax.broadcasted_iota(jnp.int32, (GLA_CHUNK, GLA_DK), 1)
    zk = jnp.zeros((LANES - GLA_CHUNK, GLA_DK), BF16)
    zv = jnp.zeros((LANES - GLA_CHUNK, GLA_DV), BF16)
    za = jnp.zeros((GLA_SUB, LANES), F32)
    gain = ng_ref[...]

    def chunk_loop(intra_scores):
        for c in range(t_rows // GLA_CHUNK):
            lo = c * GLA_CHUNK
            bc = b[lo:lo + GLA_CHUNK]
            qc = q[lo:lo + GLA_CHUNK]
            kc = k[lo:lo + GLA_CHUNK]
            vc = v[lo:lo + GLA_CHUNK].astype(BF16)
            st = st_ref[...]
            b_last = bc[GLA_CHUNK - 1:GLA_CHUNK]
            qd = (qc * jnp.exp(bc)).astype(BF16)
            scores = intra_scores(lo, bc, qc, kc, qd)
            o = _nt_dot(qd, st.astype(BF16)) + jnp.dot(
                scores.astype(BF16), jnp.concatenate([vc, zv], axis=0), preferred_element_type=F32)
            go = go_ref[lo:lo + GLA_CHUNK, :]
            o_ref[lo:lo + GLA_CHUNK, :] = _rms(o, gain) * (go * _sigmoid(go))
            kt = (kc * jnp.exp(b_last - bc)).astype(BF16)
            upd = lax.dot_general(vc, kt, (((0,), (0,)), ((), ())), preferred_element_type=F32)
            st_ref[...] = st * jnp.exp(b_last) + upd

    small_decay = jnp.min(b) >= -GLA_SAFE_LOG_DECAY

    @pl.when(small_decay)
    def _():
        def intra_scores(lo, bc, qc, kc, qd):
            kd = (kc * jnp.exp(-bc)).astype(BF16)
            a = _nt_dot(qd, jnp.concatenate([kd, zk], axis=0))
            return jnp.where(scol <= srow, a, 0.0)

        chunk_loop(intra_scores)

    @pl.when(jnp.logical_not(small_decay))
    def _():
        zero_pad = jnp.zeros((pad, GLA_DK), F32)
        sq_ref[0:pad, :] = zero_pad
        sk_ref[0:pad, :] = zero_pad
        sq_ref[pad:, :] = b
        sk_ref[pad:, :] = k
        rs = row & (GLA_SUB - 1)
        dl = rc - lane
        band = jnp.where(dl == 0, jnp.sum(q * k, axis=-1, keepdims=True), 0.0)
        for delta in range(1, GLA_SUB):
            e = jnp.exp(jnp.where(rs >= delta, b - sq_ref[pl.ds(pad - delta, t_rows), :], -jnp.inf))
            sc = jnp.sum(q * sk_ref[pl.ds(pad - delta, t_rows), :] * e, axis=-1, keepdims=True)
            band = jnp.where(dl == delta, sc, band)

        def intra_scores(lo, bc, qc, kc, qd):
            blocks = [za]
            for sub in range(1, GLA_CHUNK // GLA_SUB):
                r0 = sub * GLA_SUB
                ref_b = bc[r0:r0 + 1]
                qq = (qc[r0:r0 + GLA_SUB] * jnp.exp(bc[r0:r0 + GLA_SUB] - ref_b)).astype(BF16)
                kk = (kc * jnp.exp(jnp.where(srow < r0, ref_b - bc, -jnp.inf))).astype(BF16)
                blocks.append(_nt_dot(qq, jnp.concatenate([kk, zk], axis=0)))
            return band[lo:lo + GLA_CHUNK] + jnp.concatenate(blocks, axis=0)

        chunk_loop(intra_scores)


def _gla(p, conv_w, w2, gb, ng, tile):
    s = p.shape[0]
    t8 = tile // 8

    def prev(col):
        return lambda h, i: (jnp.maximum(i * t8 - 1, 0), col(h))

    qcol = lambda h: C_GQ // GLA_DK + h
    kcol = lambda h: C_GK // GLA_DK + h
    vcol = lambda h: C_GV // GLA_DV + h
    return pl.pallas_call(
        _gla_kernel,
        grid=(GLA_HEADS, s // tile),
        in_specs=[
            pl.BlockSpec((tile, GLA_DK), lambda h, i: (i, qcol(h))),
            pl.BlockSpec((8, GLA_DK), prev(qcol)),
            pl.BlockSpec((tile, GLA_DK), lambda h, i: (i, kcol(h))),
            pl.BlockSpec((8, GLA_DK), prev(kcol)),
            pl.BlockSpec((tile, GLA_DV), lambda h, i: (i, vcol(h))),
            pl.BlockSpec((8, GLA_DV), prev(vcol)),
            pl.BlockSpec((tile, LANES), lambda h, i: (i, C_MISC // LANES)),
            pl.BlockSpec((tile, GLA_DV), lambda h, i: (i, C_GO // GLA_DV + h)),
            pl.BlockSpec((CONV_WIDTH, GLA_DK), lambda h, i: (0, qcol(h))),
            pl.BlockSpec((CONV_WIDTH, GLA_DK), lambda h, i: (0, kcol(h))),
            pl.BlockSpec((CONV_WIDTH, GLA_DV), lambda h, i: (0, vcol(h))),
            pl.BlockSpec((GLA_GATE_RANK, GLA_DK), lambda h, i: (0, h)),
            pl.BlockSpec((1, GLA_DK), lambda h, i: (0, h)),
            pl.BlockSpec((1, GLA_DV), lambda h, i: (0, 0)),
        ],
        out_specs=pl.BlockSpec((tile, GLA_DV), lambda h, i: (i, h)),
        out_shape=jax.ShapeDtypeStruct((s, GLA_VW), F32),
        scratch_shapes=[pltpu.VMEM((GLA_DV, GLA_DK), F32),
                        pltpu.VMEM((tile + GLA_SUB, GLA_DK), F32),
                        pltpu.VMEM((tile + GLA_SUB, GLA_DK), F32),
                        pltpu.VMEM((GLA_SUB, GLA_DV), F32)],
        compiler_params=_params(("arbitrary", "arbitrary")),
        name="gla",
    )(p, p, p, p, p, p, p, p, conv_w, conv_w, conv_w, w2, gb, ng)


def _nsa_prep_kernel(q_ref, ks_ref, vs_ref, kw_ref, vw_ref, qg_ref, ksg_ref, kwg_ref,
                     qn_ref, ksa_ref, vsa_ref, kwn_ref, vwa_ref):
    t_rows = q_ref.shape[0]
    dh = NSA_HEAD_DIM
    scale = dh ** -0.5
    ones = jnp.ones((t_rows, dh), BF16)
    for h in range(NSA_HEADS):
        sl = slice(h * dh, (h + 1) * dh)
        qn_ref[:, sl] = (_rms(q_ref[:, sl], qg_ref[...]) * scale).astype(BF16)
    pos = pl.program_id(0) * t_rows + lax.broadcasted_iota(jnp.int32, (t_rows, LANES), 0)
    lane = lax.broadcasted_iota(jnp.int32, (t_rows, LANES), 1)
    onehot = jnp.where(lane == ((pos // SEL_BLOCK) & (LANES - 1)), 1.0, 0.0).astype(BF16)
    for h in range(NSA_KV_HEADS):
        sl = slice(h * dh, (h + 1) * dh)
        ksa_ref[:, 2 * h * dh:(2 * h + 1) * dh] = _rms(ks_ref[:, sl], ksg_ref[...]).astype(BF16)
        ksa_ref[:, (2 * h + 1) * dh:(2 * h + 2) * dh] = onehot
        kwn_ref[:, sl] = _rms(kw_ref[:, sl], kwg_ref[...]).astype(BF16)
        vsa_ref[:, 2 * h * dh:(2 * h + 1) * dh] = vs_ref[:, sl].astype(BF16)
        vsa_ref[:, (2 * h + 1) * dh:(2 * h + 2) * dh] = ones
        vwa_ref[:, 2 * h * dh:(2 * h + 1) * dh] = vw_ref[:, sl].astype(BF16)
        vwa_ref[:, (2 * h + 1) * dh:(2 * h + 2) * dh] = ones


def _nsa_prep(p, qg, ksg, kwg, tile):
    s = p.shape[0]
    kv = NSA_KVW
    col = lambda c, w: (lambda i: (i, c // w))
    row = lambda w: pl.BlockSpec((tile, w), lambda i: (i, 0))
    gain = pl.BlockSpec((1, NSA_HEAD_DIM), lambda i: (0, 0))
    return pl.pallas_call(
        _nsa_prep_kernel,
        grid=(s // tile,),
        in_specs=[pl.BlockSpec((tile, NSA_QW), col(C_NQ, NSA_QW)),
                  pl.BlockSpec((tile, kv), col(C_KS, kv)),
                  pl.BlockSpec((tile, kv), col(C_VS, kv)),
                  pl.BlockSpec((tile, kv), col(C_KW, kv)),
                  pl.BlockSpec((tile, kv), col(C_VW, kv)),
                  gain, gain, gain],
        out_specs=[row(NSA_QW), row(2 * kv), row(2 * kv), row(kv), row(2 * kv)],
        out_shape=[jax.ShapeDtypeStruct((s, NSA_QW), BF16),
                   jax.ShapeDtypeStruct((s, 2 * kv), BF16),
                   jax.ShapeDtypeStruct((s, 2 * kv), BF16),
                   jax.ShapeDtypeStruct((s, kv), BF16),
                   jax.ShapeDtypeStruct((s, 2 * kv), BF16)],
        compiler_params=_params(("arbitrary",)),
        name="nsa_prep",
    )(p, p, p, p, p, qg, ksg, kwg)


def _compress_kernel(u_ref, w1_ref, pos_ref, w1f_ref, w2_ref, g_ref, o_ref):
    n = o_ref.shape[1]
    half = CMP_BLOCK // CMP_STRIDE
    assert half == 2
    acc = jnp.zeros((n, 2 * CMP_HIDDEN), F32)
    for l in range(CMP_STRIDE):
        x = u_ref[pl.ds(l, n, stride=CMP_STRIDE), :].astype(BF16)
        acc = acc + jnp.dot(x, w1_ref[0, l], preferred_element_type=F32)
    posb = jnp.dot(jnp.broadcast_to(pos_ref[0], (8, pos_ref.shape[2])), w1f_ref[0],
                   preferred_element_type=F32, precision=lax.Precision.HIGHEST)[0:1]
    hid = acc[:, :CMP_HIDDEN] + pltpu.roll(acc[:, CMP_HIDDEN:], n - 1, 0) + posb
    hid = hid * _sigmoid(hid)
    out = jnp.dot(hid.astype(BF16), w2_ref[0], preferred_element_type=F32)
    is_k = pl.program_id(0) < NSA_KV_HEADS
    out = jnp.where(is_k, _rms(out, g_ref[...]), out)
    row = lax.broadcasted_iota(jnp.int32, out.shape, 0)
    o_ref[0] = jnp.where(row < n - 1, out, 0.0).astype(BF16)


def _compress(p, w1s, pos, w1f, w2, g):
    s = p.shape[0]
    n = s // CMP_STRIDE
    dh = NSA_HEAD_DIM
    return pl.pallas_call(
        _compress_kernel,
        grid=(2 * NSA_KV_HEADS,),
        in_specs=[pl.BlockSpec((s, dh), lambda j: (0, C_KC // dh + j)),
                  pl.BlockSpec((1, CMP_STRIDE, dh, 2 * CMP_HIDDEN), lambda j: (j // 2, 0, 0, 0)),
                  pl.BlockSpec((1, 1, CMP_BLOCK * dh), lambda j: (j // 2, 0, 0)),
                  pl.BlockSpec((1, CMP_BLOCK * dh, CMP_HIDDEN), lambda j: (j // 2, 0, 0)),
                  pl.BlockSpec((1, CMP_HIDDEN, dh), lambda j: (j // 2, 0, 0)),
                  pl.BlockSpec((1, dh), lambda j: (0, 0))],
        out_specs=pl.BlockSpec((1, n, dh), lambda j: (j, 0, 0)),
        out_shape=jax.ShapeDtypeStruct((2 * NSA_KV_HEADS, n, dh), BF16),
        compiler_params=_params(("arbitrary",)),
        name="compress",
    )(p, w1s, pos, w1f, w2, g)


def _unstack_heads(o, tq):
    return jnp.concatenate([o[g * tq:(g + 1) * tq] for g in range(NSA_GROUP)], axis=1)


def _branch_gate(misc, branch, tq):
    sg = _sigmoid(misc)
    base = GATE_LANE + branch * NSA_GROUP
    return jnp.concatenate([sg[:, base + g:base + g + 1] for g in range(NSA_GROUP)], axis=0)


def _cmp_attn_kernel(q_ref, kc_ref, vc_ref, ovl_ref, misc_ref, o_ref, bias_ref, *, n_sel):
    tq = q_ref.shape[0]
    dh = NSA_HEAD_DIM
    nc = kc_ref.shape[1]
    nbp = ovl_ref.shape[0]
    i = pl.program_id(1)
    s0 = i * tq
    col_step = max(LANES, nc // 4)
    n_var = nc // col_step
    tiles_per_var = (nc * CMP_STRIDE // tq) // n_var
    assert tiles_per_var * tq == col_step * CMP_STRIDE

    def variant(ncv, nbv):
        q = q_ref[...]
        row = lax.broadcasted_iota(jnp.int32, (tq, ncv), 0)
        col = lax.broadcasted_iota(jnp.int32, (tq, ncv), 1)
        maskb = jnp.where(col * CMP_STRIDE + (CMP_BLOCK - 1) <= s0 + row, 0.0, NEG)
        has_valid = (s0 + lax.broadcasted_iota(jnp.int32, (tq, 1), 0)) >= CMP_BLOCK - 1
        sg = _sigmoid(misc_ref[...])
        kc = kc_ref[0, 0:ncv, :]
        vc = vc_ref[0, 0:ncv, :]
        ps = None
        for g in range(NSA_GROUP):
            sm = _nt_dot(q[:, g * dh:(g + 1) * dh], kc) + maskb
            e = jnp.exp(sm - jnp.max(sm, axis=-1, keepdims=True))
            inv = jnp.where(has_valid, 1.0 / jnp.sum(e, axis=-1, keepdims=True), 0.0)
            pc = e * inv
            o = jnp.dot(pc.astype(BF16), vc, preferred_element_type=F32)
            o_ref[:, g * dh:(g + 1) * dh] = o * sg[:, GATE_LANE + g:GATE_LANE + g + 1]
            ps = pc if ps is None else ps + pc

        ovl = ovl_ref[0:nbv, 0:ncv]
        hi = ps.astype(BF16)
        r1 = ps - hi.astype(F32)
        mid = r1.astype(BF16)
        lo = (r1 - mid.astype(F32)).astype(BF16)
        imp = _nt_dot(ovl, hi) + _nt_dot(ovl, mid) + _nt_dot(ovl, lo)

        jb = lax.broadcasted_iota(jnp.int32, (nbv, tq), 0)
        jt = (s0 + lax.broadcasted_iota(jnp.int32, (nbv, tq), 1)) // SEL_BLOCK
        forced = (jb == 0) | (jb == jt) | (jb == jt - 1)
        future = jb > jt
        cur = jnp.where(future | forced, -jnp.inf, imp)
        for _ in range(n_sel - 3):
            mx = jnp.max(cur, axis=0, keepdims=True)
            idx = jnp.min(jnp.where(cur == mx, jb, nbv), axis=0, keepdims=True)
            cur = jnp.where(jb == idx, -jnp.inf, cur)
        bias_t = jnp.where((cur == -jnp.inf) & jnp.logical_not(future), 0.0, NEG)
        bias_ref[:, 0:nbv] = bias_t.T.astype(BF16)
        if nbv < nbp:
            bias_ref[:, nbv:nbp] = jnp.full((tq, nbp - nbv), NEG, BF16)

    for v in range(n_var):
        ncv = col_step * (v + 1)
        nbv = min(nbp, -(-(ncv // (SEL_BLOCK // CMP_STRIDE)) // LANES) * LANES)
        pl.when(i // tiles_per_var == v)(functools.partial(variant, ncv, nbv))


def _cmp_attn(qn, cmp, ovl_t, p, tq, n_sel):
    s = qn.shape[0]
    nc = cmp.shape[1]
    nbp = ovl_t.shape[0]
    gw = NSA_GROUP * NSA_HEAD_DIM
    return pl.pallas_call(
        functools.partial(_cmp_attn_kernel, n_sel=n_sel),
        grid=(NSA_KV_HEADS, s // tq),
        in_specs=[pl.BlockSpec((tq, gw), lambda h, i: (i, h)),
                  pl.BlockSpec((1, nc, NSA_HEAD_DIM), lambda h, i: (h, 0, 0)),
                  pl.BlockSpec((1, nc, NSA_HEAD_DIM), lambda h, i: (NSA_KV_HEADS + h, 0, 0)),
                  pl.BlockSpec((nbp, nc), lambda h, i: (0, 0)),
                  pl.BlockSpec((tq, LANES), lambda h, i: (i, C_MISC // LANES + h))],
        out_specs=[pl.BlockSpec((tq, gw), lambda h, i: (i, h)),
                   pl.BlockSpec((tq, nbp), lambda h, i: (i, h))],
        out_shape=[jax.ShapeDtypeStruct((s, NSA_QW), F32),
                   jax.ShapeDtypeStruct((s, NSA_KV_HEADS * nbp), BF16)],
        compiler_params=_params(("arbitrary", "arbitrary")),
        name="cmp_attn",
    )(qn, cmp, cmp, ovl_t, p)


def _win_attn_kernel(q_ref, k_ref, v_ref, misc_ref, prev_ref, o_ref, *, span):
    tq = q_ref.shape[0]
    dh = NSA_HEAD_DIM
    s0 = pl.program_id(1) * tq
    start = pl.multiple_of(jnp.maximum(s0 - WINDOW, 0), tq)
    q = q_ref[...]
    k = k_ref[pl.ds(start, span), :]
    v = v_ref[pl.ds(start, span), :]
    row = lax.broadcasted_iota(jnp.int32, (tq, span), 0)
    col = lax.broadcasted_iota(jnp.int32, (tq, span), 1)
    dist = (s0 - start) + row - col
    maskb = jnp.where((dist >= 0) & (dist < WINDOW), 0.0, NEG)
    sg = _sigmoid(misc_ref[...])
    base = GATE_LANE + 2 * NSA_GROUP
    for g in range(NSA_GROUP):
        sm = _nt_dot(q[:, g * dh:(g + 1) * dh], k) + maskb
        e = jnp.exp(sm - jnp.max(sm, axis=-1, keepdims=True))
        acc = jnp.dot(e.astype(BF16), v, preferred_element_type=F32)
        o = acc[:, :dh] * (sg[:, base + g:base + g + 1] / acc[:, dh:])
        o_ref[:, g * dh:(g + 1) * dh] = prev_ref[:, g * dh:(g + 1) * dh] + o


def _win_attn(qn, kwn, vwa, p, prev, tq):
    s = qn.shape[0]
    gw = NSA_GROUP * NSA_HEAD_DIM
    span = min(s, WINDOW + tq)
    return pl.pallas_call(
        functools.partial(_win_attn_kernel, span=span),
        grid=(NSA_KV_HEADS, s // tq),
        in_specs=[pl.BlockSpec((tq, gw), lambda h, i: (i, h)),
                  pl.BlockSpec((s, NSA_HEAD_DIM), lambda h, i: (0, h)),
                  pl.BlockSpec((s, 2 * NSA_HEAD_DIM), lambda h, i: (0, h)),
                  pl.BlockSpec((tq, LANES), lambda h, i: (i, C_MISC // LANES + h)),
                  pl.BlockSpec((tq, gw), lambda h, i: (i, h))],
        out_specs=pl.BlockSpec((tq, gw), lambda h, i: (i, h)),
        out_shape=jax.ShapeDtypeStruct((s, NSA_QW), F32),
        compiler_params=_params(("arbitrary", "arbitrary")),
        name="win_attn",
    )(qn, kwn, vwa, p, prev)


def _sel_attn_kernel(q_ref, bias_ref, ka_ref, va_ref, misc_ref, prev_ref, o_ref,
                     lhs_ref, m_ref, acc_ref, sa_ref, sb_ref, *, tk):
    tq = q_ref.shape[0]
    dh = NSA_HEAD_DIM
    n_super = lhs_ref.shape[0]
    tiles_per_super = LANES * SEL_BLOCK // tk
    s0 = pl.program_id(1) * tq
    for sup in range(n_super):
        for g in range(NSA_GROUP):
            lhs_ref[sup, g * tq:(g + 1) * tq, 0:dh] = q_ref[:, g * dh:(g + 1) * dh]
            lhs_ref[sup, g * tq:(g + 1) * tq, dh:2 * dh] = bias_ref[:, sup * LANES:(sup + 1) * LANES]
    m_ref[...] = jnp.full_like(m_ref, -jnp.inf)
    acc_ref[...] = jnp.zeros_like(acc_ref)

    def scores(kt, sc_ref):
        k0 = pl.multiple_of(kt * tk, tk)
        k = ka_ref[pl.ds(k0, tk), :]
        sup = kt // tiles_per_super
        for g in range(NSA_GROUP):
            rows = slice(g * tq, (g + 1) * tq)
            sc_ref[rows, :] = _nt_dot(lhs_ref[sup, rows, :], k)

    def accumulate(kt, sc_ref, diagonal):
        k0 = pl.multiple_of(kt * tk, tk)
        v = va_ref[pl.ds(k0, tk), :]
        for g in range(NSA_GROUP):
            rows = slice(g * tq, (g + 1) * tq)
            sc = sc_ref[rows, :]
            if diagonal:
                row = lax.broadcasted_iota(jnp.int32, sc.shape, 0)
                col = lax.broadcasted_iota(jnp.int32, sc.shape, 1)
                sc = jnp.where(k0 + col <= s0 + row, sc, NEG)
            m_old = m_ref[rows, :]
            m_new = jnp.maximum(m_old, jnp.max(sc, axis=-1, keepdims=True))
            alpha = jnp.exp(m_old - m_new)
            pr = jnp.exp(sc - jnp.tile(m_new, (1, tk // LANES)))
            acc_ref[rows, :] = (jnp.tile(alpha, (1, 2)) * acc_ref[rows, :]
                                + jnp.dot(pr.astype(BF16), v, preferred_element_type=F32))
            m_ref[rows, :] = m_new

    n_full = s0 // tk

    def body(j, carry):
        scores(2 * j + 1, sb_ref)
        accumulate(2 * j, sa_ref, False)
        scores(2 * j + 2, sa_ref)
        accumulate(2 * j + 1, sb_ref, False)
        return carry

    scores(0, sa_ref)
    lax.fori_loop(0, n_full // 2, body, 0)

    @pl.when(n_full % 2 == 0)
    def _():
        accumulate(n_full, sa_ref, True)

    @pl.when(n_full % 2 == 1)
    def _():
        scores(n_full, sb_ref)
        accumulate(n_full - 1, sa_ref, False)
        accumulate(n_full, sb_ref, True)

    acc = acc_ref[...]
    o = acc[:, :dh] * (_branch_gate(misc_ref[...], 1, tq) / acc[:, dh:])
    o_ref[...] = prev_ref[...] + _unstack_heads(o, tq)


def _sel_attn(qn, bias, ksa, vsa, p, prev, tq, tk):
    s = qn.shape[0]
    dh = NSA_HEAD_DIM
    gw = NSA_GROUP * dh
    nbp = bias.shape[1] // NSA_KV_HEADS
    n_super = nbp // LANES
    assert tk % tq == 0 and (LANES * SEL_BLOCK) % tk == 0 and s % tk == 0
    resident = lambda: pl.BlockSpec((s, 2 * dh), lambda h, i: (0, h), pipeline_mode=pl.Buffered(1))
    return pl.pallas_call(
        functools.partial(_sel_attn_kernel, tk=tk),
        grid=(NSA_KV_HEADS, s // tq),
        in_specs=[pl.BlockSpec((tq, gw), lambda h, i: (i, h)),
                  pl.BlockSpec((tq, nbp), lambda h, i: (i, h)),
                  resident(), resident(),
                  pl.BlockSpec((tq, LANES), lambda h, i: (i, C_MISC // LANES + h)),
                  pl.BlockSpec((tq, gw), lambda h, i: (i, h))],
        out_specs=pl.BlockSpec((tq, gw), lambda h, i: (i, h)),
        out_shape=jax.ShapeDtypeStruct((s, NSA_QW), F32),
        scratch_shapes=[pltpu.VMEM((n_super, NSA_GROUP * tq, 2 * dh), BF16),
                        pltpu.VMEM((NSA_GROUP * tq, LANES), F32),
                        pltpu.VMEM((NSA_GROUP * tq, 2 * dh), F32),
                        pltpu.VMEM((NSA_GROUP * tq, tk), F32),
                        pltpu.VMEM((NSA_GROUP * tq, tk), F32)],
        compiler_params=_params(("arbitrary", "arbitrary")),
        name="sel_attn",
    )(qn, bias, ksa, vsa, p, prev)


def _out_proj_kernel(x_ref, a_ref, b_ref, wa_ref, wb_ref, o_ref):
    o_ref[...] = (x_ref[...]
                  + jnp.dot(a_ref[...].astype(BF16), wa_ref[...], preferred_element_type=F32)
                  + jnp.dot(b_ref[...].astype(BF16), wb_ref[...], preferred_element_type=F32))


def _out_proj(x, a, b, w, tm):
    s, d = x.shape
    ka, kb = a.shape[1], b.shape[1]
    assert ka == kb
    return pl.pallas_call(
        _out_proj_kernel,
        grid=(s // tm,),
        in_specs=[pl.BlockSpec((tm, d), lambda i: (i, 0)),
                  pl.BlockSpec((tm, ka), lambda i: (i, 0)),
                  pl.BlockSpec((tm, kb), lambda i: (i, 0)),
                  pl.BlockSpec((ka, d), lambda i: (0, 0)),
                  pl.BlockSpec((kb, d), lambda i: (1, 0))],
        out_specs=pl.BlockSpec((tm, d), lambda i: (i, 0)),
        out_shape=jax.ShapeDtypeStruct((s, d), F32),
        compiler_params=_params(("arbitrary",)),
        name="out_proj",
    )(x, a, b, w, w)


def _ffn_kernel(h_ref, g_ref, wg_ref, wu_ref, wd_ref, o_ref, hn_ref):
    @pl.when(pl.program_id(1) == 0)
    def _():
        h = h_ref[...]
        hn_ref[...] = _rms(h, g_ref[...]).astype(BF16)
        o_ref[...] = h

    hn = hn_ref[...]
    a = jnp.dot(hn, wg_ref[...], preferred_element_type=F32)
    u = jnp.dot(hn, wu_ref[...], preferred_element_type=F32)
    z = (a * _sigmoid(a) * u).astype(BF16)
    o_ref[...] += jnp.dot(z, wd_ref[...], preferred_element_type=F32)


def _ffn(h, g, wg, wu, wd, tm, tf):
    s, d = h.shape
    f = wg.shape[1]
    return pl.pallas_call(
        _ffn_kernel,
        grid=(s // tm, f // tf),
        in_specs=[pl.BlockSpec((tm, d), lambda i, j: (i, 0)),
                  pl.BlockSpec((1, d), lambda i, j: (0, 0)),
                  pl.BlockSpec((d, tf), lambda i, j: (0, j)),
                  pl.BlockSpec((d, tf), lambda i, j: (0, j)),
                  pl.BlockSpec((tf, d), lambda i, j: (j, 0))],
        out_specs=pl.BlockSpec((tm, d), lambda i, j: (i, 0)),
        out_shape=jax.ShapeDtypeStruct((s, d), F32),
        scratch_shapes=[pltpu.VMEM((tm, d), BF16)],
        compiler_params=_params(("arbitrary", "arbitrary")),
        name="ffn",
    )(h, g, wg, wu, wd)


def _regroup_w_in(w_in):
    o_lr = GLA_QKV
    o_go = o_lr + GLA_GATE_RANK
    o_ng = o_go + GLA_VW + NSA_QW + 6 * NSA_KVW
    d = w_in.shape[0]
    w_in = w_in.astype(BF16)
    lr = w_in[:, o_lr:o_go]
    ng = w_in[:, o_ng:o_ng + 3 * NSA_HEADS].reshape(d, NSA_KV_HEADS, NSA_GROUP, 3)
    ng = ng.transpose(0, 1, 3, 2).reshape(d, NSA_KV_HEADS, 3 * NSA_GROUP)
    pad = lambda n: jnp.zeros((d, n), w_in.dtype)
    misc0 = jnp.concatenate([lr, ng[:, 0], pad(LANES - GLA_GATE_RANK - 3 * NSA_GROUP)], axis=1)
    misc1 = jnp.concatenate([pad(GLA_GATE_RANK), ng[:, 1], pad(LANES - GLA_GATE_RANK - 3 * NSA_GROUP)], axis=1)
    return jnp.concatenate([w_in[:, :o_lr], w_in[:, o_go:o_ng], misc0, misc1], axis=1)


def _overlap_t(s, nbp):
    n = s // CMP_STRIDE
    c0 = np.arange(n)[None, :] * CMP_STRIDE
    s0 = np.arange(nbp)[:, None] * SEL_BLOCK
    ov = np.clip(np.minimum(c0 + CMP_BLOCK, s0 + SEL_BLOCK) - np.maximum(c0, s0), 0, None) / CMP_STRIDE
    ov[:, n - 1] = 0.0
    ov[s // SEL_BLOCK:] = 0.0
    return jnp.asarray(ov, BF16)


def _layer(x, attn_norm_g, w_in, gla_conv_w, gla_gate_w2, gla_gate_b, gla_norm_g,
           nsa_q_norm_g, nsa_kc_norm_g, nsa_ks_norm_g, nsa_kw_norm_g,
           cmp_k_pos, cmp_k_w1, cmp_k_w2, cmp_v_pos, cmp_v_w1, cmp_v_w2,
           w_out, ffn_norm_g, w_gate, w_up, w_down):
    s = x.shape[0]
    dh = NSA_HEAD_DIM
    row = lambda v: v.reshape(1, -1)
    big = s >= 4096
    tm = 1024 if big else 256

    p = _norm_matmul(x, row(attn_norm_g), _regroup_w_in(w_in), 512 if big else 256,
                     PROJ_PAD // 2 if big else LANES)

    gla_out = _gla(p, gla_conv_w, gla_gate_w2, row(gla_gate_b), row(gla_norm_g), 512)

    qn, ksa, vsa, kwn, vwa = _nsa_prep(p, row(nsa_q_norm_g), row(nsa_ks_norm_g), row(nsa_kw_norm_g), 512)

    w1 = jnp.stack([cmp_k_w1, cmp_v_w1])
    w1s = (w1.reshape(2, 2, CMP_STRIDE, dh, CMP_HIDDEN).transpose(0, 2, 3, 1, 4)
           .reshape(2, CMP_STRIDE, dh, 2 * CMP_HIDDEN).astype(BF16))
    pos = jnp.stack([cmp_k_pos, cmp_v_pos]).reshape(2, 1, CMP_BLOCK * dh)
    w2 = jnp.stack([cmp_k_w2, cmp_v_w2]).astype(BF16)
    cmp = _compress(p, w1s, pos, w1, w2, row(nsa_kc_norm_g))

    nb = s // SEL_BLOCK
    nbp = -(-nb // LANES) * LANES
    o_cmp, bias = _cmp_attn(qn, cmp, _overlap_t(s, nbp), p, 128, min(SEL_TOPK, nb))
    o_cw = _win_attn(qn, kwn, vwa, p, o_cmp, 256)
    nsa_out = _sel_attn(qn, bias, ksa, vsa, p, o_cw, 512 if big else 256, 1024 if big else 512)

    h = _out_proj(x, gla_out, nsa_out, w_out.astype(BF16), 512 if big else 256)
    return _ffn(h, row(ffn_norm_g), w_gate.astype(BF16), w_up.astype(BF16), w_down.astype(BF16),
                512 if big else 256, 512)


def kernel(x, attn_norm_g, w_in, gla_conv_w, gla_gate_w2, gla_gate_b, gla_norm_g, nsa_q_norm_g, nsa_kc_norm_g, nsa_ks_norm_g, nsa_kw_norm_g, cmp_k_pos, cmp_k_w1, cmp_k_w2, cmp_v_pos, cmp_v_w1, cmp_v_w2, w_out, ffn_norm_g, w_gate, w_up, w_down):
    assert x.shape[0] == 1 and attn_norm_g.shape[0] == 1
    y = _layer(x[0], attn_norm_g[0], w_in[0], gla_conv_w[0], gla_gate_w2[0], gla_gate_b[0],
               gla_norm_g[0], nsa_q_norm_g[0], nsa_kc_norm_g[0], nsa_ks_norm_g[0], nsa_kw_norm_g[0],
               cmp_k_pos[0], cmp_k_w1[0], cmp_k_w2[0], cmp_v_pos[0], cmp_v_w1[0], cmp_v_w2[0],
               w_out[0], ffn_norm_g[0], w_gate[0], w_up[0], w_down[0])
    return y[None]
```

```python
import functools

import jax
import jax.numpy as jnp
import numpy as np
from jax import lax
from jax.experimental import pallas as pl
from jax.experimental.pallas import tpu as pltpu

D_MODEL = 2048
GLA_HEADS = 4
GLA_DK = 128
GLA_DV = 256
GLA_GATE_RANK = 16
GLA_GATE_TAU = 16.0
GLA_CHUNK = 64
GLA_SUB = 16
GLA_SAFE_LOG_DECAY = 60.0
SOFTMAX_SHIFT_LIMIT = 40.0
CONV_WIDTH = 4

NSA_HEADS = 8
NSA_KV_HEADS = 2
NSA_GROUP = 4
NSA_HEAD_DIM = 128
CMP_BLOCK = 32
CMP_STRIDE = 16
CMP_HIDDEN = 128
SEL_BLOCK = 64
SEL_TOPK = 16
WINDOW = 512
D_FF = 5632
EPS = 1e-6
NEG = -1e30

GLA_QK = GLA_HEADS * GLA_DK
GLA_VW = GLA_HEADS * GLA_DV
GLA_QKV = 2 * GLA_QK + GLA_VW
NSA_QW = NSA_HEADS * NSA_HEAD_DIM
NSA_KVW = NSA_KV_HEADS * NSA_HEAD_DIM

LANES = 128
VMEM_LIMIT = 56 * 1024 * 1024

C_GQ = 0
C_GK = GLA_QK
C_GV = 2 * GLA_QK
C_GO = GLA_QKV
C_NQ = C_GO + GLA_VW
C_KC = C_NQ + NSA_QW
C_VC = C_KC + NSA_KVW
C_KS = C_VC + NSA_KVW
C_VS = C_KS + NSA_KVW
C_KW = C_VS + NSA_KVW
C_VW = C_KW + NSA_KVW
C_MISC = C_VW + NSA_KVW
PROJ_PAD = C_MISC + 2 * LANES
GATE_LANE = GLA_GATE_RANK

F32 = jnp.float32
BF16 = jnp.bfloat16


def _params(sem):
    return pltpu.CompilerParams(dimension_semantics=sem, vmem_limit_bytes=VMEM_LIMIT)


def _nt_dot(a, b):
    return lax.dot_general(a, b, (((1,), (1,)), ((), ())), preferred_element_type=F32)


def _sigmoid(x):
    return 1.0 / (1.0 + jnp.exp(-x))


def _rms(x, g):
    return x * lax.rsqrt(jnp.mean(x * x, axis=-1, keepdims=True) + EPS) * g


def _norm_matmul_kernel(x_ref, g_ref, w_ref, o_ref, xn_ref):
    @pl.when(pl.program_id(1) == 0)
    def _():
        xn_ref[...] = _rms(x_ref[...], g_ref[...]).astype(BF16)

    o_ref[...] = jnp.dot(xn_ref[...], w_ref[...], preferred_element_type=F32)


def _norm_matmul(x, g, w, tm, tn):
    s, d = x.shape
    n = w.shape[1]
    return pl.pallas_call(
        _norm_matmul_kernel,
        grid=(s // tm, n // tn),
        in_specs=[pl.BlockSpec((tm, d), lambda i, j: (i, 0)),
                  pl.BlockSpec((1, d), lambda i, j: (0, 0)),
                  pl.BlockSpec((d, tn), lambda i, j: (0, j))],
        out_specs=pl.BlockSpec((tm, tn), lambda i, j: (i, j)),
        out_shape=jax.ShapeDtypeStruct((s, n), F32),
        scratch_shapes=[pltpu.VMEM((tm, d), BF16)],
        compiler_params=_params(("arbitrary", "arbitrary")),
        name="in_proj",
    )(x, g, w)


def _gla_kernel(q_ref, qp_ref, k_ref, kp_ref, v_ref, vp_ref, lr_ref, go_ref,
                cwq_ref, cwk_ref, cwv_ref, w2_ref, gb_ref, ng_ref, o_ref,
                st_ref, sq_ref, sk_ref, sv_ref):
    t_rows = q_ref.shape[0]
    pad = GLA_SUB
    first = pl.program_id(1) == 0

    @pl.when(first)
    def _():
        st_ref[...] = jnp.zeros_like(st_ref)

    def conv_silu(u_ref, p_ref, w_ref, s_ref):
        w = w_ref[...]
        s_ref[0:8, :] = jnp.where(first, 0.0, p_ref[...])
        s_ref[8:16, :] = u_ref[0:8, :]
        head = u_ref[0:8, :] * w[CONV_WIDTH - 1:CONV_WIDTH]
        body = u_ref[8:, :] * w[CONV_WIDTH - 1:CONV_WIDTH]
        for d in range(1, CONV_WIDTH):
            wd = w[CONV_WIDTH - 1 - d:CONV_WIDTH - d]
            head = head + s_ref[pl.ds(8 - d, 8), :] * wd
            body = body + u_ref[pl.ds(8 - d, t_rows - 8), :] * wd
        acc = jnp.concatenate([head, body], axis=0)
        return acc * _sigmoid(acc)

    q = conv_silu(q_ref, qp_ref, cwq_ref, sq_ref) * (GLA_DK ** -0.5)
    k = conv_silu(k_ref, kp_ref, cwk_ref, sk_ref)
    v = conv_silu(v_ref, vp_ref, cwv_ref, sv_ref)

    z = jnp.dot(lr_ref[:, :GLA_GATE_RANK], w2_ref[...], preferred_element_type=F32,
                precision=lax.Precision.HIGHEST) + gb_ref[...]
    log_a = (jnp.minimum(z, 0.0) - jnp.log(1.0 + jnp.exp(-jnp.abs(z)))) * (1.0 / GLA_GATE_TAU)
    row = lax.broadcasted_iota(jnp.int32, log_a.shape, 0)
    lane = lax.broadcasted_iota(jnp.int32, log_a.shape, 1)
    rc = row & (GLA_CHUNK - 1)
    b = log_a
    d = 1
    while d < GLA_CHUNK:
        b = b + jnp.where(rc >= d, pltpu.roll(b, d, 0), 0.0)
        d *= 2

    srow = lax.broadcasted_iota(jnp.int32, (GLA_CHUNK, GLA_DK), 0)
    scol = lax.broadcasted_iota(jnp.int32, (GLA_CHUNK, GLA_DK), 1)
    zk = jnp.zeros((LANES - GLA_CHUNK, GLA_DK), BF16)
    zv = jnp.zeros((LANES - GLA_CHUNK, GLA_DV), BF16)
    za = jnp.zeros((GLA_SUB, LANES), F32)
    gain = ng_ref[...]

    def chunk_loop(intra_scores):
        for c in range(t_rows // GLA_CHUNK):
            lo = c * GLA_CHUNK
            bc = b[lo:lo + GLA_CHUNK]
            qc = q[lo:lo + GLA_CHUNK]
            kc = k[lo:lo + GLA_CHUNK]
            vc = v[lo:lo + GLA_CHUNK].astype(BF16)
            st = st_ref[...]
            b_last = bc[GLA_CHUNK - 1:GLA_CHUNK]
            qd = (qc * jnp.exp(bc)).astype(BF16)
            scores = intra_scores(lo, bc, qc, kc, qd)
            o = _nt_dot(qd, st.astype(BF16)) + jnp.dot(
                scores.astype(BF16), jnp.concatenate([vc, zv], axis=0), preferred_element_type=F32)
            go = go_ref[lo:lo + GLA_CHUNK, :]
            o_ref[lo:lo + GLA_CHUNK, :] = _rms(o, gain) * (go * _sigmoid(go))
            kt = (kc * jnp.exp(b_last - bc)).astype(BF16)
            upd = lax.dot_general(vc, kt, (((0,), (0,)), ((), ())), preferred_element_type=F32)
            st_ref[...] = st * jnp.exp(b_last) + upd

    small_decay = jnp.min(b) >= -GLA_SAFE_LOG_DECAY

    @pl.when(small_decay)
    def _():
        def intra_scores(lo, bc, qc, kc, qd):
            kd = (kc * jnp.exp(-bc)).astype(BF16)
            a = _nt_dot(qd, jnp.concatenate([kd, zk], axis=0))
            return jnp.where(scol <= srow, a, 0.0)

        chunk_loop(intra_scores)

    @pl.when(jnp.logical_not(small_decay))
    def _():
        zero_pad = jnp.zeros((pad, GLA_DK), F32)
        sq_ref[0:pad, :] = zero_pad
        sk_ref[0:pad, :] = zero_pad
        sq_ref[pad:, :] = b
        sk_ref[pad:, :] = k
        rs = row & (GLA_SUB - 1)
        dl = rc - lane
        band = jnp.where(dl == 0, jnp.sum(q * k, axis=-1, keepdims=True), 0.0)
        for delta in range(1, GLA_SUB):
            e = jnp.exp(jnp.where(rs >= delta, b - sq_ref[pl.ds(pad - delta, t_rows), :], -jnp.inf))
            sc = jnp.sum(q * sk_ref[pl.ds(pad - delta, t_rows), :] * e, axis=-1, keepdims=True)
            band = jnp.where(dl == delta, sc, band)

        def intra_scores(lo, bc, qc, kc, qd):
            blocks = [za]
            for sub in range(1, GLA_CHUNK // GLA_SUB):
                r0 = sub * GLA_SUB
                ref_b = bc[r0:r0 + 1]
                qq = (qc[r0:r0 + GLA_SUB] * jnp.exp(bc[r0:r0 + GLA_SUB] - ref_b)).astype(BF16)
                kk = (kc * jnp.exp(jnp.where(srow < r0, ref_b - bc, -jnp.inf))).astype(BF16)
                blocks.append(_nt_dot(qq, jnp.concatenate([kk, zk], axis=0)))
            return band[lo:lo + GLA_CHUNK] + jnp.concatenate(blocks, axis=0)

        chunk_loop(intra_scores)


def _gla(p, conv_w, w2, gb, ng, tile):
    s = p.shape[0]
    t8 = tile // 8

    def prev(col):
        return lambda h, i: (jnp.maximum(i * t8 - 1, 0), col(h))

    qcol = lambda h: C_GQ // GLA_DK + h
    kcol = lambda h: C_GK // GLA_DK + h
    vcol = lambda h: C_GV // GLA_DV + h
    return pl.pallas_call(
        _gla_kernel,
        grid=(GLA_HEADS, s // tile),
        in_specs=[
            pl.BlockSpec((tile, GLA_DK), lambda h, i: (i, qcol(h))),
            pl.BlockSpec((8, GLA_DK), prev(qcol)),
            pl.BlockSpec((tile, GLA_DK), lambda h, i: (i, kcol(h))),
            pl.BlockSpec((8, GLA_DK), prev(kcol)),
            pl.BlockSpec((tile, GLA_DV), lambda h, i: (i, vcol(h))),
            pl.BlockSpec((8, GLA_DV), prev(vcol)),
            pl.BlockSpec((tile, LANES), lambda h, i: (i, C_MISC // LANES)),
            pl.BlockSpec((tile, GLA_DV), lambda h, i: (i, C_GO // GLA_DV + h)),
            pl.BlockSpec((CONV_WIDTH, GLA_DK), lambda h, i: (0, qcol(h))),
            pl.BlockSpec((CONV_WIDTH, GLA_DK), lambda h, i: (0, kcol(h))),
            pl.BlockSpec((CONV_WIDTH, GLA_DV), lambda h, i: (0, vcol(h))),
            pl.BlockSpec((GLA_GATE_RANK, GLA_DK), lambda h, i: (0, h)),
            pl.BlockSpec((1, GLA_DK), lambda h, i: (0, h)),
            pl.BlockSpec((1, GLA_DV), lambda h, i: (0, 0)),
        ],
        out_specs=pl.BlockSpec((tile, GLA_DV), lambda h, i: (i, h)),
        out_shape=jax.ShapeDtypeStruct((s, GLA_VW), F32),
        scratch_shapes=[pltpu.VMEM((GLA_DV, GLA_DK), F32),
                        pltpu.VMEM((tile + GLA_SUB, GLA_DK), F32),
                        pltpu.VMEM((tile + GLA_SUB, GLA_DK), F32),
                        pltpu.VMEM((GLA_SUB, GLA_DV), F32)],
        compiler_params=_params(("arbitrary", "arbitrary")),
        name="gla",
    )(p, p, p, p, p, p, p, p, conv_w, conv_w, conv_w, w2, gb, ng)


def _nsa_prep_kernel(q_ref, ks_ref, vs_ref, kw_ref, vw_ref, qg_ref, ksg_ref, kwg_ref,
                     qn_ref, ksa_ref, vsa_ref, kwn_ref, vwa_ref):
    t_rows = q_ref.shape[0]
    dh = NSA_HEAD_DIM
    scale = dh ** -0.5
    ones = jnp.ones((t_rows, dh), BF16)
    for h in range(NSA_HEADS):
        sl = slice(h * dh, (h + 1) * dh)
        qn_ref[:, sl] = (_rms(q_ref[:, sl], qg_ref[...]) * scale).astype(BF16)
    pos = pl.program_id(0) * t_rows + lax.broadcasted_iota(jnp.int32, (t_rows, LANES), 0)
    lane = lax.broadcasted_iota(jnp.int32, (t_rows, LANES), 1)
    onehot = jnp.where(lane == ((pos // SEL_BLOCK) & (LANES - 1)), 1.0, 0.0).astype(BF16)
    for h in range(NSA_KV_HEADS):
        sl = slice(h * dh, (h + 1) * dh)
        ksa_ref[:, 2 * h * dh:(2 * h + 1) * dh] = _rms(ks_ref[:, sl], ksg_ref[...]).astype(BF16)
        ksa_ref[:, (2 * h + 1) * dh:(2 * h + 2) * dh] = onehot
        kwn_ref[:, sl] = _rms(kw_ref[:, sl], kwg_ref[...]).astype(BF16)
        vsa_ref[:, 2 * h * dh:(2 * h + 1) * dh] = vs_ref[:, sl].astype(BF16)
        vsa_ref[:, (2 * h + 1) * dh:(2 * h + 2) * dh] = ones
        vwa_ref[:, 2 * h * dh:(2 * h + 1) * dh] = vw_ref[:, sl].astype(BF16)
        vwa_ref[:, (2 * h + 1) * dh:(2 * h + 2) * dh] = ones


def _nsa_prep(p, qg, ksg, kwg, tile):
    s = p.shape[0]
    kv = NSA_KVW
    col = lambda c, w: (lambda i: (i, c // w))
    row = lambda w: pl.BlockSpec((tile, w), lambda i: (i, 0))
    gain = pl.BlockSpec((1, NSA_HEAD_DIM), lambda i: (0, 0))
    return pl.pallas_call(
        _nsa_prep_kernel,
        grid=(s // tile,),
        in_specs=[pl.BlockSpec((tile, NSA_QW), col(C_NQ, NSA_QW)),
                  pl.BlockSpec((tile, kv), col(C_KS, kv)),
                  pl.BlockSpec((tile, kv), col(C_VS, kv)),
                  pl.BlockSpec((tile, kv), col(C_KW, kv)),
                  pl.BlockSpec((tile, kv), col(C_VW, kv)),
                  gain, gain, gain],
        out_specs=[row(NSA_QW), row(2 * kv), row(2 * kv), row(kv), row(2 * kv)],
        out_shape=[jax.ShapeDtypeStruct((s, NSA_QW), BF16),
                   jax.ShapeDtypeStruct((s, 2 * kv), BF16),
                   jax.ShapeDtypeStruct((s, 2 * kv), BF16),
                   jax.ShapeDtypeStruct((s, kv), BF16),
                   jax.ShapeDtypeStruct((s, 2 * kv), BF16)],
        compiler_params=_params(("arbitrary",)),
        name="nsa_prep",
    )(p, p, p, p, p, qg, ksg, kwg)


def _compress_kernel(u_ref, w1_ref, pos_ref, w1f_ref, w2_ref, g_ref, o_ref):
    n = o_ref.shape[1]
    half = CMP_BLOCK // CMP_STRIDE
    assert half == 2
    acc = jnp.zeros((n, 2 * CMP_HIDDEN), F32)
    for l in range(CMP_STRIDE):
        x = u_ref[pl.ds(l, n, stride=CMP_STRIDE), :].astype(BF16)
        acc = acc + jnp.dot(x, w1_ref[0, l], preferred_element_type=F32)
    posb = jnp.dot(jnp.broadcast_to(pos_ref[0], (8, pos_ref.shape[2])), w1f_ref[0],
                   preferred_element_type=F32, precision=lax.Precision.HIGHEST)[0:1]
    hid = acc[:, :CMP_HIDDEN] + pltpu.roll(acc[:, CMP_HIDDEN:], n - 1, 0) + posb
    hid = hid * _sigmoid(hid)
    out = jnp.dot(hid.astype(BF16), w2_ref[0], preferred_element_type=F32)
    is_k = pl.program_id(0) < NSA_KV_HEADS
    out = jnp.where(is_k, _rms(out, g_ref[...]), out)
    row = lax.broadcasted_iota(jnp.int32, out.shape, 0)
    o_ref[0] = jnp.where(row < n - 1, out, 0.0).astype(BF16)


def _compress(p, w1s, pos, w1f, w2, g):
    s = p.shape[0]
    n = s // CMP_STRIDE
    dh = NSA_HEAD_DIM
    return pl.pallas_call(
        _compress_kernel,
        grid=(2 * NSA_KV_HEADS,),
        in_specs=[pl.BlockSpec((s, dh), lambda j: (0, C_KC // dh + j)),
                  pl.BlockSpec((1, CMP_STRIDE, dh, 2 * CMP_HIDDEN), lambda j: (j // 2, 0, 0, 0)),
                  pl.BlockSpec((1, 1, CMP_BLOCK * dh), lambda j: (j // 2, 0, 0)),
                  pl.BlockSpec((1, CMP_BLOCK * dh, CMP_HIDDEN), lambda j: (j // 2, 0, 0)),
                  pl.BlockSpec((1, CMP_HIDDEN, dh), lambda j: (j // 2, 0, 0)),
                  pl.BlockSpec((1, dh), lambda j: (0, 0))],
        out_specs=pl.BlockSpec((1, n, dh), lambda j: (j, 0, 0)),
        out_shape=jax.ShapeDtypeStruct((2 * NSA_KV_HEADS, n, dh), BF16),
        compiler_params=_params(("arbitrary",)),
        name="compress",
    )(p, w1s, pos, w1f, w2, g)


def _softmax_shift(gq_ref, gk_ref):
    bound = (NSA_HEAD_DIM ** 0.5) * jnp.max(jnp.abs(gq_ref[...])) * jnp.max(jnp.abs(gk_ref[...]))
    usable = bound <= SOFTMAX_SHIFT_LIMIT
    return usable, jnp.where(usable, bound, 0.0)


def _unstack_heads(o, tq):
    return jnp.concatenate([o[g * tq:(g + 1) * tq] for g in range(NSA_GROUP)], axis=1)


def _branch_gate(misc, branch, tq):
    sg = _sigmoid(misc)
    base = GATE_LANE + branch * NSA_GROUP
    return jnp.concatenate([sg[:, base + g:base + g + 1] for g in range(NSA_GROUP)], axis=0)


def _cmp_attn_kernel(q_ref, kc_ref, ovl_ref, misc_ref, gq_ref, gk_ref, o_ref, bias_ref, *, n_sel):
    tq = q_ref.shape[0]
    dh = NSA_HEAD_DIM
    nc = kc_ref.shape[1]
    nbp = ovl_ref.shape[0]
    i = pl.program_id(0)
    s0 = i * tq
    col_step = max(LANES, nc // 4)
    n_var = nc // col_step
    tiles_per_var = (nc * CMP_STRIDE // tq) // n_var
    assert tiles_per_var * tq == col_step * CMP_STRIDE

    fixed, shift = _softmax_shift(gq_ref, gk_ref)

    def variant(ncv, nbv, online):
        q = q_ref[...]
        row = lax.broadcasted_iota(jnp.int32, (tq, ncv), 0)
        col = lax.broadcasted_iota(jnp.int32, (tq, ncv), 1)
        maskb = jnp.where(col * CMP_STRIDE + (CMP_BLOCK - 1) <= s0 + row, -shift, NEG)
        has_valid = (s0 + lax.broadcasted_iota(jnp.int32, (tq, 1), 0)) >= CMP_BLOCK - 1
        sg = _sigmoid(misc_ref[...])
        ovl = ovl_ref[0:nbv, 0:ncv]
        imps = []
        for h in range(NSA_KV_HEADS):
            kc = kc_ref[h, 0:ncv, :]
            vc = kc_ref[NSA_KV_HEADS + h, 0:ncv, :]
            ps = None
            for g in range(NSA_GROUP):
                hd = h * NSA_GROUP + g
                sm = _nt_dot(q[:, hd * dh:(hd + 1) * dh], kc) + maskb
                if online:
                    sm = sm - jnp.max(sm, axis=-1, keepdims=True)
                e = jnp.exp(sm)
                inv = jnp.where(has_valid, 1.0 / jnp.sum(e, axis=-1, keepdims=True), 0.0)
                pc = e * inv
                o = jnp.dot(pc.astype(BF16), vc, preferred_element_type=F32)
                gl = h * LANES + GATE_LANE + g
                o_ref[:, hd * dh:(hd + 1) * dh] = o * sg[:, gl:gl + 1]
                ps = pc if ps is None else ps + pc
            hi = ps.astype(BF16)
            r1 = ps - hi.astype(F32)
            mid = r1.astype(BF16)
            lo = (r1 - mid.astype(F32)).astype(BF16)
            imps.append(_nt_dot(ovl, hi) + _nt_dot(ovl, mid) + _nt_dot(ovl, lo))

        jb = lax.broadcasted_iota(jnp.int32, (nbv, tq), 0)
        jt = (s0 + lax.broadcasted_iota(jnp.int32, (nbv, tq), 1)) // SEL_BLOCK
        forced = (jb == 0) | (jb == jt) | (jb == jt - 1)
        future = jb > jt
        curs = [jnp.where(future | forced, -jnp.inf, imp) for imp in imps]
        for _ in range(n_sel - 3):
            for h in range(NSA_KV_HEADS):
                mx = jnp.max(curs[h], axis=0, keepdims=True)
                idx = jnp.min(jnp.where(curs[h] == mx, jb, nbv), axis=0, keepdims=True)
                curs[h] = jnp.where(jb == idx, -jnp.inf, curs[h])
        for h in range(NSA_KV_HEADS):
            bias_t = jnp.where((curs[h] == -jnp.inf) & jnp.logical_not(future), 0.0, NEG)
            bias_ref[:, h * nbp:h * nbp + nbv] = bias_t.T.astype(BF16)
            if nbv < nbp:
                bias_ref[:, h * nbp + nbv:(h + 1) * nbp] = jnp.full((tq, nbp - nbv), NEG, BF16)

    for v in range(n_var):
        ncv = col_step * (v + 1)
        nbv = min(nbp, -(-(ncv // (SEL_BLOCK // CMP_STRIDE)) // LANES) * LANES)
        pl.when(fixed & (i // tiles_per_var == v))(functools.partial(variant, ncv, nbv, False))
    pl.when(jnp.logical_not(fixed))(functools.partial(variant, nc, nbp, True))


def _cmp_attn(qn, cmp, ovl_t, p, gq, gk, tq, n_sel):
    s = qn.shape[0]
    nc = cmp.shape[1]
    nbp = ovl_t.shape[0]
    assert n_sel >= 3
    misc_w = NSA_KV_HEADS * LANES
    gain = pl.BlockSpec((1, NSA_HEAD_DIM), lambda i: (0, 0))
    return pl.pallas_call(
        functools.partial(_cmp_attn_kernel, n_sel=n_sel),
        grid=(s // tq,),
        in_specs=[pl.BlockSpec((tq, NSA_QW), lambda i: (i, 0)),
                  pl.BlockSpec((2 * NSA_KV_HEADS, nc, NSA_HEAD_DIM), lambda i: (0, 0, 0)),
                  pl.BlockSpec((nbp, nc), lambda i: (0, 0)),
                  pl.BlockSpec((tq, misc_w), lambda i: (i, C_MISC // misc_w)),
                  gain, gain],
        out_specs=[pl.BlockSpec((tq, NSA_QW), lambda i: (i, 0)),
                   pl.BlockSpec((tq, NSA_KV_HEADS * nbp), lambda i: (i, 0))],
        out_shape=[jax.ShapeDtypeStruct((s, NSA_QW), F32),
                   jax.ShapeDtypeStruct((s, NSA_KV_HEADS * nbp), BF16)],
        compiler_params=_params(("arbitrary",)),
        name="cmp_attn",
    )(qn, cmp, ovl_t, p, gq, gk)


def _win_attn_kernel(q_ref, k_ref, v_ref, misc_ref, prev_ref, gq_ref, gk_ref, o_ref, *, span):
    tq = q_ref.shape[0]
    dh = NSA_HEAD_DIM
    s0 = pl.program_id(1) * tq
    start = pl.multiple_of(jnp.maximum(s0 - WINDOW, 0), tq)
    fixed, shift = _softmax_shift(gq_ref, gk_ref)
    row = lax.broadcasted_iota(jnp.int32, (tq, span), 0)
    col = lax.broadcasted_iota(jnp.int32, (tq, span), 1)
    dist = (s0 - start) + row - col
    maskb = jnp.where((dist >= 0) & (dist < WINDOW), -shift, NEG)

    def run(online):
        q = q_ref[...]
        k = k_ref[pl.ds(start, span), :]
        v = v_ref[pl.ds(start, span), :]
        sg = _sigmoid(misc_ref[...])
        base = GATE_LANE + 2 * NSA_GROUP
        for g in range(NSA_GROUP):
            sm = _nt_dot(q[:, g * dh:(g + 1) * dh], k) + maskb
            if online:
                sm = sm - jnp.max(sm, axis=-1, keepdims=True)
            acc = jnp.dot(jnp.exp(sm).astype(BF16), v, preferred_element_type=F32)
            o = acc[:, :dh] * (sg[:, base + g:base + g + 1] / acc[:, dh:])
            o_ref[:, g * dh:(g + 1) * dh] = prev_ref[:, g * dh:(g + 1) * dh] + o

    pl.when(fixed)(functools.partial(run, False))
    pl.when(jnp.logical_not(fixed))(functools.partial(run, True))


def _win_attn(qn, kwn, vwa, p, prev, gq, gk, tq):
    s = qn.shape[0]
    gw = NSA_GROUP * NSA_HEAD_DIM
    span = min(s, WINDOW + tq)
    gain = pl.BlockSpec((1, NSA_HEAD_DIM), lambda h, i: (0, 0))
    return pl.pallas_call(
        functools.partial(_win_attn_kernel, span=span),
        grid=(NSA_KV_HEADS, s // tq),
        in_specs=[pl.BlockSpec((tq, gw), lambda h, i: (i, h)),
                  pl.BlockSpec((s, NSA_HEAD_DIM), lambda h, i: (0, h)),
                  pl.BlockSpec((s, 2 * NSA_HEAD_DIM), lambda h, i: (0, h)),
                  pl.BlockSpec((tq, LANES), lambda h, i: (i, C_MISC // LANES + h)),
                  pl.BlockSpec((tq, gw), lambda h, i: (i, h)),
                  gain, gain],
        out_specs=pl.BlockSpec((tq, gw), lambda h, i: (i, h)),
        out_shape=jax.ShapeDtypeStruct((s, NSA_QW), F32),
        compiler_params=_params(("arbitrary", "arbitrary")),
        name="win_attn",
    )(qn, kwn, vwa, p, prev, gq, gk)


def _sel_attn_kernel(q_ref, bias_ref, ka_ref, va_ref, misc_ref, prev_ref, gq_ref, gk_ref, o_ref,
                     lhs_ref, m_ref, acc_ref, sa_ref, sb_ref, *, tk):
    tq = q_ref.shape[0]
    dh = NSA_HEAD_DIM
    n_super = lhs_ref.shape[0]
    tiles_per_super = LANES * SEL_BLOCK // tk
    s0 = pl.program_id(1) * tq
    fixed, shift = _softmax_shift(gq_ref, gk_ref)
    for sup in range(n_super):
        for g in range(NSA_GROUP):
            lhs_ref[sup, g * tq:(g + 1) * tq, 0:dh] = q_ref[:, g * dh:(g + 1) * dh]
            lhs_ref[sup, g * tq:(g + 1) * tq, dh:2 * dh] = (
                bias_ref[:, sup * LANES:(sup + 1) * LANES].astype(F32) - shift).astype(BF16)
    m_ref[...] = jnp.full_like(m_ref, -jnp.inf)
    acc_ref[...] = jnp.zeros_like(acc_ref)

    def scores(kt, sc_ref):
        k0 = pl.multiple_of(kt * tk, tk)
        k = ka_ref[pl.ds(k0, tk), :]
        sup = kt // tiles_per_super
        for g in range(NSA_GROUP):
            rows = slice(g * tq, (g + 1) * tq)
            sc_ref[rows, :] = _nt_dot(lhs_ref[sup, rows, :], k)

    def accumulate(online, kt, sc_ref, diagonal):
        k0 = pl.multiple_of(kt * tk, tk)
        v = va_ref[pl.ds(k0, tk), :]
        for g in range(NSA_GROUP):
            rows = slice(g * tq, (g + 1) * tq)
            sc = sc_ref[rows, :]
            if diagonal:
                row = lax.broadcasted_iota(jnp.int32, sc.shape, 0)
                col = lax.broadcasted_iota(jnp.int32, sc.shape, 1)
                sc = jnp.where(k0 + col <= s0 + row, sc, NEG)
            if online:
                m_old = m_ref[rows, :]
                m_new = jnp.maximum(m_old, jnp.max(sc, axis=-1, keepdims=True))
                alpha = jnp.exp(m_old - m_new)
                pr = jnp.exp(sc - jnp.tile(m_new, (1, tk // LANES)))
                acc_ref[rows, :] = (jnp.tile(alpha, (1, 2)) * acc_ref[rows, :]
                                    + jnp.dot(pr.astype(BF16), v, preferred_element_type=F32))
                m_ref[rows, :] = m_new
            else:
                acc_ref[rows, :] += jnp.dot(jnp.exp(sc).astype(BF16), v, preferred_element_type=F32)

    n_full = s0 // tk

    def run(online):
        acc_fn = functools.partial(accumulate, online)

        def body(j, carry):
            scores(2 * j + 1, sb_ref)
            acc_fn(2 * j, sa_ref, False)
            scores(2 * j + 2, sa_ref)
            acc_fn(2 * j + 1, sb_ref, False)
            return carry

        scores(0, sa_ref)
        lax.fori_loop(0, n_full // 2, body, 0)

        @pl.when(n_full % 2 == 0)
        def _():
            acc_fn(n_full, sa_ref, True)

        @pl.when(n_full % 2 == 1)
        def _():
            scores(n_full, sb_ref)
            acc_fn(n_full - 1, sa_ref, False)
            acc_fn(n_full, sb_ref, True)

    pl.when(fixed)(functools.partial(run, False))
    pl.when(jnp.logical_not(fixed))(functools.partial(run, True))

    acc = acc_ref[...]
    o = acc[:, :dh] * (_branch_gate(misc_ref[...], 1, tq) / acc[:, dh:])
    o_ref[...] = prev_ref[...] + _unstack_heads(o, tq)


def _sel_attn(qn, bias, ksa, vsa, p, prev, gq, gk, tq, tk):
    s = qn.shape[0]
    dh = NSA_HEAD_DIM
    gw = NSA_GROUP * dh
    nbp = bias.shape[1] // NSA_KV_HEADS
    n_super = nbp // LANES
    assert tk % tq == 0 and (LANES * SEL_BLOCK) % tk == 0 and s % tk == 0
    resident = lambda: pl.BlockSpec((s, 2 * dh), lambda h, i: (0, h), pipeline_mode=pl.Buffered(1))
    gain = pl.BlockSpec((1, dh), lambda h, i: (0, 0))
    return pl.pallas_call(
        functools.partial(_sel_attn_kernel, tk=tk),
        grid=(NSA_KV_HEADS, s // tq),
        in_specs=[pl.BlockSpec((tq, gw), lambda h, i: (i, h)),
                  pl.BlockSpec((tq, nbp), lambda h, i: (i, h)),
                  resident(), resident(),
                  pl.BlockSpec((tq, LANES), lambda h, i: (i, C_MISC // LANES + h)),
                  pl.BlockSpec((tq, gw), lambda h, i: (i, h)),
                  gain, gain],
        out_specs=pl.BlockSpec((tq, gw), lambda h, i: (i, h)),
        out_shape=jax.ShapeDtypeStruct((s, NSA_QW), F32),
        scratch_shapes=[pltpu.VMEM((n_super, NSA_GROUP * tq, 2 * dh), BF16),
                        pltpu.VMEM((NSA_GROUP * tq, LANES), F32),
                        pltpu.VMEM((NSA_GROUP * tq, 2 * dh), F32),
                        pltpu.VMEM((NSA_GROUP * tq, tk), F32),
                        pltpu.VMEM((NSA_GROUP * tq, tk), F32)],
        compiler_params=_params(("arbitrary", "arbitrary")),
        name="sel_attn",
    )(qn, bias, ksa, vsa, p, prev, gq, gk)


def _out_proj_kernel(x_ref, a_ref, b_ref, wa_ref, wb_ref, o_ref):
    o_ref[...] = (x_ref[...]
                  + jnp.dot(a_ref[...].astype(BF16), wa_ref[...], preferred_element_type=F32)
                  + jnp.dot(b_ref[...].astype(BF16), wb_ref[...], preferred_element_type=F32))


def _out_proj(x, a, b, w, tm):
    s, d = x.shape
    ka, kb = a.shape[1], b.shape[1]
    assert ka == kb
    return pl.pallas_call(
        _out_proj_kernel,
        grid=(s // tm,),
        in_specs=[pl.BlockSpec((tm, d), lambda i: (i, 0)),
                  pl.BlockSpec((tm, ka), lambda i: (i, 0)),
                  pl.BlockSpec((tm, kb), lambda i: (i, 0)),
                  pl.BlockSpec((ka, d), lambda i: (0, 0)),
                  pl.BlockSpec((kb, d), lambda i: (1, 0))],
        out_specs=pl.BlockSpec((tm, d), lambda i: (i, 0)),
        out_shape=jax.ShapeDtypeStruct((s, d), F32),
        compiler_params=_params(("arbitrary",)),
        name="out_proj",
    )(x, a, b, w, w)


def _ffn_kernel(h_ref, g_ref, wg_ref, wu_ref, wd_ref, o_ref, hn_ref):
    @pl.when(pl.program_id(1) == 0)
    def _():
        h = h_ref[...]
        hn_ref[...] = _rms(h, g_ref[...]).astype(BF16)
        o_ref[...] = h

    hn = hn_ref[...]
    a = jnp.dot(hn, wg_ref[...], preferred_element_type=F32)
    u = jnp.dot(hn, wu_ref[...], preferred_element_type=F32)
    z = (a * _sigmoid(a) * u).astype(BF16)
    o_ref[...] += jnp.dot(z, wd_ref[...], preferred_element_type=F32)


def _ffn(h, g, wg, wu, wd, tm, tf):
    s, d = h.shape
    f = wg.shape[1]
    return pl.pallas_call(
        _ffn_kernel,
        grid=(s // tm, f // tf),
        in_specs=[pl.BlockSpec((tm, d), lambda i, j: (i, 0)),
                  pl.BlockSpec((1, d), lambda i, j: (0, 0)),
                  pl.BlockSpec((d, tf), lambda i, j: (0, j)),
                  pl.BlockSpec((d, tf), lambda i, j: (0, j)),
                  pl.BlockSpec((tf, d), lambda i, j: (j, 0))],
        out_specs=pl.BlockSpec((tm, d), lambda i, j: (i, 0)),
        out_shape=jax.ShapeDtypeStruct((s, d), F32),
        scratch_shapes=[pltpu.VMEM((tm, d), BF16)],
        compiler_params=_params(("arbitrary", "arbitrary")),
        name="ffn",
    )(h, g, wg, wu, wd)


def _regroup_w_in(w_in):
    o_lr = GLA_QKV
    o_go = o_lr + GLA_GATE_RANK
    o_ng = o_go + GLA_VW + NSA_QW + 6 * NSA_KVW
    d = w_in.shape[0]
    w_in = w_in.astype(BF16)
    lr = w_in[:, o_lr:o_go]
    ng = w_in[:, o_ng:o_ng + 3 * NSA_HEADS].reshape(d, NSA_KV_HEADS, NSA_GROUP, 3)
    ng = ng.transpose(0, 1, 3, 2).reshape(d, NSA_KV_HEADS, 3 * NSA_GROUP)
    pad = lambda n: jnp.zeros((d, n), w_in.dtype)
    misc0 = jnp.concatenate([lr, ng[:, 0], pad(LANES - GLA_GATE_RANK - 3 * NSA_GROUP)], axis=1)
    misc1 = jnp.concatenate([pad(GLA_GATE_RANK), ng[:, 1], pad(LANES - GLA_GATE_RANK - 3 * NSA_GROUP)], axis=1)
    return jnp.concatenate([w_in[:, :o_lr], w_in[:, o_go:o_ng], misc0, misc1], axis=1)


def _overlap_t(s, nbp):
    n = s // CMP_STRIDE
    c0 = np.arange(n)[None, :] * CMP_STRIDE
    s0 = np.arange(nbp)[:, None] * SEL_BLOCK
    ov = np.clip(np.minimum(c0 + CMP_BLOCK, s0 + SEL_BLOCK) - np.maximum(c0, s0), 0, None) / CMP_STRIDE
    ov[:, n - 1] = 0.0
    ov[s // SEL_BLOCK:] = 0.0
    return jnp.asarray(ov, BF16)


def _layer(x, attn_norm_g, w_in, gla_conv_w, gla_gate_w2, gla_gate_b, gla_norm_g,
           nsa_q_norm_g, nsa_kc_norm_g, nsa_ks_norm_g, nsa_kw_norm_g,
           cmp_k_pos, cmp_k_w1, cmp_k_w2, cmp_v_pos, cmp_v_w1, cmp_v_w2,
           w_out, ffn_norm_g, w_gate, w_up, w_down):
    s = x.shape[0]
    dh = NSA_HEAD_DIM
    row = lambda v: v.reshape(1, -1)
    big = s >= 4096
    tm = 1024 if big else 256

    p = _norm_matmul(x, row(attn_norm_g), _regroup_w_in(w_in), 512 if big else 256,
                     PROJ_PAD // 2 if big else LANES)

    gla_out = _gla(p, gla_conv_w, gla_gate_w2, row(gla_gate_b), row(gla_norm_g), 512)

    qn, ksa, vsa, kwn, vwa = _nsa_prep(p, row(nsa_q_norm_g), row(nsa_ks_norm_g), row(nsa_kw_norm_g), 512)

    w1 = jnp.stack([cmp_k_w1, cmp_v_w1])
    w1s = (w1.reshape(2, 2, CMP_STRIDE, dh, CMP_HIDDEN).transpose(0, 2, 3, 1, 4)
           .reshape(2, CMP_STRIDE, dh, 2 * CMP_HIDDEN).astype(BF16))
    pos = jnp.stack([cmp_k_pos, cmp_v_pos]).reshape(2, 1, CMP_BLOCK * dh)
    w2 = jnp.stack([cmp_k_w2, cmp_v_w2]).astype(BF16)
    cmp = _compress(p, w1s, pos, w1, w2, row(nsa_kc_norm_g))

    nb = s // SEL_BLOCK
    nbp = -(-nb // LANES) * LANES
    gq = row(nsa_q_norm_g)
    o_cmp, bias = _cmp_attn(qn, cmp, _overlap_t(s, nbp), p, gq, row(nsa_kc_norm_g), 128, min(SEL_TOPK, nb))
    o_cw = _win_attn(qn, kwn, vwa, p, o_cmp, gq, row(nsa_kw_norm_g), 256)
    nsa_out = _sel_attn(qn, bias, ksa, vsa, p, o_cw, gq, row(nsa_ks_norm_g),
                        512 if big else 256, 1024 if big else 512)

    h = _out_proj(x, gla_out, nsa_out, w_out.astype(BF16), 512 if big else 256)
    return _ffn(h, row(ffn_norm_g), w_gate.astype(BF16), w_up.astype(BF16), w_down.astype(BF16),
                512 if big else 256, 512)


def kernel(x, attn_norm_g, w_in, gla_conv_w, gla_gate_w2, gla_gate_b, gla_norm_g, nsa_q_norm_g, nsa_kc_norm_g, nsa_ks_norm_g, nsa_kw_norm_g, cmp_k_pos, cmp_k_w1, cmp_k_w2, cmp_v_pos, cmp_v_w1, cmp_v_w2, w_out, ffn_norm_g, w_gate, w_up, w_down):
    assert x.shape[0] == 1 and attn_norm_g.shape[0] == 1
    y = _layer(x[0], attn_norm_g[0], w_in[0], gla_conv_w[0], gla_gate_w2[0], gla_gate_b[0],
               gla_norm_g[0], nsa_q_norm_g[0], nsa_kc_norm_g[0], nsa_ks_norm_g[0], nsa_kw_norm_g[0],
               cmp_k_pos[0], cmp_k_w1[0], cmp_k_w2[0], cmp_v_pos[0], cmp_v_w1[0], cmp_v_w2[0],
               w_out[0], ffn_norm_g[0], w_gate[0], w_up[0], w_down[0])
    return y[None]
```

```python
import functools

import jax
import jax.numpy as jnp
import numpy as np
from jax import lax
from jax.experimental import pallas as pl
from jax.experimental.pallas import tpu as pltpu

D_MODEL = 2048
GLA_HEADS = 4
GLA_DK = 128
GLA_DV = 256
GLA_GATE_RANK = 16
GLA_GATE_TAU = 16.0
GLA_CHUNK = 64
GLA_SUB = 16
GLA_SAFE_LOG_DECAY = 60.0
SOFTMAX_SHIFT_LIMIT = 40.0
CONV_WIDTH = 4

NSA_HEADS = 8
NSA_KV_HEADS = 2
NSA_GROUP = 4
NSA_HEAD_DIM = 128
CMP_BLOCK = 32
CMP_STRIDE = 16
CMP_HIDDEN = 128
SEL_BLOCK = 64
SEL_TOPK = 16
WINDOW = 512
D_FF = 5632
EPS = 1e-6
NEG = -1e30

GLA_QK = GLA_HEADS * GLA_DK
GLA_VW = GLA_HEADS * GLA_DV
GLA_QKV = 2 * GLA_QK + GLA_VW
NSA_QW = NSA_HEADS * NSA_HEAD_DIM
NSA_KVW = NSA_KV_HEADS * NSA_HEAD_DIM

LANES = 128
VMEM_LIMIT = 56 * 1024 * 1024

C_GQ = 0
C_GK = GLA_QK
C_GV = 2 * GLA_QK
C_GO = GLA_QKV
C_NQ = C_GO + GLA_VW
C_KC = C_NQ + NSA_QW
C_VC = C_KC + NSA_KVW
C_KS = C_VC + NSA_KVW
C_VS = C_KS + NSA_KVW
C_KW = C_VS + NSA_KVW
C_VW = C_KW + NSA_KVW
C_MISC = C_VW + NSA_KVW
PROJ_PAD = C_MISC + 2 * LANES
GATE_LANE = GLA_GATE_RANK

F32 = jnp.float32
BF16 = jnp.bfloat16


def _params(sem):
    return pltpu.CompilerParams(dimension_semantics=sem, vmem_limit_bytes=VMEM_LIMIT)


def _nt_dot(a, b):
    return lax.dot_general(a, b, (((1,), (1,)), ((), ())), preferred_element_type=F32)


def _sigmoid(x):
    return 1.0 / (1.0 + jnp.exp(-x))


def _rms(x, g):
    return x * lax.rsqrt(jnp.mean(x * x, axis=-1, keepdims=True) + EPS) * g


def _norm_matmul_kernel(x_ref, g_ref, w_ref, o_ref, xn_ref):
    @pl.when(pl.program_id(1) == 0)
    def _():
        xn_ref[...] = _rms(x_ref[...], g_ref[...]).astype(BF16)

    o_ref[...] = jnp.dot(xn_ref[...], w_ref[...], preferred_element_type=F32)


def _norm_matmul(x, g, w, tm, tn):
    s, d = x.shape
    n = w.shape[1]
    return pl.pallas_call(
        _norm_matmul_kernel,
        grid=(s // tm, n // tn),
        in_specs=[pl.BlockSpec((tm, d), lambda i, j: (i, 0)),
                  pl.BlockSpec((1, d), lambda i, j: (0, 0)),
                  pl.BlockSpec((d, tn), lambda i, j: (0, j))],
        out_specs=pl.BlockSpec((tm, tn), lambda i, j: (i, j)),
        out_shape=jax.ShapeDtypeStruct((s, n), F32),
        scratch_shapes=[pltpu.VMEM((tm, d), BF16)],
        compiler_params=_params(("arbitrary", "arbitrary")),
        name="in_proj",
    )(x, g, w)


def _gla_kernel(q_ref, qp_ref, k_ref, kp_ref, v_ref, vp_ref, lr_ref, go_ref,
                cwq_ref, cwk_ref, cwv_ref, w2_ref, gb_ref, ng_ref, o_ref,
                st_ref, sq_ref, sk_ref, sv_ref):
    t_rows = q_ref.shape[0]
    pad = GLA_SUB
    first = pl.program_id(1) == 0

    @pl.when(first)
    def _():
        st_ref[...] = jnp.zeros_like(st_ref)

    def conv_silu(u_ref, p_ref, w_ref, s_ref):
        w = w_ref[...]
        s_ref[0:8, :] = jnp.where(first, 0.0, p_ref[...])
        s_ref[8:16, :] = u_ref[0:8, :]
        head = u_ref[0:8, :] * w[CONV_WIDTH - 1:CONV_WIDTH]
        body = u_ref[8:, :] * w[CONV_WIDTH - 1:CONV_WIDTH]
        for d in range(1, CONV_WIDTH):
            wd = w[CONV_WIDTH - 1 - d:CONV_WIDTH - d]
            head = head + s_ref[pl.ds(8 - d, 8), :] * wd
            body = body + u_ref[pl.ds(8 - d, t_rows - 8), :] * wd
        acc = jnp.concatenate([head, body], axis=0)
        return acc * _sigmoid(acc)

    q = conv_silu(q_ref, qp_ref, cwq_ref, sq_ref) * (GLA_DK ** -0.5)
    k = conv_silu(k_ref, kp_ref, cwk_ref, sk_ref)
    v = conv_silu(v_ref, vp_ref, cwv_ref, sv_ref)

    z = jnp.dot(lr_ref[:, :GLA_GATE_RANK], w2_ref[...], preferred_element_type=F32,
                precision=lax.Precision.HIGHEST) + gb_ref[...]
    log_a = (jnp.minimum(z, 0.0) - jnp.log(1.0 + jnp.exp(-jnp.abs(z)))) * (1.0 / GLA_GATE_TAU)
    row = lax.broadcasted_iota(jnp.int32, log_a.shape, 0)
    lane = lax.broadcasted_iota(jnp.int32, log_a.shape, 1)
    rc = row & (GLA_CHUNK - 1)
    b = log_a
    d = 1
    while d < GLA_CHUNK:
        b = b + jnp.where(rc >= d, pltpu.roll(b, d, 0), 0.0)
        d *= 2

    srow = lax.broadcasted_iota(jnp.int32, (GLA_CHUNK, GLA_DK), 0)
    scol = lax.broadcasted_iota(jnp.int32, (GLA_CHUNK, GLA_DK), 1)
    zk = jnp.zeros((LANES - GLA_CHUNK, GLA_DK), BF16)
    zv = jnp.zeros((LANES - GLA_CHUNK, GLA_DV), BF16)
    za = jnp.zeros((GLA_SUB, LANES), F32)
    gain = ng_ref[...]

    def chunk_loop(intra_scores):
        for c in range(t_rows // GLA_CHUNK):
            lo = c * GLA_CHUNK
            bc = b[lo:lo + GLA_CHUNK]
            qc = q[lo:lo + GLA_CHUNK]
            kc = k[lo:lo + GLA_CHUNK]
            vc = v[lo:lo + GLA_CHUNK].astype(BF16)
            st = st_ref[...]
            b_last = bc[GLA_CHUNK - 1:GLA_CHUNK]
            qd = (qc * jnp.exp(bc)).astype(BF16)
            scores = intra_scores(lo, bc, qc, kc, qd)
            o = _nt_dot(qd, st.astype(BF16)) + jnp.dot(
                scores.astype(BF16), jnp.concatenate([vc, zv], axis=0), preferred_element_type=F32)
            go = go_ref[lo:lo + GLA_CHUNK, :]
            o_ref[lo:lo + GLA_CHUNK, :] = _rms(o, gain) * (go * _sigmoid(go))
            kt = (kc * jnp.exp(b_last - bc)).astype(BF16)
            upd = lax.dot_general(vc, kt, (((0,), (0,)), ((), ())), preferred_element_type=F32)
            st_ref[...] = st * jnp.exp(b_last) + upd

    small_decay = jnp.min(b) >= -GLA_SAFE_LOG_DECAY

    @pl.when(small_decay)
    def _():
        def intra_scores(lo, bc, qc, kc, qd):
            kd = (kc * jnp.exp(-bc)).astype(BF16)
            a = _nt_dot(qd, jnp.concatenate([kd, zk], axis=0))
            return jnp.where(scol <= srow, a, 0.0)

        chunk_loop(intra_scores)

    @pl.when(jnp.logical_not(small_decay))
    def _():
        zero_pad = jnp.zeros((pad, GLA_DK), F32)
        sq_ref[0:pad, :] = zero_pad
        sk_ref[0:pad, :] = zero_pad
        sq_ref[pad:, :] = b
        sk_ref[pad:, :] = k
        rs = row & (GLA_SUB - 1)
        dl = rc - lane
        band = jnp.where(dl == 0, jnp.sum(q * k, axis=-1, keepdims=True), 0.0)
        for delta in range(1, GLA_SUB):
            e = jnp.exp(jnp.where(rs >= delta, b - sq_ref[pl.ds(pad - delta, t_rows), :], -jnp.inf))
            sc = jnp.sum(q * sk_ref[pl.ds(pad - delta, t_rows), :] * e, axis=-1, keepdims=True)
            band = jnp.where(dl == delta, sc, band)

        def intra_scores(lo, bc, qc, kc, qd):
            blocks = [za]
            for sub in range(1, GLA_CHUNK // GLA_SUB):
                r0 = sub * GLA_SUB
                ref_b = bc[r0:r0 + 1]
                qq = (qc[r0:r0 + GLA_SUB] * jnp.exp(bc[r0:r0 + GLA_SUB] - ref_b)).astype(BF16)
                kk = (kc * jnp.exp(jnp.where(srow < r0, ref_b - bc, -jnp.inf))).astype(BF16)
                blocks.append(_nt_dot(qq, jnp.concatenate([kk, zk], axis=0)))
            return band[lo:lo + GLA_CHUNK] + jnp.concatenate(blocks, axis=0)

        chunk_loop(intra_scores)


def _gla(p, conv_w, w2, gb, ng, tile):
    s = p.shape[0]
    t8 = tile // 8

    def prev(col):
        return lambda h, i: (jnp.maximum(i * t8 - 1, 0), col(h))

    qcol = lambda h: C_GQ // GLA_DK + h
    kcol = lambda h: C_GK // GLA_DK + h
    vcol = lambda h: C_GV // GLA_DV + h
    return pl.pallas_call(
        _gla_kernel,
        grid=(GLA_HEADS, s // tile),
        in_specs=[
            pl.BlockSpec((tile, GLA_DK), lambda h, i: (i, qcol(h))),
            pl.BlockSpec((8, GLA_DK), prev(qcol)),
            pl.BlockSpec((tile, GLA_DK), lambda h, i: (i, kcol(h))),
            pl.BlockSpec((8, GLA_DK), prev(kcol)),
            pl.BlockSpec((tile, GLA_DV), lambda h, i: (i, vcol(h))),
            pl.BlockSpec((8, GLA_DV), prev(vcol)),
            pl.BlockSpec((tile, LANES), lambda h, i: (i, C_MISC // LANES)),
            pl.BlockSpec((tile, GLA_DV), lambda h, i: (i, C_GO // GLA_DV + h)),
            pl.BlockSpec((CONV_WIDTH, GLA_DK), lambda h, i: (0, qcol(h))),
            pl.BlockSpec((CONV_WIDTH, GLA_DK), lambda h, i: (0, kcol(h))),
            pl.BlockSpec((CONV_WIDTH, GLA_DV), lambda h, i: (0, vcol(h))),
            pl.BlockSpec((GLA_GATE_RANK, GLA_DK), lambda h, i: (0, h)),
            pl.BlockSpec((1, GLA_DK), lambda h, i: (0, h)),
            pl.BlockSpec((1, GLA_DV), lambda h, i: (0, 0)),
        ],
        out_specs=pl.BlockSpec((tile, GLA_DV), lambda h, i: (i, h)),
        out_shape=jax.ShapeDtypeStruct((s, GLA_VW), F32),
        scratch_shapes=[pltpu.VMEM((GLA_DV, GLA_DK), F32),
                        pltpu.VMEM((tile + GLA_SUB, GLA_DK), F32),
                        pltpu.VMEM((tile + GLA_SUB, GLA_DK), F32),
                        pltpu.VMEM((GLA_SUB, GLA_DV), F32)],
        compiler_params=_params(("arbitrary", "arbitrary")),
        name="gla",
    )(p, p, p, p, p, p, p, p, conv_w, conv_w, conv_w, w2, gb, ng)


def _nsa_prep_kernel(q_ref, ks_ref, vs_ref, kw_ref, vw_ref, qg_ref, ksg_ref, kwg_ref,
                     qn_ref, ksa_ref, vsa_ref, kwn_ref, vwa_ref):
    t_rows = q_ref.shape[0]
    dh = NSA_HEAD_DIM
    scale = dh ** -0.5
    ones = jnp.ones((t_rows, dh), BF16)
    for h in range(NSA_HEADS):
        sl = slice(h * dh, (h + 1) * dh)
        qn_ref[:, sl] = (_rms(q_ref[:, sl], qg_ref[...]) * scale).astype(BF16)
    pos = pl.program_id(0) * t_rows + lax.broadcasted_iota(jnp.int32, (t_rows, LANES), 0)
    lane = lax.broadcasted_iota(jnp.int32, (t_rows, LANES), 1)
    onehot = jnp.where(lane == ((pos // SEL_BLOCK) & (LANES - 1)), 1.0, 0.0).astype(BF16)
    for h in range(NSA_KV_HEADS):
        sl = slice(h * dh, (h + 1) * dh)
        ksa_ref[:, 2 * h * dh:(2 * h + 1) * dh] = _rms(ks_ref[:, sl], ksg_ref[...]).astype(BF16)
        ksa_ref[:, (2 * h + 1) * dh:(2 * h + 2) * dh] = onehot
        kwn_ref[:, sl] = _rms(kw_ref[:, sl], kwg_ref[...]).astype(BF16)
        vsa_ref[:, 2 * h * dh:(2 * h + 1) * dh] = vs_ref[:, sl].astype(BF16)
        vsa_ref[:, (2 * h + 1) * dh:(2 * h + 2) * dh] = ones
        vwa_ref[:, 2 * h * dh:(2 * h + 1) * dh] = vw_ref[:, sl].astype(BF16)
        vwa_ref[:, (2 * h + 1) * dh:(2 * h + 2) * dh] = ones


def _nsa_prep(p, qg, ksg, kwg, tile):
    s = p.shape[0]
    kv = NSA_KVW
    col = lambda c, w: (lambda i: (i, c // w))
    row = lambda w: pl.BlockSpec((tile, w), lambda i: (i, 0))
    gain = pl.BlockSpec((1, NSA_HEAD_DIM), lambda i: (0, 0))
    return pl.pallas_call(
        _nsa_prep_kernel,
        grid=(s // tile,),
        in_specs=[pl.BlockSpec((tile, NSA_QW), col(C_NQ, NSA_QW)),
                  pl.BlockSpec((tile, kv), col(C_KS, kv)),
                  pl.BlockSpec((tile, kv), col(C_VS, kv)),
                  pl.BlockSpec((tile, kv), col(C_KW, kv)),
                  pl.BlockSpec((tile, kv), col(C_VW, kv)),
                  gain, gain, gain],
        out_specs=[row(NSA_QW), row(2 * kv), row(2 * kv), row(kv), row(2 * kv)],
        out_shape=[jax.ShapeDtypeStruct((s, NSA_QW), BF16),
                   jax.ShapeDtypeStruct((s, 2 * kv), BF16),
                   jax.ShapeDtypeStruct((s, 2 * kv), BF16),
                   jax.ShapeDtypeStruct((s, kv), BF16),
                   jax.ShapeDtypeStruct((s, 2 * kv), BF16)],
        compiler_params=_params(("arbitrary",)),
        name="nsa_prep",
    )(p, p, p, p, p, qg, ksg, kwg)


def _compress_kernel(u_ref, w1_ref, pos_ref, w1f_ref, w2_ref, g_ref, o_ref, oa_ref):
    n = o_ref.shape[1]
    half = CMP_BLOCK // CMP_STRIDE
    assert half == 2
    acc = jnp.zeros((n, 2 * CMP_HIDDEN), F32)
    for l in range(CMP_STRIDE):
        x = u_ref[pl.ds(l, n, stride=CMP_STRIDE), :].astype(BF16)
        acc = acc + jnp.dot(x, w1_ref[0, l], preferred_element_type=F32)
    posb = jnp.dot(jnp.broadcast_to(pos_ref[0], (8, pos_ref.shape[2])), w1f_ref[0],
                   preferred_element_type=F32, precision=lax.Precision.HIGHEST)[0:1]
    hid = acc[:, :CMP_HIDDEN] + pltpu.roll(acc[:, CMP_HIDDEN:], n - 1, 0) + posb
    hid = hid * _sigmoid(hid)
    out = jnp.dot(hid.astype(BF16), w2_ref[0], preferred_element_type=F32)
    is_k = pl.program_id(0) < NSA_KV_HEADS
    out = jnp.where(is_k, _rms(out, g_ref[...]), out)
    row = lax.broadcasted_iota(jnp.int32, out.shape, 0)
    out = jnp.where(row < n - 1, out, 0.0).astype(BF16)
    o_ref[0] = out
    oa_ref[0] = jnp.concatenate([out, jnp.ones_like(out)], axis=1)


def _compress(p, w1s, pos, w1f, w2, g):
    s = p.shape[0]
    n = s // CMP_STRIDE
    dh = NSA_HEAD_DIM
    return pl.pallas_call(
        _compress_kernel,
        grid=(2 * NSA_KV_HEADS,),
        in_specs=[pl.BlockSpec((s, dh), lambda j: (0, C_KC // dh + j)),
                  pl.BlockSpec((1, CMP_STRIDE, dh, 2 * CMP_HIDDEN), lambda j: (j // 2, 0, 0, 0)),
                  pl.BlockSpec((1, 1, CMP_BLOCK * dh), lambda j: (j // 2, 0, 0)),
                  pl.BlockSpec((1, CMP_BLOCK * dh, CMP_HIDDEN), lambda j: (j // 2, 0, 0)),
                  pl.BlockSpec((1, CMP_HIDDEN, dh), lambda j: (j // 2, 0, 0)),
                  pl.BlockSpec((1, dh), lambda j: (0, 0))],
        out_specs=[pl.BlockSpec((1, n, dh), lambda j: (j, 0, 0)),
                   pl.BlockSpec((1, n, 2 * dh), lambda j: (j, 0, 0))],
        out_shape=[jax.ShapeDtypeStruct((2 * NSA_KV_HEADS, n, dh), BF16),
                   jax.ShapeDtypeStruct((2 * NSA_KV_HEADS, n, 2 * dh), BF16)],
        compiler_params=_params(("arbitrary",)),
        name="compress",
    )(p, w1s, pos, w1f, w2, g)


def _softmax_shift(gq_ref, gk_ref):
    bound = (NSA_HEAD_DIM ** 0.5) * jnp.max(jnp.abs(gq_ref[...])) * jnp.max(jnp.abs(gk_ref[...]))
    usable = bound <= SOFTMAX_SHIFT_LIMIT
    return usable, jnp.where(usable, bound, 0.0)


def _unstack_heads(o, tq):
    return jnp.concatenate([o[g * tq:(g + 1) * tq] for g in range(NSA_GROUP)], axis=1)


def _branch_gate(misc, branch, tq):
    sg = _sigmoid(misc)
    base = GATE_LANE + branch * NSA_GROUP
    return jnp.concatenate([sg[:, base + g:base + g + 1] for g in range(NSA_GROUP)], axis=0)


def _cmp_attn_kernel(q_ref, kc_ref, va_ref, ovl_ref, misc_ref, gq_ref, gk_ref, o_ref, bias_ref, e_ref,
                     *, n_sel):
    tq = q_ref.shape[0]
    dh = NSA_HEAD_DIM
    nc = kc_ref.shape[1]
    nbp = ovl_ref.shape[0]
    i = pl.program_id(0)
    s0 = i * tq
    col_step = max(LANES, nc // 4)
    n_var = nc // col_step
    tiles_per_var = (nc * CMP_STRIDE // tq) // n_var
    assert tiles_per_var * tq == col_step * CMP_STRIDE

    fixed, shift = _softmax_shift(gq_ref, gk_ref)

    def variant(ncv, nbv, online):
        q = q_ref[...]
        row = lax.broadcasted_iota(jnp.int32, (tq, ncv), 0)
        col = lax.broadcasted_iota(jnp.int32, (tq, ncv), 1)
        maskb = jnp.where(col * CMP_STRIDE + (CMP_BLOCK - 1) <= s0 + row, -shift, NEG)
        has_valid = (s0 + lax.broadcasted_iota(jnp.int32, (tq, 1), 0)) >= CMP_BLOCK - 1
        sg = _sigmoid(misc_ref[...])
        ovl = ovl_ref[0:nbv, 0:ncv]
        imps = []
        for h in range(NSA_KV_HEADS):
            kc = kc_ref[h, 0:ncv, :]
            va = va_ref[h, 0:ncv, :]
            invs = []
            for g in range(NSA_GROUP):
                hd = h * NSA_GROUP + g
                sm = _nt_dot(q[:, hd * dh:(hd + 1) * dh], kc) + maskb
                if online:
                    sm = sm - jnp.max(sm, axis=-1, keepdims=True)
                e = jnp.exp(sm)
                e_ref[hd, :, 0:ncv] = e
                acc = jnp.dot(e.astype(BF16), va, preferred_element_type=F32)
                inv = jnp.where(has_valid, 1.0 / acc[:, dh:], 0.0)
                gl = h * LANES + GATE_LANE + g
                o_ref[:, hd * dh:(hd + 1) * dh] = acc[:, :dh] * inv * sg[:, gl:gl + 1]
                invs.append(inv)
            ps = None
            for g in range(NSA_GROUP):
                pc = e_ref[h * NSA_GROUP + g, :, 0:ncv] * jnp.tile(invs[g], (1, ncv // LANES))
                ps = pc if ps is None else ps + pc
            hi = ps.astype(BF16)
            r1 = ps - hi.astype(F32)
            mid = r1.astype(BF16)
            lo = (r1 - mid.astype(F32)).astype(BF16)
            imps.append(_nt_dot(ovl, hi) + _nt_dot(ovl, mid) + _nt_dot(ovl, lo))

        jb = lax.broadcasted_iota(jnp.int32, (nbv, tq), 0)
        jt = (s0 + lax.broadcasted_iota(jnp.int32, (nbv, tq), 1)) // SEL_BLOCK
        forced = (jb == 0) | (jb == jt) | (jb == jt - 1)
        future = jb > jt
        curs = [jnp.where(future | forced, -jnp.inf, imp) for imp in imps]
        for _ in range(n_sel - 3):
            for h in range(NSA_KV_HEADS):
                mx = jnp.max(curs[h], axis=0, keepdims=True)
                idx = jnp.min(jnp.where(curs[h] == mx, jb, nbv), axis=0, keepdims=True)
                curs[h] = jnp.where(jb == idx, -jnp.inf, curs[h])
        for h in range(NSA_KV_HEADS):
            bias_t = jnp.where((curs[h] == -jnp.inf) & jnp.logical_not(future), 0.0, NEG)
            bias_ref[:, h * nbp:h * nbp + nbv] = bias_t.T.astype(BF16)
            if nbv < nbp:
                bias_ref[:, h * nbp + nbv:(h + 1) * nbp] = jnp.full((tq, nbp - nbv), NEG, BF16)

    for v in range(n_var):
        ncv = col_step * (v + 1)
        nbv = min(nbp, -(-(ncv // (SEL_BLOCK // CMP_STRIDE)) // LANES) * LANES)
        pl.when(fixed & (i // tiles_per_var == v))(functools.partial(variant, ncv, nbv, False))
    pl.when(jnp.logical_not(fixed))(functools.partial(variant, nc, nbp, True))


def _cmp_attn(qn, cmp, cmp_a, ovl_t, p, gq, gk, tq, n_sel):
    s = qn.shape[0]
    nc = cmp.shape[1]
    nbp = ovl_t.shape[0]
    assert n_sel >= 3
    misc_w = NSA_KV_HEADS * LANES
    gain = pl.BlockSpec((1, NSA_HEAD_DIM), lambda i: (0, 0))
    return pl.pallas_call(
        functools.partial(_cmp_attn_kernel, n_sel=n_sel),
        grid=(s // tq,),
        in_specs=[pl.BlockSpec((tq, NSA_QW), lambda i: (i, 0)),
                  pl.BlockSpec((NSA_KV_HEADS, nc, NSA_HEAD_DIM), lambda i: (0, 0, 0)),
                  pl.BlockSpec((NSA_KV_HEADS, nc, 2 * NSA_HEAD_DIM), lambda i: (1, 0, 0)),
                  pl.BlockSpec((nbp, nc), lambda i: (0, 0)),
                  pl.BlockSpec((tq, misc_w), lambda i: (i, C_MISC // misc_w)),
                  gain, gain],
        out_specs=[pl.BlockSpec((tq, NSA_QW), lambda i: (i, 0)),
                   pl.BlockSpec((tq, NSA_KV_HEADS * nbp), lambda i: (i, 0))],
        out_shape=[jax.ShapeDtypeStruct((s, NSA_QW), F32),
                   jax.ShapeDtypeStruct((s, NSA_KV_HEADS * nbp), BF16)],
        scratch_shapes=[pltpu.VMEM((NSA_HEADS, tq, nc), F32)],
        compiler_params=_params(("arbitrary",)),
        name="cmp_attn",
    )(qn, cmp, cmp_a, ovl_t, p, gq, gk)


def _win_attn_kernel(q_ref, k_ref, v_ref, misc_ref, prev_ref, gq_ref, gk_ref, o_ref, *, span):
    tq = q_ref.shape[0]
    dh = NSA_HEAD_DIM
    s0 = pl.program_id(1) * tq
    start = pl.multiple_of(jnp.maximum(s0 - WINDOW, 0), tq)
    fixed, shift = _softmax_shift(gq_ref, gk_ref)
    row = lax.broadcasted_iota(jnp.int32, (tq, span), 0)
    col = lax.broadcasted_iota(jnp.int32, (tq, span), 1)
    dist = (s0 - start) + row - col
    maskb = jnp.where((dist >= 0) & (dist < WINDOW), -shift, NEG)

    def run(online):
        q = q_ref[...]
        k = k_ref[pl.ds(start, span), :]
        v = v_ref[pl.ds(start, span), :]
        sg = _sigmoid(misc_ref[...])
        base = GATE_LANE + 2 * NSA_GROUP
        for g in range(NSA_GROUP):
            sm = _nt_dot(q[:, g * dh:(g + 1) * dh], k) + maskb
            if online:
                sm = sm - jnp.max(sm, axis=-1, keepdims=True)
            acc = jnp.dot(jnp.exp(sm).astype(BF16), v, preferred_element_type=F32)
            o = acc[:, :dh] * (sg[:, base + g:base + g + 1] / acc[:, dh:])
            o_ref[:, g * dh:(g + 1) * dh] = prev_ref[:, g * dh:(g + 1) * dh] + o

    pl.when(fixed)(functools.partial(run, False))
    pl.when(jnp.logical_not(fixed))(functools.partial(run, True))


def _win_attn(qn, kwn, vwa, p, prev, gq, gk, tq):
    s = qn.shape[0]
    gw = NSA_GROUP * NSA_HEAD_DIM
    span = min(s, WINDOW + tq)
    gain = pl.BlockSpec((1, NSA_HEAD_DIM), lambda h, i: (0, 0))
    return pl.pallas_call(
        functools.partial(_win_attn_kernel, span=span),
        grid=(NSA_KV_HEADS, s // tq),
        in_specs=[pl.BlockSpec((tq, gw), lambda h, i: (i, h)),
                  pl.BlockSpec((s, NSA_HEAD_DIM), lambda h, i: (0, h)),
                  pl.BlockSpec((s, 2 * NSA_HEAD_DIM), lambda h, i: (0, h)),
                  pl.BlockSpec((tq, LANES), lambda h, i: (i, C_MISC // LANES + h)),
                  pl.BlockSpec((tq, gw), lambda h, i: (i, h)),
                  gain, gain],
        out_specs=pl.BlockSpec((tq, gw), lambda h, i: (i, h)),
        out_shape=jax.ShapeDtypeStruct((s, NSA_QW), F32),
        compiler_params=_params(("arbitrary", "arbitrary")),
        name="win_attn",
    )(qn, kwn, vwa, p, prev, gq, gk)


def _sel_attn_kernel(q_ref, bias_ref, ka_ref, va_ref, misc_ref, prev_ref, gq_ref, gk_ref, o_ref,
                     lhs_ref, m_ref, acc_ref, sa_ref, sb_ref, *, tk):
    tq = q_ref.shape[0]
    dh = NSA_HEAD_DIM
    n_super = lhs_ref.shape[0]
    tiles_per_super = LANES * SEL_BLOCK // tk
    s0 = pl.program_id(1) * tq
    fixed, shift = _softmax_shift(gq_ref, gk_ref)
    for sup in range(n_super):
        for g in range(NSA_GROUP):
            lhs_ref[sup, g * tq:(g + 1) * tq, 0:dh] = q_ref[:, g * dh:(g + 1) * dh]
            lhs_ref[sup, g * tq:(g + 1) * tq, dh:2 * dh] = (
                bias_ref[:, sup * LANES:(sup + 1) * LANES].astype(F32) - shift).astype(BF16)
    m_ref[...] = jnp.full_like(m_ref, -jnp.inf)
    acc_ref[...] = jnp.zeros_like(acc_ref)

    def scores(kt, sc_ref):
        k0 = pl.multiple_of(kt * tk, tk)
        k = ka_ref[pl.ds(k0, tk), :]
        sup = kt // tiles_per_super
        for g in range(NSA_GROUP):
            rows = slice(g * tq, (g + 1) * tq)
            sc_ref[rows, :] = _nt_dot(lhs_ref[sup, rows, :], k)

    def accumulate(online, kt, sc_ref, diagonal):
        k0 = pl.multiple_of(kt * tk, tk)
        v = va_ref[pl.ds(k0, tk), :]
        for g in range(NSA_GROUP):
            rows = slice(g * tq, (g + 1) * tq)
            sc = sc_ref[rows, :]
            if diagonal:
                row = lax.broadcasted_iota(jnp.int32, sc.shape, 0)
                col = lax.broadcasted_iota(jnp.int32, sc.shape, 1)
                sc = jnp.where(k0 + col <= s0 + row, sc, NEG)
            if online:
                m_old = m_ref[rows, :]
                m_new = jnp.maximum(m_old, jnp.max(sc, axis=-1, keepdims=True))
                alpha = jnp.exp(m_old - m_new)
                pr = jnp.exp(sc - jnp.tile(m_new, (1, tk // LANES)))
                acc_ref[rows, :] = (jnp.tile(alpha, (1, 2)) * acc_ref[rows, :]
                                    + jnp.dot(pr.astype(BF16), v, preferred_element_type=F32))
                m_ref[rows, :] = m_new
            else:
                acc_ref[rows, :] += jnp.dot(jnp.exp(sc).astype(BF16), v, preferred_element_type=F32)

    n_full = s0 // tk

    def run(online):
        acc_fn = functools.partial(accumulate, online)

        def body(j, carry):
            scores(2 * j + 1, sb_ref)
            acc_fn(2 * j, sa_ref, False)
            scores(2 * j + 2, sa_ref)
            acc_fn(2 * j + 1, sb_ref, False)
            return carry

        scores(0, sa_ref)
        lax.fori_loop(0, n_full // 2, body, 0)

        @pl.when(n_full % 2 == 0)
        def _():
            acc_fn(n_full, sa_ref, True)

        @pl.when(n_full % 2 == 1)
        def _():
            scores(n_full, sb_ref)
            acc_fn(n_full - 1, sa_ref, False)
            acc_fn(n_full, sb_ref, True)

    pl.when(fixed)(functools.partial(run, False))
    pl.when(jnp.logical_not(fixed))(functools.partial(run, True))

    acc = acc_ref[...]
    o = acc[:, :dh] * (_branch_gate(misc_ref[...], 1, tq) / acc[:, dh:])
    o_ref[...] = prev_ref[...] + _unstack_heads(o, tq)


def _sel_attn(qn, bias, ksa, vsa, p, prev, gq, gk, tq, tk):
    s = qn.shape[0]
    dh = NSA_HEAD_DIM
    gw = NSA_GROUP * dh
    nbp = bias.shape[1] // NSA_KV_HEADS
    n_super = nbp // LANES
    assert tk % tq == 0 and (LANES * SEL_BLOCK) % tk == 0 and s % tk == 0
    resident = lambda: pl.BlockSpec((s, 2 * dh), lambda h, i: (0, h), pipeline_mode=pl.Buffered(1))
    gain = pl.BlockSpec((1, dh), lambda h, i: (0, 0))
    return pl.pallas_call(
        functools.partial(_sel_attn_kernel, tk=tk),
        grid=(NSA_KV_HEADS, s // tq),
        in_specs=[pl.BlockSpec((tq, gw), lambda h, i: (i, h)),
                  pl.BlockSpec((tq, nbp), lambda h, i: (i, h)),
                  resident(), resident(),
                  pl.BlockSpec((tq, LANES), lambda h, i: (i, C_MISC // LANES + h)),
                  pl.BlockSpec((tq, gw), lambda h, i: (i, h)),
                  gain, gain],
        out_specs=pl.BlockSpec((tq, gw), lambda h, i: (i, h)),
        out_shape=jax.ShapeDtypeStruct((s, NSA_QW), F32),
        scratch_shapes=[pltpu.VMEM((n_super, NSA_GROUP * tq, 2 * dh), BF16),
                        pltpu.VMEM((NSA_GROUP * tq, LANES), F32),
                        pltpu.VMEM((NSA_GROUP * tq, 2 * dh), F32),
                        pltpu.VMEM((NSA_GROUP * tq, tk), F32),
                        pltpu.VMEM((NSA_GROUP * tq, tk), F32)],
        compiler_params=_params(("arbitrary", "arbitrary")),
        name="sel_attn",
    )(qn, bias, ksa, vsa, p, prev, gq, gk)


def _out_proj_kernel(x_ref, a_ref, b_ref, wa_ref, wb_ref, o_ref):
    o_ref[...] = (x_ref[...]
                  + jnp.dot(a_ref[...].astype(BF16), wa_ref[...], preferred_element_type=F32)
                  + jnp.dot(b_ref[...].astype(BF16), wb_ref[...], preferred_element_type=F32))


def _out_proj(x, a, b, w, tm):
    s, d = x.shape
    ka, kb = a.shape[1], b.shape[1]
    assert ka == kb
    return pl.pallas_call(
        _out_proj_kernel,
        grid=(s // tm,),
        in_specs=[pl.BlockSpec((tm, d), lambda i: (i, 0)),
                  pl.BlockSpec((tm, ka), lambda i: (i, 0)),
                  pl.BlockSpec((tm, kb), lambda i: (i, 0)),
                  pl.BlockSpec((ka, d), lambda i: (0, 0)),
                  pl.BlockSpec((kb, d), lambda i: (1, 0))],
        out_specs=pl.BlockSpec((tm, d), lambda i: (i, 0)),
        out_shape=jax.ShapeDtypeStruct((s, d), F32),
        compiler_params=_params(("arbitrary",)),
        name="out_proj",
    )(x, a, b, w, w)


def _ffn_kernel(h_ref, g_ref, wg_ref, wu_ref, wd_ref, o_ref, hn_ref):
    @pl.when(pl.program_id(1) == 0)
    def _():
        h = h_ref[...]
        hn_ref[...] = _rms(h, g_ref[...]).astype(BF16)
        o_ref[...] = h

    hn = hn_ref[...]
    a = jnp.dot(hn, wg_ref[...], preferred_element_type=F32)
    u = jnp.dot(hn, wu_ref[...], preferred_element_type=F32)
    z = (a * _sigmoid(a) * u).astype(BF16)
    o_ref[...] += jnp.dot(z, wd_ref[...], preferred_element_type=F32)


def _ffn(h, g, wg, wu, wd, tm, tf):
    s, d = h.shape
    f = wg.shape[1]
    return pl.pallas_call(
        _ffn_kernel,
        grid=(s // tm, f // tf),
        in_specs=[pl.BlockSpec((tm, d), lambda i, j: (i, 0)),
                  pl.BlockSpec((1, d), lambda i, j: (0, 0)),
                  pl.BlockSpec((d, tf), lambda i, j: (0, j)),
                  pl.BlockSpec((d, tf), lambda i, j: (0, j)),
                  pl.BlockSpec((tf, d), lambda i, j: (j, 0))],
        out_specs=pl.BlockSpec((tm, d), lambda i, j: (i, 0)),
        out_shape=jax.ShapeDtypeStruct((s, d), F32),
        scratch_shapes=[pltpu.VMEM((tm, d), BF16)],
        compiler_params=_params(("arbitrary", "arbitrary")),
        name="ffn",
    )(h, g, wg, wu, wd)


_W_IN_LR = GLA_QKV
_W_IN_GO = _W_IN_LR + GLA_GATE_RANK
_W_IN_NG = _W_IN_GO + GLA_VW + NSA_QW + 6 * NSA_KVW


def _regroup_kernel(w_ref, misc_ref, o_ref):
    o_ref[:, 0:GLA_QKV] = w_ref[:, 0:GLA_QKV].astype(BF16)
    o_ref[:, GLA_QKV:C_MISC] = w_ref[:, _W_IN_GO:_W_IN_NG].astype(BF16)
    o_ref[:, C_MISC:PROJ_PAD] = misc_ref[...]


def _regroup_w_in(w_in, tile=256):
    d = w_in.shape[0]
    lr = w_in[:, _W_IN_LR:_W_IN_GO]
    ng = w_in[:, _W_IN_NG:_W_IN_NG + 3 * NSA_HEADS].reshape(d, NSA_KV_HEADS, NSA_GROUP, 3)
    ng = ng.transpose(0, 1, 3, 2).reshape(d, NSA_KV_HEADS, 3 * NSA_GROUP)
    pad = lambda n: jnp.zeros((d, n), w_in.dtype)
    tail = pad(LANES - GLA_GATE_RANK - 3 * NSA_GROUP)
    misc = jnp.concatenate([lr, ng[:, 0], tail, pad(GLA_GATE_RANK), ng[:, 1], tail], axis=1).astype(BF16)
    return pl.pallas_call(
        _regroup_kernel,
        grid=(d // tile,),
        in_specs=[pl.BlockSpec((tile, w_in.shape[1]), lambda i: (i, 0)),
                  pl.BlockSpec((tile, 2 * LANES), lambda i: (i, 0))],
        out_specs=pl.BlockSpec((tile, PROJ_PAD), lambda i: (i, 0)),
        out_shape=jax.ShapeDtypeStruct((d, PROJ_PAD), BF16),
        compiler_params=_params(("arbitrary",)),
        name="regroup_w_in",
    )(w_in, misc)


def _overlap_t(s, nbp):
    n = s // CMP_STRIDE
    c0 = np.arange(n)[None, :] * CMP_STRIDE
    s0 = np.arange(nbp)[:, None] * SEL_BLOCK
    ov = np.clip(np.minimum(c0 + CMP_BLOCK, s0 + SEL_BLOCK) - np.maximum(c0, s0), 0, None) / CMP_STRIDE
    ov[:, n - 1] = 0.0
    ov[s // SEL_BLOCK:] = 0.0
    return jnp.asarray(ov, BF16)


def _layer(x, attn_norm_g, w_in, gla_conv_w, gla_gate_w2, gla_gate_b, gla_norm_g,
           nsa_q_norm_g, nsa_kc_norm_g, nsa_ks_norm_g, nsa_kw_norm_g,
           cmp_k_pos, cmp_k_w1, cmp_k_w2, cmp_v_pos, cmp_v_w1, cmp_v_w2,
           w_out, ffn_norm_g, w_gate, w_up, w_down):
    s = x.shape[0]
    dh = NSA_HEAD_DIM
    row = lambda v: v.reshape(1, -1)
    big = s >= 4096
    tm = 1024 if big else 256

    p = _norm_matmul(x, row(attn_norm_g), _regroup_w_in(w_in), 512 if big else 256,
                     PROJ_PAD // 2 if big else LANES)

    gla_out = _gla(p, gla_conv_w, gla_gate_w2, row(gla_gate_b), row(gla_norm_g), 512)

    qn, ksa, vsa, kwn, vwa = _nsa_prep(p, row(nsa_q_norm_g), row(nsa_ks_norm_g), row(nsa_kw_norm_g), 512)

    w1 = jnp.stack([cmp_k_w1, cmp_v_w1])
    w1s = (w1.reshape(2, 2, CMP_STRIDE, dh, CMP_HIDDEN).transpose(0, 2, 3, 1, 4)
           .reshape(2, CMP_STRIDE, dh, 2 * CMP_HIDDEN).astype(BF16))
    pos = jnp.stack([cmp_k_pos, cmp_v_pos]).reshape(2, 1, CMP_BLOCK * dh)
    w2 = jnp.stack([cmp_k_w2, cmp_v_w2]).astype(BF16)
    cmp, cmp_a = _compress(p, w1s, pos, w1, w2, row(nsa_kc_norm_g))

    nb = s // SEL_BLOCK
    nbp = -(-nb // LANES) * LANES
    gq = row(nsa_q_norm_g)
    o_cmp, bias = _cmp_attn(qn, cmp, cmp_a, _overlap_t(s, nbp), p, gq, row(nsa_kc_norm_g),
                            128, min(SEL_TOPK, nb))
    o_cw = _win_attn(qn, kwn, vwa, p, o_cmp, gq, row(nsa_kw_norm_g), 256)
    nsa_out = _sel_attn(qn, bias, ksa, vsa, p, o_cw, gq, row(nsa_ks_norm_g),
                        512 if big else 256, 1024 if big else 512)

    h = _out_proj(x, gla_out, nsa_out, w_out.astype(BF16), 512 if big else 256)
    return _ffn(h, row(ffn_norm_g), w_gate.astype(BF16), w_up.astype(BF16), w_down.astype(BF16),
                512 if big else 256, 512)


def kernel(x, attn_norm_g, w_in, gla_conv_w, gla_gate_w2, gla_gate_b, gla_norm_g, nsa_q_norm_g, nsa_kc_norm_g, nsa_ks_norm_g, nsa_kw_norm_g, cmp_k_pos, cmp_k_w1, cmp_k_w2, cmp_v_pos, cmp_v_w1, cmp_v_w2, w_out, ffn_norm_g, w_gate, w_up, w_down):
    assert x.shape[0] == 1 and attn_norm_g.shape[0] == 1
    y = _layer(x[0], attn_norm_g[0], w_in[0], gla_conv_w[0], gla_gate_w2[0], gla_gate_b[0],
               gla_norm_g[0], nsa_q_norm_g[0], nsa_kc_norm_g[0], nsa_ks_norm_g[0], nsa_kw_norm_g[0],
               cmp_k_pos[0], cmp_k_w1[0], cmp_k_w2[0], cmp_v_pos[0], cmp_v_w1[0], cmp_v_w2[0],
               w_out[0], ffn_norm_g[0], w_gate[0], w_up[0], w_down[0])
    return y[None]
```

```python
import functools

import jax
import jax.numpy as jnp
import numpy as np
from jax import lax
from jax.experimental import pallas as pl
from jax.experimental.pallas import tpu as pltpu

D_MODEL = 2048
GLA_HEADS = 4
GLA_DK = 128
GLA_DV = 256
GLA_GATE_RANK = 16
GLA_GATE_TAU = 16.0
GLA_CHUNK = 64
GLA_SUB = 16
GLA_SAFE_LOG_DECAY = 60.0
GLA_HEADS_PER_STEP = 1
SOFTMAX_SHIFT_LIMIT = 40.0
CONV_WIDTH = 4

NSA_HEADS = 8
NSA_KV_HEADS = 2
NSA_GROUP = 4
NSA_HEAD_DIM = 128
CMP_BLOCK = 32
CMP_STRIDE = 16
CMP_HIDDEN = 128
SEL_BLOCK = 64
SEL_TOPK = 16
WINDOW = 512
D_FF = 5632
EPS = 1e-6
NEG = -1e30

GLA_QK = GLA_HEADS * GLA_DK
GLA_VW = GLA_HEADS * GLA_DV
GLA_QKV = 2 * GLA_QK + GLA_VW
NSA_QW = NSA_HEADS * NSA_HEAD_DIM
NSA_KVW = NSA_KV_HEADS * NSA_HEAD_DIM

LANES = 128
VMEM_LIMIT = 56 * 1024 * 1024

C_GQ = 0
C_GK = GLA_QK
C_GV = 2 * GLA_QK
C_GO = GLA_QKV
C_NQ = C_GO + GLA_VW
C_KC = C_NQ + NSA_QW
C_VC = C_KC + NSA_KVW
C_KS = C_VC + NSA_KVW
C_VS = C_KS + NSA_KVW
C_KW = C_VS + NSA_KVW
C_VW = C_KW + NSA_KVW
C_MISC = C_VW + NSA_KVW
PROJ_PAD = C_MISC + 2 * LANES
GATE_LANE = GLA_GATE_RANK

F32 = jnp.float32
BF16 = jnp.bfloat16


def _params(sem):
    return pltpu.CompilerParams(dimension_semantics=sem, vmem_limit_bytes=VMEM_LIMIT)


def _nt_dot(a, b):
    return lax.dot_general(a, b, (((1,), (1,)), ((), ())), preferred_element_type=F32)


def _sigmoid(x):
    return 1.0 / (1.0 + jnp.exp(-x))


def _rms(x, g):
    return x * lax.rsqrt(jnp.mean(x * x, axis=-1, keepdims=True) + EPS) * g


def _norm_matmul_kernel(x_ref, g_ref, w_ref, o_ref, xn_ref):
    @pl.when(pl.program_id(1) == 0)
    def _():
        xn_ref[...] = _rms(x_ref[...], g_ref[...]).astype(BF16)

    o_ref[...] = _nt_dot(xn_ref[...], w_ref[...])


def _norm_matmul(x, g, wt, tm, tn):
    s, d = x.shape
    n = wt.shape[0]
    return pl.pallas_call(
        _norm_matmul_kernel,
        grid=(s // tm, n // tn),
        in_specs=[pl.BlockSpec((tm, d), lambda i, j: (i, 0)),
                  pl.BlockSpec((1, d), lambda i, j: (0, 0)),
                  pl.BlockSpec((tn, d), lambda i, j: (j, 0))],
        out_specs=pl.BlockSpec((tm, tn), lambda i, j: (i, j)),
        out_shape=jax.ShapeDtypeStruct((s, n), F32),
        scratch_shapes=[pltpu.VMEM((tm, d), BF16)],
        compiler_params=_params(("arbitrary", "arbitrary")),
        name="in_proj",
    )(x, g, wt)


def _gla_kernel(q_ref, qp_ref, k_ref, kp_ref, v_ref, vp_ref, lr_ref, go_ref,
                cwq_ref, cwk_ref, cwv_ref, w2_ref, gb_ref, ng_ref, o_ref,
                st_ref, sq_ref, sk_ref, sv_ref):
    t_rows = q_ref.shape[0]
    pad = GLA_SUB
    first = pl.program_id(1) == 0

    @pl.when(first)
    def _():
        st_ref[...] = jnp.zeros_like(st_ref)

    def cols(ref, hh, width):
        return ref.at[:, hh * width:(hh + 1) * width]

    def conv_silu(u_ref, p_ref, w_ref, s_ref):
        w = w_ref[...]
        s_ref[0:8, :] = jnp.where(first, 0.0, p_ref[...])
        s_ref[8:16, :] = u_ref[0:8, :]
        head = u_ref[0:8, :] * w[CONV_WIDTH - 1:CONV_WIDTH]
        body = u_ref[8:, :] * w[CONV_WIDTH - 1:CONV_WIDTH]
        for d in range(1, CONV_WIDTH):
            wd = w[CONV_WIDTH - 1 - d:CONV_WIDTH - d]
            head = head + s_ref[pl.ds(8 - d, 8), :] * wd
            body = body + u_ref[pl.ds(8 - d, t_rows - 8), :] * wd
        acc = jnp.concatenate([head, body], axis=0)
        return acc * _sigmoid(acc)

    row = lax.broadcasted_iota(jnp.int32, (t_rows, GLA_DK), 0)
    lane = lax.broadcasted_iota(jnp.int32, (t_rows, GLA_DK), 1)
    rc = row & (GLA_CHUNK - 1)

    def front(hh):
        dk, dv = GLA_DK, GLA_DV
        q = conv_silu(cols(q_ref, hh, dk), cols(qp_ref, hh, dk), cols(cwq_ref, hh, dk), sq_ref.at[hh])
        k = conv_silu(cols(k_ref, hh, dk), cols(kp_ref, hh, dk), cols(cwk_ref, hh, dk), sk_ref.at[hh])
        v = conv_silu(cols(v_ref, hh, dv), cols(vp_ref, hh, dv), cols(cwv_ref, hh, dv), sv_ref.at[hh])
        z = jnp.dot(lr_ref[:, :GLA_GATE_RANK], w2_ref[:, hh * dk:(hh + 1) * dk], preferred_element_type=F32,
                    precision=lax.Precision.HIGHEST) + gb_ref[:, hh * dk:(hh + 1) * dk]
        b = (jnp.minimum(z, 0.0) - jnp.log(1.0 + jnp.exp(-jnp.abs(z)))) * (1.0 / GLA_GATE_TAU)
        d = 1
        while d < GLA_CHUNK:
            b = b + jnp.where(rc >= d, pltpu.roll(b, d, 0), 0.0)
            d *= 2
        return q * (dk ** -0.5), k, v, b

    heads = [front(hh) for hh in range(GLA_HEADS_PER_STEP)]

    srow = lax.broadcasted_iota(jnp.int32, (GLA_CHUNK, GLA_DK), 0)
    scol = lax.broadcasted_iota(jnp.int32, (GLA_CHUNK, GLA_DK), 1)
    zk = jnp.zeros((LANES - GLA_CHUNK, GLA_DK), BF16)
    zv = jnp.zeros((LANES - GLA_CHUNK, GLA_DV), BF16)
    za = jnp.zeros((GLA_SUB, LANES), F32)
    gain = ng_ref[...]

    def chunk_loop(intra_scores):
        for c in range(t_rows // GLA_CHUNK):
            lo = c * GLA_CHUNK
            for hh, (q, k, v, b) in enumerate(heads):
                bc = b[lo:lo + GLA_CHUNK]
                qc = q[lo:lo + GLA_CHUNK]
                kc = k[lo:lo + GLA_CHUNK]
                vc = v[lo:lo + GLA_CHUNK].astype(BF16)
                st = st_ref[hh]
                b_last = bc[GLA_CHUNK - 1:GLA_CHUNK]
                qd = (qc * jnp.exp(bc)).astype(BF16)
                scores = intra_scores(hh, lo, bc, qc, kc, qd)
                o = _nt_dot(qd, st.astype(BF16)) + jnp.dot(
                    scores.astype(BF16), jnp.concatenate([vc, zv], axis=0), preferred_element_type=F32)
                go = go_ref[lo:lo + GLA_CHUNK, hh * GLA_DV:(hh + 1) * GLA_DV]
                o_ref[lo:lo + GLA_CHUNK, hh * GLA_DV:(hh + 1) * GLA_DV] = _rms(o, gain) * (go * _sigmoid(go))
                kt = (kc * jnp.exp(b_last - bc)).astype(BF16)
                upd = lax.dot_general(vc, kt, (((0,), (0,)), ((), ())), preferred_element_type=F32)
                st_ref[hh] = st * jnp.exp(b_last) + upd

    b_min = heads[0][3]
    for hd in heads[1:]:
        b_min = jnp.minimum(b_min, hd[3])
    small_decay = jnp.min(b_min) >= -GLA_SAFE_LOG_DECAY

    @pl.when(small_decay)
    def _():
        def intra_scores(hh, lo, bc, qc, kc, qd):
            kd = (kc * jnp.exp(-bc)).astype(BF16)
            a = _nt_dot(qd, jnp.concatenate([kd, zk], axis=0))
            return jnp.where(scol <= srow, a, 0.0)

        chunk_loop(intra_scores)

    @pl.when(jnp.logical_not(small_decay))
    def _():
        zero_pad = jnp.zeros((pad, GLA_DK), F32)
        rs = row & (GLA_SUB - 1)
        dl = rc - lane
        bands = []
        for hh, (q, k, v, b) in enumerate(heads):
            sq_ref[hh, 0:pad, :] = zero_pad
            sk_ref[hh, 0:pad, :] = zero_pad
            sq_ref[hh, pad:, :] = b
            sk_ref[hh, pad:, :] = k
            band = jnp.where(dl == 0, jnp.sum(q * k, axis=-1, keepdims=True), 0.0)
            for delta in range(1, GLA_SUB):
                e = jnp.exp(jnp.where(rs >= delta, b - sq_ref[hh, pl.ds(pad - delta, t_rows), :], -jnp.inf))
                sc = jnp.sum(q * sk_ref[hh, pl.ds(pad - delta, t_rows), :] * e, axis=-1, keepdims=True)
                band = jnp.where(dl == delta, sc, band)
            bands.append(band)

        def intra_scores(hh, lo, bc, qc, kc, qd):
            blocks = [za]
            for sub in range(1, GLA_CHUNK // GLA_SUB):
                r0 = sub * GLA_SUB
                ref_b = bc[r0:r0 + 1]
                qq = (qc[r0:r0 + GLA_SUB] * jnp.exp(bc[r0:r0 + GLA_SUB] - ref_b)).astype(BF16)
                kk = (kc * jnp.exp(jnp.where(srow < r0, ref_b - bc, -jnp.inf))).astype(BF16)
                blocks.append(_nt_dot(qq, jnp.concatenate([kk, zk], axis=0)))
            return bands[hh][lo:lo + GLA_CHUNK] + jnp.concatenate(blocks, axis=0)

        chunk_loop(intra_scores)


def _gla(p, conv_w, w2, gb, ng, tile):
    s = p.shape[0]
    t8 = tile // 8
    nh = GLA_HEADS_PER_STEP
    dk, dv = nh * GLA_DK, nh * GLA_DV

    def prev(col):
        return lambda h, i: (jnp.maximum(i * t8 - 1, 0), col(h))

    qcol = lambda h: C_GQ // dk + h
    kcol = lambda h: C_GK // dk + h
    vcol = lambda h: C_GV // dv + h
    return pl.pallas_call(
        _gla_kernel,
        grid=(GLA_HEADS // nh, s // tile),
        in_specs=[
            pl.BlockSpec((tile, dk), lambda h, i: (i, qcol(h))),
            pl.BlockSpec((8, dk), prev(qcol)),
            pl.BlockSpec((tile, dk), lambda h, i: (i, kcol(h))),
            pl.BlockSpec((8, dk), prev(kcol)),
            pl.BlockSpec((tile, dv), lambda h, i: (i, vcol(h))),
            pl.BlockSpec((8, dv), prev(vcol)),
            pl.BlockSpec((tile, LANES), lambda h, i: (i, C_MISC // LANES)),
            pl.BlockSpec((tile, dv), lambda h, i: (i, C_GO // dv + h)),
            pl.BlockSpec((CONV_WIDTH, dk), lambda h, i: (0, qcol(h))),
            pl.BlockSpec((CONV_WIDTH, dk), lambda h, i: (0, kcol(h))),
            pl.BlockSpec((CONV_WIDTH, dv), lambda h, i: (0, vcol(h))),
            pl.BlockSpec((GLA_GATE_RANK, dk), lambda h, i: (0, h)),
            pl.BlockSpec((1, dk), lambda h, i: (0, h)),
            pl.BlockSpec((1, GLA_DV), lambda h, i: (0, 0)),
        ],
        out_specs=pl.BlockSpec((tile, dv), lambda h, i: (i, h)),
        out_shape=jax.ShapeDtypeStruct((s, GLA_VW), F32),
        scratch_shapes=[pltpu.VMEM((nh, GLA_DV, GLA_DK), F32),
                        pltpu.VMEM((nh, tile + GLA_SUB, GLA_DK), F32),
                        pltpu.VMEM((nh, tile + GLA_SUB, GLA_DK), F32),
                        pltpu.VMEM((nh, GLA_SUB, GLA_DV), F32)],
        compiler_params=_params(("arbitrary", "arbitrary")),
        name="gla",
    )(p, p, p, p, p, p, p, p, conv_w, conv_w, conv_w, w2, gb, ng)


def _nsa_prep_kernel(q_ref, ks_ref, vs_ref, kw_ref, vw_ref, qg_ref, ksg_ref, kwg_ref,
                     qn_ref, ksa_ref, vsa_ref, kwn_ref, vwa_ref):
    t_rows = q_ref.shape[0]
    dh = NSA_HEAD_DIM
    scale = dh ** -0.5
    ones = jnp.ones((t_rows, dh), BF16)
    for h in range(NSA_HEADS):
        sl = slice(h * dh, (h + 1) * dh)
        qn_ref[:, sl] = (_rms(q_ref[:, sl], qg_ref[...]) * scale).astype(BF16)
    pos = pl.program_id(0) * t_rows + lax.broadcasted_iota(jnp.int32, (t_rows, LANES), 0)
    lane = lax.broadcasted_iota(jnp.int32, (t_rows, LANES), 1)
    onehot = jnp.where(lane == ((pos // SEL_BLOCK) & (LANES - 1)), 1.0, 0.0).astype(BF16)
    for h in range(NSA_KV_HEADS):
        sl = slice(h * dh, (h + 1) * dh)
        ksa_ref[:, 2 * h * dh:(2 * h + 1) * dh] = _rms(ks_ref[:, sl], ksg_ref[...]).astype(BF16)
        ksa_ref[:, (2 * h + 1) * dh:(2 * h + 2) * dh] = onehot
        kwn_ref[:, sl] = _rms(kw_ref[:, sl], kwg_ref[...]).astype(BF16)
        vsa_ref[:, 2 * h * dh:(2 * h + 1) * dh] = vs_ref[:, sl].astype(BF16)
        vsa_ref[:, (2 * h + 1) * dh:(2 * h + 2) * dh] = ones
        vwa_ref[:, 2 * h * dh:(2 * h + 1) * dh] = vw_ref[:, sl].astype(BF16)
        vwa_ref[:, (2 * h + 1) * dh:(2 * h + 2) * dh] = ones


def _nsa_prep(p, qg, ksg, kwg, tile):
    s = p.shape[0]
    kv = NSA_KVW
    col = lambda c, w: (lambda i: (i, c // w))
    row = lambda w: pl.BlockSpec((tile, w), lambda i: (i, 0))
    gain = pl.BlockSpec((1, NSA_HEAD_DIM), lambda i: (0, 0))
    return pl.pallas_call(
        _nsa_prep_kernel,
        grid=(s // tile,),
        in_specs=[pl.BlockSpec((tile, NSA_QW), col(C_NQ, NSA_QW)),
                  pl.BlockSpec((tile, kv), col(C_KS, kv)),
                  pl.BlockSpec((tile, kv), col(C_VS, kv)),
                  pl.BlockSpec((tile, kv), col(C_KW, kv)),
                  pl.BlockSpec((tile, kv), col(C_VW, kv)),
                  gain, gain, gain],
        out_specs=[row(NSA_QW), row(2 * kv), row(2 * kv), row(kv), row(2 * kv)],
        out_shape=[jax.ShapeDtypeStruct((s, NSA_QW), BF16),
                   jax.ShapeDtypeStruct((s, 2 * kv), BF16),
                   jax.ShapeDtypeStruct((s, 2 * kv), BF16),
                   jax.ShapeDtypeStruct((s, kv), BF16),
                   jax.ShapeDtypeStruct((s, 2 * kv), BF16)],
        compiler_params=_params(("arbitrary",)),
        name="nsa_prep",
    )(p, p, p, p, p, qg, ksg, kwg)


def _compress_kernel(u_ref, w1_ref, pos_ref, w1f_ref, w2_ref, g_ref, o_ref, oa_ref):
    n = o_ref.shape[1]
    half = CMP_BLOCK // CMP_STRIDE
    assert half == 2
    acc = jnp.zeros((n, 2 * CMP_HIDDEN), F32)
    for l in range(CMP_STRIDE):
        x = u_ref[pl.ds(l, n, stride=CMP_STRIDE), :].astype(BF16)
        acc = acc + jnp.dot(x, w1_ref[0, l], preferred_element_type=F32)
    posb = jnp.dot(jnp.broadcast_to(pos_ref[0], (8, pos_ref.shape[2])), w1f_ref[0],
                   preferred_element_type=F32, precision=lax.Precision.HIGHEST)[0:1]
    hid = acc[:, :CMP_HIDDEN] + pltpu.roll(acc[:, CMP_HIDDEN:], n - 1, 0) + posb
    hid = hid * _sigmoid(hid)
    out = jnp.dot(hid.astype(BF16), w2_ref[0], preferred_element_type=F32)
    is_k = pl.program_id(0) < NSA_KV_HEADS
    out = jnp.where(is_k, _rms(out, g_ref[...]), out)
    row = lax.broadcasted_iota(jnp.int32, out.shape, 0)
    out = jnp.where(row < n - 1, out, 0.0).astype(BF16)
    o_ref[0] = out
    oa_ref[0] = jnp.concatenate([out, jnp.ones_like(out)], axis=1)


def _compress(p, w1s, pos, w1f, w2, g):
    s = p.shape[0]
    n = s // CMP_STRIDE
    dh = NSA_HEAD_DIM
    return pl.pallas_call(
        _compress_kernel,
        grid=(2 * NSA_KV_HEADS,),
        in_specs=[pl.BlockSpec((s, dh), lambda j: (0, C_KC // dh + j)),
                  pl.BlockSpec((1, CMP_STRIDE, dh, 2 * CMP_HIDDEN), lambda j: (j // 2, 0, 0, 0)),
                  pl.BlockSpec((1, 1, CMP_BLOCK * dh), lambda j: (j // 2, 0, 0)),
                  pl.BlockSpec((1, CMP_BLOCK * dh, CMP_HIDDEN), lambda j: (j // 2, 0, 0)),
                  pl.BlockSpec((1, CMP_HIDDEN, dh), lambda j: (j // 2, 0, 0)),
                  pl.BlockSpec((1, dh), lambda j: (0, 0))],
        out_specs=[pl.BlockSpec((1, n, dh), lambda j: (j, 0, 0)),
                   pl.BlockSpec((1, n, 2 * dh), lambda j: (j, 0, 0))],
        out_shape=[jax.ShapeDtypeStruct((2 * NSA_KV_HEADS, n, dh), BF16),
                   jax.ShapeDtypeStruct((2 * NSA_KV_HEADS, n, 2 * dh), BF16)],
        compiler_params=_params(("arbitrary",)),
        name="compress",
    )(p, w1s, pos, w1f, w2, g)


def _softmax_shift(gq_ref, gk_ref):
    bound = (NSA_HEAD_DIM ** 0.5) * jnp.max(jnp.abs(gq_ref[...])) * jnp.max(jnp.abs(gk_ref[...]))
    usable = bound <= SOFTMAX_SHIFT_LIMIT
    return usable, jnp.where(usable, bound, 0.0)


def _unstack_heads(o, tq):
    return jnp.concatenate([o[g * tq:(g + 1) * tq] for g in range(NSA_GROUP)], axis=1)


def _branch_gate(misc, branch, tq):
    sg = _sigmoid(misc)
    base = GATE_LANE + branch * NSA_GROUP
    return jnp.concatenate([sg[:, base + g:base + g + 1] for g in range(NSA_GROUP)], axis=0)


def _cmp_attn_kernel(q_ref, kc_ref, va_ref, ovl_ref, misc_ref, gq_ref, gk_ref, o_ref, bias_ref, e_ref,
                     *, n_sel):
    tq = q_ref.shape[0]
    dh = NSA_HEAD_DIM
    nc = kc_ref.shape[1]
    nbp = ovl_ref.shape[0]
    i = pl.program_id(0)
    s0 = i * tq
    col_step = max(LANES, nc // 4)
    n_var = nc // col_step
    tiles_per_var = (nc * CMP_STRIDE // tq) // n_var
    assert tiles_per_var * tq == col_step * CMP_STRIDE

    fixed, shift = _softmax_shift(gq_ref, gk_ref)

    def variant(ncv, nbv, online):
        q = q_ref[...]
        row = lax.broadcasted_iota(jnp.int32, (tq, ncv), 0)
        col = lax.broadcasted_iota(jnp.int32, (tq, ncv), 1)
        maskb = jnp.where(col * CMP_STRIDE + (CMP_BLOCK - 1) <= s0 + row, -shift, NEG)
        has_valid = (s0 + lax.broadcasted_iota(jnp.int32, (tq, 1), 0)) >= CMP_BLOCK - 1
        sg = _sigmoid(misc_ref[...])
        ovl = ovl_ref[0:nbv, 0:ncv]
        imps = []
        for h in range(NSA_KV_HEADS):
            kc = kc_ref[h, 0:ncv, :]
            va = va_ref[h, 0:ncv, :]
            invs = []
            for g in range(NSA_GROUP):
                hd = h * NSA_GROUP + g
                sm = _nt_dot(q[:, hd * dh:(hd + 1) * dh], kc) + maskb
                if online:
                    sm = sm - jnp.max(sm, axis=-1, keepdims=True)
                e = jnp.exp(sm)
                e_ref[hd, :, 0:ncv] = e
                acc = jnp.dot(e.astype(BF16), va, preferred_element_type=F32)
                inv = jnp.where(has_valid, 1.0 / acc[:, dh:], 0.0)
                gl = h * LANES + GATE_LANE + g
                o_ref[:, hd * dh:(hd + 1) * dh] = acc[:, :dh] * inv * sg[:, gl:gl + 1]
                invs.append(inv)
            ps = None
            for g in range(NSA_GROUP):
                pc = e_ref[h * NSA_GROUP + g, :, 0:ncv] * jnp.tile(invs[g], (1, ncv // LANES))
                ps = pc if ps is None else ps + pc
            hi = ps.astype(BF16)
            r1 = ps - hi.astype(F32)
            mid = r1.astype(BF16)
            lo = (r1 - mid.astype(F32)).astype(BF16)
            imps.append(_nt_dot(ovl, hi) + _nt_dot(ovl, mid) + _nt_dot(ovl, lo))

        jb = lax.broadcasted_iota(jnp.int32, (nbv, tq), 0)
        jt = (s0 + lax.broadcasted_iota(jnp.int32, (nbv, tq), 1)) // SEL_BLOCK
        forced = (jb == 0) | (jb == jt) | (jb == jt - 1)
        future = jb > jt
        curs = [jnp.where(future | forced, -jnp.inf, imp) for imp in imps]
        for _ in range(n_sel - 3):
            for h in range(NSA_KV_HEADS):
                mx = jnp.max(curs[h], axis=0, keepdims=True)
                idx = jnp.min(jnp.where(curs[h] == mx, jb, nbv), axis=0, keepdims=True)
                curs[h] = jnp.where(jb == idx, -jnp.inf, curs[h])
        for h in range(NSA_KV_HEADS):
            bias_t = jnp.where((curs[h] == -jnp.inf) & jnp.logical_not(future), 0.0, NEG)
            bias_ref[:, h * nbp:h * nbp + nbv] = bias_t.T.astype(BF16)
            if nbv < nbp:
                bias_ref[:, h * nbp + nbv:(h + 1) * nbp] = jnp.full((tq, nbp - nbv), NEG, BF16)

    for v in range(n_var):
        ncv = col_step * (v + 1)
        nbv = min(nbp, -(-(ncv // (SEL_BLOCK // CMP_STRIDE)) // LANES) * LANES)
        pl.when(fixed & (i // tiles_per_var == v))(functools.partial(variant, ncv, nbv, False))
    pl.when(jnp.logical_not(fixed))(functools.partial(variant, nc, nbp, True))


def _cmp_attn(qn, cmp, cmp_a, ovl_t, p, gq, gk, tq, n_sel):
    s = qn.shape[0]
    nc = cmp.shape[1]
    nbp = ovl_t.shape[0]
    assert n_sel >= 3
    misc_w = NSA_KV_HEADS * LANES
    gain = pl.BlockSpec((1, NSA_HEAD_DIM), lambda i: (0, 0))
    return pl.pallas_call(
        functools.partial(_cmp_attn_kernel, n_sel=n_sel),
        grid=(s // tq,),
        in_specs=[pl.BlockSpec((tq, NSA_QW), lambda i: (i, 0)),
                  pl.BlockSpec((NSA_KV_HEADS, nc, NSA_HEAD_DIM), lambda i: (0, 0, 0)),
                  pl.BlockSpec((NSA_KV_HEADS, nc, 2 * NSA_HEAD_DIM), lambda i: (1, 0, 0)),
                  pl.BlockSpec((nbp, nc), lambda i: (0, 0)),
                  pl.BlockSpec((tq, misc_w), lambda i: (i, C_MISC // misc_w)),
                  gain, gain],
        out_specs=[pl.BlockSpec((tq, NSA_QW), lambda i: (i, 0)),
                   pl.BlockSpec((tq, NSA_KV_HEADS * nbp), lambda i: (i, 0))],
        out_shape=[jax.ShapeDtypeStruct((s, NSA_QW), F32),
                   jax.ShapeDtypeStruct((s, NSA_KV_HEADS * nbp), BF16)],
        scratch_shapes=[pltpu.VMEM((NSA_HEADS, tq, nc), F32)],
        compiler_params=_params(("arbitrary",)),
        name="cmp_attn",
    )(qn, cmp, cmp_a, ovl_t, p, gq, gk)


def _win_attn_kernel(q_ref, k_ref, v_ref, misc_ref, prev_ref, gq_ref, gk_ref, o_ref, *, span):
    tq = q_ref.shape[0]
    dh = NSA_HEAD_DIM
    s0 = pl.program_id(1) * tq
    start = pl.multiple_of(jnp.maximum(s0 - WINDOW, 0), tq)
    fixed, shift = _softmax_shift(gq_ref, gk_ref)
    row = lax.broadcasted_iota(jnp.int32, (tq, span), 0)
    col = lax.broadcasted_iota(jnp.int32, (tq, span), 1)
    dist = (s0 - start) + row - col
    maskb = jnp.where((dist >= 0) & (dist < WINDOW), -shift, NEG)

    def run(online):
        q = q_ref[...]
        k = k_ref[pl.ds(start, span), :]
        v = v_ref[pl.ds(start, span), :]
        sg = _sigmoid(misc_ref[...])
        base = GATE_LANE + 2 * NSA_GROUP
        for g in range(NSA_GROUP):
            sm = _nt_dot(q[:, g * dh:(g + 1) * dh], k) + maskb
            if online:
                sm = sm - jnp.max(sm, axis=-1, keepdims=True)
            acc = jnp.dot(jnp.exp(sm).astype(BF16), v, preferred_element_type=F32)
            o = acc[:, :dh] * (sg[:, base + g:base + g + 1] / acc[:, dh:])
            o_ref[:, g * dh:(g + 1) * dh] = prev_ref[:, g * dh:(g + 1) * dh] + o

    pl.when(fixed)(functools.partial(run, False))
    pl.when(jnp.logical_not(fixed))(functools.partial(run, True))


def _win_attn(qn, kwn, vwa, p, prev, gq, gk, tq):
    s = qn.shape[0]
    gw = NSA_GROUP * NSA_HEAD_DIM
    span = min(s, WINDOW + tq)
    gain = pl.BlockSpec((1, NSA_HEAD_DIM), lambda h, i: (0, 0))
    return pl.pallas_call(
        functools.partial(_win_attn_kernel, span=span),
        grid=(NSA_KV_HEADS, s // tq),
        in_specs=[pl.BlockSpec((tq, gw), lambda h, i: (i, h)),
                  pl.BlockSpec((s, NSA_HEAD_DIM), lambda h, i: (0, h)),
                  pl.BlockSpec((s, 2 * NSA_HEAD_DIM), lambda h, i: (0, h)),
                  pl.BlockSpec((tq, LANES), lambda h, i: (i, C_MISC // LANES + h)),
                  pl.BlockSpec((tq, gw), lambda h, i: (i, h)),
                  gain, gain],
        out_specs=pl.BlockSpec((tq, gw), lambda h, i: (i, h)),
        out_shape=jax.ShapeDtypeStruct((s, NSA_QW), F32),
        compiler_params=_params(("arbitrary", "arbitrary")),
        name="win_attn",
    )(qn, kwn, vwa, p, prev, gq, gk)


def _sel_attn_kernel(q_ref, bias_ref, ka_ref, va_ref, misc_ref, prev_ref, gq_ref, gk_ref, o_ref,
                     lhs_ref, m_ref, acc_ref, sa_ref, sb_ref, *, tk):
    tq = q_ref.shape[0]
    dh = NSA_HEAD_DIM
    n_super = lhs_ref.shape[0]
    tiles_per_super = LANES * SEL_BLOCK // tk
    s0 = pl.program_id(1) * tq
    fixed, shift = _softmax_shift(gq_ref, gk_ref)
    for sup in range(n_super):
        for g in range(NSA_GROUP):
            lhs_ref[sup, g * tq:(g + 1) * tq, 0:dh] = q_ref[:, g * dh:(g + 1) * dh]
            lhs_ref[sup, g * tq:(g + 1) * tq, dh:2 * dh] = (
                bias_ref[:, sup * LANES:(sup + 1) * LANES].astype(F32) - shift).astype(BF16)
    m_ref[...] = jnp.full_like(m_ref, -jnp.inf)
    acc_ref[...] = jnp.zeros_like(acc_ref)

    def scores(kt, sc_ref):
        k0 = pl.multiple_of(kt * tk, tk)
        k = ka_ref[pl.ds(k0, tk), :]
        sup = kt // tiles_per_super
        for g in range(NSA_GROUP):
            rows = slice(g * tq, (g + 1) * tq)
            sc_ref[rows, :] = _nt_dot(lhs_ref[sup, rows, :], k)

    def accumulate(online, kt, sc_ref, diagonal):
        k0 = pl.multiple_of(kt * tk, tk)
        v = va_ref[pl.ds(k0, tk), :]
        for g in range(NSA_GROUP):
            rows = slice(g * tq, (g + 1) * tq)
            sc = sc_ref[rows, :]
            if diagonal:
                row = lax.broadcasted_iota(jnp.int32, sc.shape, 0)
                col = lax.broadcasted_iota(jnp.int32, sc.shape, 1)
                sc = jnp.where(k0 + col <= s0 + row, sc, NEG)
            if online:
                m_old = m_ref[rows, :]
                m_new = jnp.maximum(m_old, jnp.max(sc, axis=-1, keepdims=True))
                alpha = jnp.exp(m_old - m_new)
                pr = jnp.exp(sc - jnp.tile(m_new, (1, tk // LANES)))
                acc_ref[rows, :] = (jnp.tile(alpha, (1, 2)) * acc_ref[rows, :]
                                    + jnp.dot(pr.astype(BF16), v, preferred_element_type=F32))
                m_ref[rows, :] = m_new
            else:
                acc_ref[rows, :] += jnp.dot(jnp.exp(sc).astype(BF16), v, preferred_element_type=F32)

    n_full = s0 // tk

    def run(online):
        acc_fn = functools.partial(accumulate, online)

        def body(j, carry):
            scores(2 * j + 1, sb_ref)
            acc_fn(2 * j, sa_ref, False)
            scores(2 * j + 2, sa_ref)
            acc_fn(2 * j + 1, sb_ref, False)
            return carry

        scores(0, sa_ref)
        lax.fori_loop(0, n_full // 2, body, 0)

        @pl.when(n_full % 2 == 0)
        def _():
            acc_fn(n_full, sa_ref, True)

        @pl.when(n_full % 2 == 1)
        def _():
            scores(n_full, sb_ref)
            acc_fn(n_full - 1, sa_ref, False)
            acc_fn(n_full, sb_ref, True)

    pl.when(fixed)(functools.partial(run, False))
    pl.when(jnp.logical_not(fixed))(functools.partial(run, True))

    acc = acc_ref[...]
    o = acc[:, :dh] * (_branch_gate(misc_ref[...], 1, tq) / acc[:, dh:])
    o_ref[...] = prev_ref[...] + _unstack_heads(o, tq)


def _sel_attn(qn, bias, ksa, vsa, p, prev, gq, gk, tq, tk):
    s = qn.shape[0]
    dh = NSA_HEAD_DIM
    gw = NSA_GROUP * dh
    nbp = bias.shape[1] // NSA_KV_HEADS
    n_super = nbp // LANES
    assert tk % tq == 0 and (LANES * SEL_BLOCK) % tk == 0 and s % tk == 0
    resident = lambda: pl.BlockSpec((s, 2 * dh), lambda h, i: (0, h), pipeline_mode=pl.Buffered(1))
    gain = pl.BlockSpec((1, dh), lambda h, i: (0, 0))
    return pl.pallas_call(
        functools.partial(_sel_attn_kernel, tk=tk),
        grid=(NSA_KV_HEADS, s // tq),
        in_specs=[pl.BlockSpec((tq, gw), lambda h, i: (i, h)),
                  pl.BlockSpec((tq, nbp), lambda h, i: (i, h)),
                  resident(), resident(),
                  pl.BlockSpec((tq, LANES), lambda h, i: (i, C_MISC // LANES + h)),
                  pl.BlockSpec((tq, gw), lambda h, i: (i, h)),
                  gain, gain],
        out_specs=pl.BlockSpec((tq, gw), lambda h, i: (i, h)),
        out_shape=jax.ShapeDtypeStruct((s, NSA_QW), F32),
        scratch_shapes=[pltpu.VMEM((n_super, NSA_GROUP * tq, 2 * dh), BF16),
                        pltpu.VMEM((NSA_GROUP * tq, LANES), F32),
                        pltpu.VMEM((NSA_GROUP * tq, 2 * dh), F32),
                        pltpu.VMEM((NSA_GROUP * tq, tk), F32),
                        pltpu.VMEM((NSA_GROUP * tq, tk), F32)],
        compiler_params=_params(("arbitrary", "arbitrary")),
        name="sel_attn",
    )(qn, bias, ksa, vsa, p, prev, gq, gk)


def _out_proj_kernel(x_ref, a_ref, b_ref, wa_ref, wb_ref, o_ref):
    o_ref[...] = (x_ref[...]
                  + jnp.dot(a_ref[...].astype(BF16), wa_ref[...], preferred_element_type=F32)
                  + jnp.dot(b_ref[...].astype(BF16), wb_ref[...], preferred_element_type=F32))


def _out_proj(x, a, b, w, tm):
    s, d = x.shape
    ka, kb = a.shape[1], b.shape[1]
    assert ka == kb
    return pl.pallas_call(
        _out_proj_kernel,
        grid=(s // tm,),
        in_specs=[pl.BlockSpec((tm, d), lambda i: (i, 0)),
                  pl.BlockSpec((tm, ka), lambda i: (i, 0)),
                  pl.BlockSpec((tm, kb), lambda i: (i, 0)),
                  pl.BlockSpec((ka, d), lambda i: (0, 0)),
                  pl.BlockSpec((kb, d), lambda i: (1, 0))],
        out_specs=pl.BlockSpec((tm, d), lambda i: (i, 0)),
        out_shape=jax.ShapeDtypeStruct((s, d), F32),
        compiler_params=_params(("arbitrary",)),
        name="out_proj",
    )(x, a, b, w, w)


def _ffn_kernel(h_ref, g_ref, wg_ref, wu_ref, wd_ref, o_ref, hn_ref):
    @pl.when(pl.program_id(1) == 0)
    def _():
        h = h_ref[...]
        hn_ref[...] = _rms(h, g_ref[...]).astype(BF16)
        o_ref[...] = h

    hn = hn_ref[...]
    a = jnp.dot(hn, wg_ref[...].astype(BF16), preferred_element_type=F32)
    u = jnp.dot(hn, wu_ref[...].astype(BF16), preferred_element_type=F32)
    z = (a * _sigmoid(a) * u).astype(BF16)
    o_ref[...] += jnp.dot(z, wd_ref[...].astype(BF16), preferred_element_type=F32)


def _ffn(h, g, wg, wu, wd, tm, tf):
    s, d = h.shape
    f = wg.shape[1]
    return pl.pallas_call(
        _ffn_kernel,
        grid=(s // tm, f // tf),
        in_specs=[pl.BlockSpec((tm, d), lambda i, j: (i, 0)),
                  pl.BlockSpec((1, d), lambda i, j: (0, 0)),
                  pl.BlockSpec((d, tf), lambda i, j: (0, j)),
                  pl.BlockSpec((d, tf), lambda i, j: (0, j)),
                  pl.BlockSpec((tf, d), lambda i, j: (j, 0))],
        out_specs=pl.BlockSpec((tm, d), lambda i, j: (i, 0)),
        out_shape=jax.ShapeDtypeStruct((s, d), F32),
        scratch_shapes=[pltpu.VMEM((tm, d), BF16)],
        compiler_params=_params(("arbitrary", "arbitrary")),
        name="ffn",
    )(h, g, wg, wu, wd)


_W_IN_LR = GLA_QKV
_W_IN_GO = _W_IN_LR + GLA_GATE_RANK
_W_IN_NG = _W_IN_GO + GLA_VW + NSA_QW + 6 * NSA_KVW


def _regroup_kernel(w_ref, misc_ref, o_ref):
    o_ref[0:GLA_QKV, :] = w_ref[0:GLA_QKV, :].astype(BF16)
    o_ref[GLA_QKV:C_MISC, :] = w_ref[_W_IN_GO:_W_IN_NG, :].astype(BF16)
    o_ref[C_MISC:PROJ_PAD, :] = misc_ref[...]


def _regroup_w_in(w_in, tile=256):
    d = w_in.shape[0]
    wt = w_in.T
    lr = wt[_W_IN_LR:_W_IN_GO]
    ng = wt[_W_IN_NG:_W_IN_NG + 3 * NSA_HEADS].reshape(NSA_KV_HEADS, NSA_GROUP, 3, d)
    ng = ng.transpose(0, 2, 1, 3).reshape(NSA_KV_HEADS, 3 * NSA_GROUP, d)
    pad = lambda n: jnp.zeros((n, d), w_in.dtype)
    tail = pad(LANES - GLA_GATE_RANK - 3 * NSA_GROUP)
    misc = jnp.concatenate([lr, ng[0], tail, pad(GLA_GATE_RANK), ng[1], tail], axis=0).astype(BF16)
    return pl.pallas_call(
        _regroup_kernel,
        grid=(d // tile,),
        in_specs=[pl.BlockSpec((wt.shape[0], tile), lambda i: (0, i)),
                  pl.BlockSpec((2 * LANES, tile), lambda i: (0, i))],
        out_specs=pl.BlockSpec((PROJ_PAD, tile), lambda i: (0, i)),
        out_shape=jax.ShapeDtypeStruct((PROJ_PAD, d), BF16),
        compiler_params=_params(("arbitrary",)),
        name="regroup_w_in",
    )(wt, misc)


def _overlap_t(s, nbp):
    n = s // CMP_STRIDE
    c0 = np.arange(n)[None, :] * CMP_STRIDE
    s0 = np.arange(nbp)[:, None] * SEL_BLOCK
    ov = np.clip(np.minimum(c0 + CMP_BLOCK, s0 + SEL_BLOCK) - np.maximum(c0, s0), 0, None) / CMP_STRIDE
    ov[:, n - 1] = 0.0
    ov[s // SEL_BLOCK:] = 0.0
    return jnp.asarray(ov, BF16)


def _layer(x, attn_norm_g, w_in, gla_conv_w, gla_gate_w2, gla_gate_b, gla_norm_g,
           nsa_q_norm_g, nsa_kc_norm_g, nsa_ks_norm_g, nsa_kw_norm_g,
           cmp_k_pos, cmp_k_w1, cmp_k_w2, cmp_v_pos, cmp_v_w1, cmp_v_w2,
           w_out, ffn_norm_g, w_gate, w_up, w_down):
    s = x.shape[0]
    dh = NSA_HEAD_DIM
    row = lambda v: v.reshape(1, -1)
    big = s >= 4096
    tm = 1024 if big else 256

    p = _norm_matmul(x, row(attn_norm_g), _regroup_w_in(w_in), 512 if big else 256,
                     PROJ_PAD // 2 if big else LANES)

    gla_out = _gla(p, gla_conv_w, gla_gate_w2, row(gla_gate_b), row(gla_norm_g), 512)

    qn, ksa, vsa, kwn, vwa = _nsa_prep(p, row(nsa_q_norm_g), row(nsa_ks_norm_g), row(nsa_kw_norm_g), 512)

    w1 = jnp.stack([cmp_k_w1, cmp_v_w1])
    w1s = (w1.reshape(2, 2, CMP_STRIDE, dh, CMP_HIDDEN).transpose(0, 2, 3, 1, 4)
           .reshape(2, CMP_STRIDE, dh, 2 * CMP_HIDDEN).astype(BF16))
    pos = jnp.stack([cmp_k_pos, cmp_v_pos]).reshape(2, 1, CMP_BLOCK * dh)
    w2 = jnp.stack([cmp_k_w2, cmp_v_w2]).astype(BF16)
    cmp, cmp_a = _compress(p, w1s, pos, w1, w2, row(nsa_kc_norm_g))

    nb = s // SEL_BLOCK
    nbp = -(-nb // LANES) * LANES
    gq = row(nsa_q_norm_g)
    o_cmp, bias = _cmp_attn(qn, cmp, cmp_a, _overlap_t(s, nbp), p, gq, row(nsa_kc_norm_g),
                            128, min(SEL_TOPK, nb))
    o_cw = _win_attn(qn, kwn, vwa, p, o_cmp, gq, row(nsa_kw_norm_g), 256)
    nsa_out = _sel_attn(qn, bias, ksa, vsa, p, o_cw, gq, row(nsa_ks_norm_g),
                        512 if big else 256, 1024 if big else 512)

    h = _out_proj(x, gla_out, nsa_out, w_out.astype(BF16), 512 if big else 256)
    return _ffn(h, row(ffn_norm_g), w_gate, w_up, w_down, 1024 if big else 256, 256)


def kernel(x, attn_norm_g, w_in, gla_conv_w, gla_gate_w2, gla_gate_b, gla_norm_g, nsa_q_norm_g, nsa_kc_norm_g, nsa_ks_norm_g, nsa_kw_norm_g, cmp_k_pos, cmp_k_w1, cmp_k_w2, cmp_v_pos, cmp_v_w1, cmp_v_w2, w_out, ffn_norm_g, w_gate, w_up, w_down):
    assert x.shape[0] == 1 and attn_norm_g.shape[0] == 1
    y = _layer(x[0], attn_norm_g[0], w_in[0], gla_conv_w[0], gla_gate_w2[0], gla_gate_b[0],
               gla_norm_g[0], nsa_q_norm_g[0], nsa_kc_norm_g[0], nsa_ks_norm_g[0], nsa_kw_norm_g[0],
               cmp_k_pos[0], cmp_k_w1[0], cmp_k_w2[0], cmp_v_pos[0], cmp_v_w1[0], cmp_v_w2[0],
               w_out[0], ffn_norm_g[0], w_gate[0], w_up[0], w_down[0])
    return y[None]
```

```python
import functools

import jax
import jax.numpy as jnp
import numpy as np
from jax import lax
from jax.experimental import pallas as pl
from jax.experimental.pallas import tpu as pltpu

D_MODEL = 2048
GLA_HEADS = 4
GLA_DK = 128
GLA_DV = 256
GLA_GATE_RANK = 16
GLA_GATE_TAU = 16.0
GLA_CHUNK = 64
GLA_SUB = 16
GLA_SAFE_LOG_DECAY = 60.0
GLA_HEADS_PER_STEP = 1
SOFTMAX_SHIFT_LIMIT = 40.0
CONV_WIDTH = 4

NSA_HEADS = 8
NSA_KV_HEADS = 2
NSA_GROUP = 4
NSA_HEAD_DIM = 128
CMP_BLOCK = 32
CMP_STRIDE = 16
CMP_HIDDEN = 128
SEL_BLOCK = 64
SEL_TOPK = 16
WINDOW = 512
D_FF = 5632
EPS = 1e-6
NEG = -1e30

GLA_QK = GLA_HEADS * GLA_DK
GLA_VW = GLA_HEADS * GLA_DV
GLA_QKV = 2 * GLA_QK + GLA_VW
NSA_QW = NSA_HEADS * NSA_HEAD_DIM
NSA_KVW = NSA_KV_HEADS * NSA_HEAD_DIM

LANES = 128
VMEM_LIMIT = 56 * 1024 * 1024

C_GQ = 0
C_GK = GLA_QK
C_GV = 2 * GLA_QK
C_GO = GLA_QKV
C_NQ = C_GO + GLA_VW
C_KC = C_NQ + NSA_QW
C_VC = C_KC + NSA_KVW
C_KS = C_VC + NSA_KVW
C_VS = C_KS + NSA_KVW
C_KW = C_VS + NSA_KVW
C_VW = C_KW + NSA_KVW
C_MISC = C_VW + NSA_KVW
PROJ_PAD = C_MISC + 2 * LANES
GATE_LANE = GLA_GATE_RANK

F32 = jnp.float32
BF16 = jnp.bfloat16


def _params(sem):
    return pltpu.CompilerParams(dimension_semantics=sem, vmem_limit_bytes=VMEM_LIMIT)


def _nt_dot(a, b):
    return lax.dot_general(a, b, (((1,), (1,)), ((), ())), preferred_element_type=F32)


def _sigmoid(x):
    return 1.0 / (1.0 + jnp.exp(-x))


def _rms(x, g):
    return x * lax.rsqrt(jnp.mean(x * x, axis=-1, keepdims=True) + EPS) * g


def _norm_matmul_kernel(x_ref, g_ref, w_ref, o_ref, xn_ref):
    @pl.when(pl.program_id(1) == 0)
    def _():
        xn_ref[...] = _rms(x_ref[...], g_ref[...]).astype(BF16)

    o_ref[...] = _nt_dot(xn_ref[...], w_ref[...])


def _norm_matmul(x, g, wt, tm, tn):
    s, d = x.shape
    n = wt.shape[0]
    return pl.pallas_call(
        _norm_matmul_kernel,
        grid=(s // tm, n // tn),
        in_specs=[pl.BlockSpec((tm, d), lambda i, j: (i, 0)),
                  pl.BlockSpec((1, d), lambda i, j: (0, 0)),
                  pl.BlockSpec((tn, d), lambda i, j: (j, 0))],
        out_specs=pl.BlockSpec((tm, tn), lambda i, j: (i, j)),
        out_shape=jax.ShapeDtypeStruct((s, n), F32),
        scratch_shapes=[pltpu.VMEM((tm, d), BF16)],
        compiler_params=_params(("arbitrary", "arbitrary")),
        name="in_proj",
    )(x, g, wt)


def _gla_kernel(q_ref, qp_ref, k_ref, kp_ref, v_ref, vp_ref, lr_ref, go_ref,
                cwq_ref, cwk_ref, cwv_ref, w2_ref, gb_ref, ng_ref, o_ref,
                st_ref, sq_ref, sk_ref, sv_ref):
    t_rows = q_ref.shape[0]
    pad = GLA_SUB
    first = pl.program_id(1) == 0

    @pl.when(first)
    def _():
        st_ref[...] = jnp.zeros_like(st_ref)

    def cols(ref, hh, width):
        return ref.at[:, hh * width:(hh + 1) * width]

    def conv_silu(u_ref, p_ref, w_ref, s_ref):
        w = w_ref[...]
        s_ref[0:8, :] = jnp.where(first, 0.0, p_ref[...])
        s_ref[8:16, :] = u_ref[0:8, :]
        head = u_ref[0:8, :] * w[CONV_WIDTH - 1:CONV_WIDTH]
        body = u_ref[8:, :] * w[CONV_WIDTH - 1:CONV_WIDTH]
        for d in range(1, CONV_WIDTH):
            wd = w[CONV_WIDTH - 1 - d:CONV_WIDTH - d]
            head = head + s_ref[pl.ds(8 - d, 8), :] * wd
            body = body + u_ref[pl.ds(8 - d, t_rows - 8), :] * wd
        acc = jnp.concatenate([head, body], axis=0)
        return acc * _sigmoid(acc)

    row = lax.broadcasted_iota(jnp.int32, (t_rows, GLA_DK), 0)
    lane = lax.broadcasted_iota(jnp.int32, (t_rows, GLA_DK), 1)
    rc = row & (GLA_CHUNK - 1)

    def front(hh):
        dk, dv = GLA_DK, GLA_DV
        q = conv_silu(cols(q_ref, hh, dk), cols(qp_ref, hh, dk), cols(cwq_ref, hh, dk), sq_ref.at[hh])
        k = conv_silu(cols(k_ref, hh, dk), cols(kp_ref, hh, dk), cols(cwk_ref, hh, dk), sk_ref.at[hh])
        v = conv_silu(cols(v_ref, hh, dv), cols(vp_ref, hh, dv), cols(cwv_ref, hh, dv), sv_ref.at[hh])
        z = jnp.dot(lr_ref[:, :GLA_GATE_RANK], w2_ref[:, hh * dk:(hh + 1) * dk], preferred_element_type=F32,
                    precision=lax.Precision.HIGHEST) + gb_ref[:, hh * dk:(hh + 1) * dk]
        b = (jnp.minimum(z, 0.0) - jnp.log(1.0 + jnp.exp(-jnp.abs(z)))) * (1.0 / GLA_GATE_TAU)
        d = 1
        while d < GLA_CHUNK:
            b = b + jnp.where(rc >= d, pltpu.roll(b, d, 0), 0.0)
            d *= 2
        return q * (dk ** -0.5), k, v, b

    heads = [front(hh) for hh in range(GLA_HEADS_PER_STEP)]

    srow = lax.broadcasted_iota(jnp.int32, (GLA_CHUNK, GLA_DK), 0)
    scol = lax.broadcasted_iota(jnp.int32, (GLA_CHUNK, GLA_DK), 1)
    zk = jnp.zeros((LANES - GLA_CHUNK, GLA_DK), BF16)
    zv = jnp.zeros((LANES - GLA_CHUNK, GLA_DV), BF16)
    za = jnp.zeros((GLA_SUB, LANES), F32)
    gain = ng_ref[...]

    def chunk_loop(intra_scores):
        for c in range(t_rows // GLA_CHUNK):
            lo = c * GLA_CHUNK
            for hh, (q, k, v, b) in enumerate(heads):
                bc = b[lo:lo + GLA_CHUNK]
                qc = q[lo:lo + GLA_CHUNK]
                kc = k[lo:lo + GLA_CHUNK]
                vc = v[lo:lo + GLA_CHUNK].astype(BF16)
                st = st_ref[hh]
                b_last = bc[GLA_CHUNK - 1:GLA_CHUNK]
                qd = (qc * jnp.exp(bc)).astype(BF16)
                scores = intra_scores(hh, lo, bc, qc, kc, qd)
                o = _nt_dot(qd, st.astype(BF16)) + jnp.dot(
                    scores.astype(BF16), jnp.concatenate([vc, zv], axis=0), preferred_element_type=F32)
                go = go_ref[lo:lo + GLA_CHUNK, hh * GLA_DV:(hh + 1) * GLA_DV]
                o_ref[lo:lo + GLA_CHUNK, hh * GLA_DV:(hh + 1) * GLA_DV] = _rms(o, gain) * (go * _sigmoid(go))
                kt = (kc * jnp.exp(b_last - bc)).astype(BF16)
                upd = lax.dot_general(vc, kt, (((0,), (0,)), ((), ())), preferred_element_type=F32)
                st_ref[hh] = st * jnp.exp(b_last) + upd

    b_min = heads[0][3]
    for hd in heads[1:]:
        b_min = jnp.minimum(b_min, hd[3])
    small_decay = jnp.min(b_min) >= -GLA_SAFE_LOG_DECAY

    @pl.when(small_decay)
    def _():
        def intra_scores(hh, lo, bc, qc, kc, qd):
            kd = (kc * jnp.exp(-bc)).astype(BF16)
            a = _nt_dot(qd, jnp.concatenate([kd, zk], axis=0))
            return jnp.where(scol <= srow, a, 0.0)

        chunk_loop(intra_scores)

    @pl.when(jnp.logical_not(small_decay))
    def _():
        zero_pad = jnp.zeros((pad, GLA_DK), F32)
        rs = row & (GLA_SUB - 1)
        dl = rc - lane
        bands = []
        for hh, (q, k, v, b) in enumerate(heads):
            sq_ref[hh, 0:pad, :] = zero_pad
            sk_ref[hh, 0:pad, :] = zero_pad
            sq_ref[hh, pad:, :] = b
            sk_ref[hh, pad:, :] = k
            band = jnp.where(dl == 0, jnp.sum(q * k, axis=-1, keepdims=True), 0.0)
            for delta in range(1, GLA_SUB):
                e = jnp.exp(jnp.where(rs >= delta, b - sq_ref[hh, pl.ds(pad - delta, t_rows), :], -jnp.inf))
                sc = jnp.sum(q * sk_ref[hh, pl.ds(pad - delta, t_rows), :] * e, axis=-1, keepdims=True)
                band = jnp.where(dl == delta, sc, band)
            bands.append(band)

        def intra_scores(hh, lo, bc, qc, kc, qd):
            blocks = [za]
            for sub in range(1, GLA_CHUNK // GLA_SUB):
                r0 = sub * GLA_SUB
                ref_b = bc[r0:r0 + 1]
                qq = (qc[r0:r0 + GLA_SUB] * jnp.exp(bc[r0:r0 + GLA_SUB] - ref_b)).astype(BF16)
                kk = (kc * jnp.exp(jnp.where(srow < r0, ref_b - bc, -jnp.inf))).astype(BF16)
                blocks.append(_nt_dot(qq, jnp.concatenate([kk, zk], axis=0)))
            return bands[hh][lo:lo + GLA_CHUNK] + jnp.concatenate(blocks, axis=0)

        chunk_loop(intra_scores)


def _gla(p, conv_w, w2, gb, ng, tile):
    s = p.shape[0]
    t8 = tile // 8
    nh = GLA_HEADS_PER_STEP
    dk, dv = nh * GLA_DK, nh * GLA_DV

    def prev(col):
        return lambda h, i: (jnp.maximum(i * t8 - 1, 0), col(h))

    qcol = lambda h: C_GQ // dk + h
    kcol = lambda h: C_GK // dk + h
    vcol = lambda h: C_GV // dv + h
    return pl.pallas_call(
        _gla_kernel,
        grid=(GLA_HEADS // nh, s // tile),
        in_specs=[
            pl.BlockSpec((tile, dk), lambda h, i: (i, qcol(h))),
            pl.BlockSpec((8, dk), prev(qcol)),
            pl.BlockSpec((tile, dk), lambda h, i: (i, kcol(h))),
            pl.BlockSpec((8, dk), prev(kcol)),
            pl.BlockSpec((tile, dv), lambda h, i: (i, vcol(h))),
            pl.BlockSpec((8, dv), prev(vcol)),
            pl.BlockSpec((tile, LANES), lambda h, i: (i, C_MISC // LANES)),
            pl.BlockSpec((tile, dv), lambda h, i: (i, C_GO // dv + h)),
            pl.BlockSpec((CONV_WIDTH, dk), lambda h, i: (0, qcol(h))),
            pl.BlockSpec((CONV_WIDTH, dk), lambda h, i: (0, kcol(h))),
            pl.BlockSpec((CONV_WIDTH, dv), lambda h, i: (0, vcol(h))),
            pl.BlockSpec((GLA_GATE_RANK, dk), lambda h, i: (0, h)),
            pl.BlockSpec((1, dk), lambda h, i: (0, h)),
            pl.BlockSpec((1, GLA_DV), lambda h, i: (0, 0)),
        ],
        out_specs=pl.BlockSpec((tile, dv), lambda h, i: (i, h)),
        out_shape=jax.ShapeDtypeStruct((s, GLA_VW), F32),
        scratch_shapes=[pltpu.VMEM((nh, GLA_DV, GLA_DK), F32),
                        pltpu.VMEM((nh, tile + GLA_SUB, GLA_DK), F32),
                        pltpu.VMEM((nh, tile + GLA_SUB, GLA_DK), F32),
                        pltpu.VMEM((nh, GLA_SUB, GLA_DV), F32)],
        compiler_params=_params(("arbitrary", "arbitrary")),
        name="gla",
    )(p, p, p, p, p, p, p, p, conv_w, conv_w, conv_w, w2, gb, ng)


def _nsa_prep_kernel(q_ref, ks_ref, vs_ref, kw_ref, vw_ref, qg_ref, ksg_ref, kwg_ref,
                     qn_ref, ksa_ref, vsa_ref, kwn_ref, vwa_ref):
    t_rows = q_ref.shape[0]
    dh = NSA_HEAD_DIM
    scale = dh ** -0.5
    ones = jnp.ones((t_rows, dh), BF16)
    for h in range(NSA_HEADS):
        sl = slice(h * dh, (h + 1) * dh)
        qn_ref[:, sl] = (_rms(q_ref[:, sl], qg_ref[...]) * scale).astype(BF16)
    pos = pl.program_id(0) * t_rows + lax.broadcasted_iota(jnp.int32, (t_rows, LANES), 0)
    lane = lax.broadcasted_iota(jnp.int32, (t_rows, LANES), 1)
    onehot = jnp.where(lane == ((pos // SEL_BLOCK) & (LANES - 1)), 1.0, 0.0).astype(BF16)
    for h in range(NSA_KV_HEADS):
        sl = slice(h * dh, (h + 1) * dh)
        ksa_ref[:, 2 * h * dh:(2 * h + 1) * dh] = _rms(ks_ref[:, sl], ksg_ref[...]).astype(BF16)
        ksa_ref[:, (2 * h + 1) * dh:(2 * h + 2) * dh] = onehot
        kwn_ref[:, sl] = _rms(kw_ref[:, sl], kwg_ref[...]).astype(BF16)
        vsa_ref[:, 2 * h * dh:(2 * h + 1) * dh] = vs_ref[:, sl].astype(BF16)
        vsa_ref[:, (2 * h + 1) * dh:(2 * h + 2) * dh] = ones
        vwa_ref[:, 2 * h * dh:(2 * h + 1) * dh] = vw_ref[:, sl].astype(BF16)
        vwa_ref[:, (2 * h + 1) * dh:(2 * h + 2) * dh] = ones


def _nsa_prep(p, qg, ksg, kwg, tile):
    s = p.shape[0]
    kv = NSA_KVW
    col = lambda c, w: (lambda i: (i, c // w))
    row = lambda w: pl.BlockSpec((tile, w), lambda i: (i, 0))
    gain = pl.BlockSpec((1, NSA_HEAD_DIM), lambda i: (0, 0))
    return pl.pallas_call(
        _nsa_prep_kernel,
        grid=(s // tile,),
        in_specs=[pl.BlockSpec((tile, NSA_QW), col(C_NQ, NSA_QW)),
                  pl.BlockSpec((tile, kv), col(C_KS, kv)),
                  pl.BlockSpec((tile, kv), col(C_VS, kv)),
                  pl.BlockSpec((tile, kv), col(C_KW, kv)),
                  pl.BlockSpec((tile, kv), col(C_VW, kv)),
                  gain, gain, gain],
        out_specs=[row(NSA_QW), row(2 * kv), row(2 * kv), row(kv), row(2 * kv)],
        out_shape=[jax.ShapeDtypeStruct((s, NSA_QW), BF16),
                   jax.ShapeDtypeStruct((s, 2 * kv), BF16),
                   jax.ShapeDtypeStruct((s, 2 * kv), BF16),
                   jax.ShapeDtypeStruct((s, kv), BF16),
                   jax.ShapeDtypeStruct((s, 2 * kv), BF16)],
        compiler_params=_params(("arbitrary",)),
        name="nsa_prep",
    )(p, p, p, p, p, qg, ksg, kwg)


def _compress_kernel(u_ref, w1_ref, pos_ref, w1f_ref, w2_ref, g_ref, o_ref, oa_ref):
    n = o_ref.shape[1]
    half = CMP_BLOCK // CMP_STRIDE
    assert half == 2
    acc = jnp.zeros((n, 2 * CMP_HIDDEN), F32)
    for l in range(CMP_STRIDE):
        x = u_ref[pl.ds(l, n, stride=CMP_STRIDE), :].astype(BF16)
        acc = acc + jnp.dot(x, w1_ref[0, l], preferred_element_type=F32)
    posb = jnp.dot(jnp.broadcast_to(pos_ref[0], (8, pos_ref.shape[2])), w1f_ref[0],
                   preferred_element_type=F32, precision=lax.Precision.HIGHEST)[0:1]
    hid = acc[:, :CMP_HIDDEN] + pltpu.roll(acc[:, CMP_HIDDEN:], n - 1, 0) + posb
    hid = hid * _sigmoid(hid)
    out = jnp.dot(hid.astype(BF16), w2_ref[0], preferred_element_type=F32)
    is_k = pl.program_id(0) < NSA_KV_HEADS
    out = jnp.where(is_k, _rms(out, g_ref[...]), out)
    row = lax.broadcasted_iota(jnp.int32, out.shape, 0)
    out = jnp.where(row < n - 1, out, 0.0).astype(BF16)
    o_ref[0] = out
    oa_ref[0] = jnp.concatenate([out, jnp.ones_like(out)], axis=1)


def _compress(p, w1s, pos, w1f, w2, g):
    s = p.shape[0]
    n = s // CMP_STRIDE
    dh = NSA_HEAD_DIM
    return pl.pallas_call(
        _compress_kernel,
        grid=(2 * NSA_KV_HEADS,),
        in_specs=[pl.BlockSpec((s, dh), lambda j: (0, C_KC // dh + j)),
                  pl.BlockSpec((1, CMP_STRIDE, dh, 2 * CMP_HIDDEN), lambda j: (j // 2, 0, 0, 0)),
                  pl.BlockSpec((1, 1, CMP_BLOCK * dh), lambda j: (j // 2, 0, 0)),
                  pl.BlockSpec((1, CMP_BLOCK * dh, CMP_HIDDEN), lambda j: (j // 2, 0, 0)),
                  pl.BlockSpec((1, CMP_HIDDEN, dh), lambda j: (j // 2, 0, 0)),
                  pl.BlockSpec((1, dh), lambda j: (0, 0))],
        out_specs=[pl.BlockSpec((1, n, dh), lambda j: (j, 0, 0)),
                   pl.BlockSpec((1, n, 2 * dh), lambda j: (j, 0, 0))],
        out_shape=[jax.ShapeDtypeStruct((2 * NSA_KV_HEADS, n, dh), BF16),
                   jax.ShapeDtypeStruct((2 * NSA_KV_HEADS, n, 2 * dh), BF16)],
        compiler_params=_params(("arbitrary",)),
        name="compress",
    )(p, w1s, pos, w1f, w2, g)


def _softmax_shift(gq_ref, gk_ref):
    bound = (NSA_HEAD_DIM ** 0.5) * jnp.max(jnp.abs(gq_ref[...])) * jnp.max(jnp.abs(gk_ref[...]))
    usable = bound <= SOFTMAX_SHIFT_LIMIT
    return usable, jnp.where(usable, bound, 0.0)


def _unstack_heads(o, tq):
    return jnp.concatenate([o[g * tq:(g + 1) * tq] for g in range(NSA_GROUP)], axis=1)


def _branch_gate(misc, branch, tq):
    sg = _sigmoid(misc)
    base = GATE_LANE + branch * NSA_GROUP
    return jnp.concatenate([sg[:, base + g:base + g + 1] for g in range(NSA_GROUP)], axis=0)


def _cmp_attn_kernel(q_ref, kc_ref, va_ref, ovl_ref, misc_ref, gq_ref, gk_ref, o_ref, bias_ref, e_ref,
                     *, n_sel):
    tq = q_ref.shape[0]
    dh = NSA_HEAD_DIM
    nc = kc_ref.shape[1]
    nbp = ovl_ref.shape[0]
    i = pl.program_id(0)
    s0 = i * tq
    col_step = max(LANES, nc // 4)
    n_var = nc // col_step
    tiles_per_var = (nc * CMP_STRIDE // tq) // n_var
    assert tiles_per_var * tq == col_step * CMP_STRIDE

    fixed, shift = _softmax_shift(gq_ref, gk_ref)

    def variant(ncv, nbv, online):
        q = q_ref[...]
        row = lax.broadcasted_iota(jnp.int32, (tq, ncv), 0)
        col = lax.broadcasted_iota(jnp.int32, (tq, ncv), 1)
        maskb = jnp.where(col * CMP_STRIDE + (CMP_BLOCK - 1) <= s0 + row, -shift, NEG)
        has_valid = (s0 + lax.broadcasted_iota(jnp.int32, (tq, 1), 0)) >= CMP_BLOCK - 1
        sg = _sigmoid(misc_ref[...])
        ovl = ovl_ref[0:nbv, 0:ncv]
        imps = []
        for h in range(NSA_KV_HEADS):
            kc = kc_ref[h, 0:ncv, :]
            va = va_ref[h, 0:ncv, :]
            invs = []
            for g in range(NSA_GROUP):
                hd = h * NSA_GROUP + g
                sm = _nt_dot(q[:, hd * dh:(hd + 1) * dh], kc) + maskb
                if online:
                    sm = sm - jnp.max(sm, axis=-1, keepdims=True)
                e = jnp.exp(sm)
                e_ref[hd, :, 0:ncv] = e
                acc = jnp.dot(e.astype(BF16), va, preferred_element_type=F32)
                inv = jnp.where(has_valid, 1.0 / acc[:, dh:], 0.0)
                gl = h * LANES + GATE_LANE + g
                o_ref[:, hd * dh:(hd + 1) * dh] = acc[:, :dh] * inv * sg[:, gl:gl + 1]
                invs.append(inv)
            ps = None
            for g in range(NSA_GROUP):
                pc = e_ref[h * NSA_GROUP + g, :, 0:ncv] * jnp.tile(invs[g], (1, ncv // LANES))
                ps = pc if ps is None else ps + pc
            hi = ps.astype(BF16)
            r1 = ps - hi.astype(F32)
            mid = r1.astype(BF16)
            lo = (r1 - mid.astype(F32)).astype(BF16)
            imps.append(_nt_dot(ovl, hi) + _nt_dot(ovl, mid) + _nt_dot(ovl, lo))

        jb = lax.broadcasted_iota(jnp.int32, (nbv, tq), 0)
        jt = (s0 + lax.broadcasted_iota(jnp.int32, (nbv, tq), 1)) // SEL_BLOCK
        forced = (jb == 0) | (jb == jt) | (jb == jt - 1)
        future = jb > jt
        curs = [jnp.where(future | forced, -jnp.inf, imp) for imp in imps]
        for _ in range(n_sel - 3):
            for h in range(NSA_KV_HEADS):
                mx = jnp.max(curs[h], axis=0, keepdims=True)
                idx = jnp.min(jnp.where(curs[h] == mx, jb, nbv), axis=0, keepdims=True)
                curs[h] = jnp.where(jb == idx, -jnp.inf, curs[h])
        for h in range(NSA_KV_HEADS):
            bias_t = jnp.where((curs[h] == -jnp.inf) & jnp.logical_not(future), 0.0, NEG)
            bias_ref[:, h * nbp:h * nbp + nbv] = bias_t.T.astype(BF16)
            if nbv < nbp:
                bias_ref[:, h * nbp + nbv:(h + 1) * nbp] = jnp.full((tq, nbp - nbv), NEG, BF16)

    for v in range(n_var):
        ncv = col_step * (v + 1)
        nbv = min(nbp, -(-(ncv // (SEL_BLOCK // CMP_STRIDE)) // LANES) * LANES)
        pl.when(fixed & (i // tiles_per_var == v))(functools.partial(variant, ncv, nbv, False))
    pl.when(jnp.logical_not(fixed))(functools.partial(variant, nc, nbp, True))


def _cmp_attn(qn, cmp, cmp_a, ovl_t, p, gq, gk, tq, n_sel):
    s = qn.shape[0]
    nc = cmp.shape[1]
    nbp = ovl_t.shape[0]
    assert n_sel >= 3
    misc_w = NSA_KV_HEADS * LANES
    gain = pl.BlockSpec((1, NSA_HEAD_DIM), lambda i: (0, 0))
    return pl.pallas_call(
        functools.partial(_cmp_attn_kernel, n_sel=n_sel),
        grid=(s // tq,),
        in_specs=[pl.BlockSpec((tq, NSA_QW), lambda i: (i, 0)),
                  pl.BlockSpec((NSA_KV_HEADS, nc, NSA_HEAD_DIM), lambda i: (0, 0, 0)),
                  pl.BlockSpec((NSA_KV_HEADS, nc, 2 * NSA_HEAD_DIM), lambda i: (1, 0, 0)),
                  pl.BlockSpec((nbp, nc), lambda i: (0, 0)),
                  pl.BlockSpec((tq, misc_w), lambda i: (i, C_MISC // misc_w)),
                  gain, gain],
        out_specs=[pl.BlockSpec((tq, NSA_QW), lambda i: (i, 0)),
                   pl.BlockSpec((tq, NSA_KV_HEADS * nbp), lambda i: (i, 0))],
        out_shape=[jax.ShapeDtypeStruct((s, NSA_QW), F32),
                   jax.ShapeDtypeStruct((s, NSA_KV_HEADS * nbp), BF16)],
        scratch_shapes=[pltpu.VMEM((NSA_HEADS, tq, nc), F32)],
        compiler_params=_params(("arbitrary",)),
        name="cmp_attn",
    )(qn, cmp, cmp_a, ovl_t, p, gq, gk)


def _win_attn_kernel(q_ref, k_ref, v_ref, misc_ref, prev_ref, gq_ref, gk_ref, o_ref, *, span):
    tq = q_ref.shape[0]
    dh = NSA_HEAD_DIM
    s0 = pl.program_id(1) * tq
    start = pl.multiple_of(jnp.maximum(s0 - WINDOW, 0), tq)
    fixed, shift = _softmax_shift(gq_ref, gk_ref)
    row = lax.broadcasted_iota(jnp.int32, (tq, span), 0)
    col = lax.broadcasted_iota(jnp.int32, (tq, span), 1)
    dist = (s0 - start) + row - col
    maskb = jnp.where((dist >= 0) & (dist < WINDOW), -shift, NEG)

    def run(online):
        q = q_ref[...]
        k = k_ref[pl.ds(start, span), :]
        v = v_ref[pl.ds(start, span), :]
        sg = _sigmoid(misc_ref[...])
        base = GATE_LANE + 2 * NSA_GROUP
        for g in range(NSA_GROUP):
            sm = _nt_dot(q[:, g * dh:(g + 1) * dh], k) + maskb
            if online:
                sm = sm - jnp.max(sm, axis=-1, keepdims=True)
            acc = jnp.dot(jnp.exp(sm).astype(BF16), v, preferred_element_type=F32)
            o = acc[:, :dh] * (sg[:, base + g:base + g + 1] / acc[:, dh:])
            o_ref[:, g * dh:(g + 1) * dh] = prev_ref[:, g * dh:(g + 1) * dh] + o

    pl.when(fixed)(functools.partial(run, False))
    pl.when(jnp.logical_not(fixed))(functools.partial(run, True))


def _win_attn(qn, kwn, vwa, p, prev, gq, gk, tq):
    s = qn.shape[0]
    gw = NSA_GROUP * NSA_HEAD_DIM
    span = min(s, WINDOW + tq)
    gain = pl.BlockSpec((1, NSA_HEAD_DIM), lambda h, i: (0, 0))
    return pl.pallas_call(
        functools.partial(_win_attn_kernel, span=span),
        grid=(NSA_KV_HEADS, s // tq),
        in_specs=[pl.BlockSpec((tq, gw), lambda h, i: (i, h)),
                  pl.BlockSpec((s, NSA_HEAD_DIM), lambda h, i: (0, h)),
                  pl.BlockSpec((s, 2 * NSA_HEAD_DIM), lambda h, i: (0, h)),
                  pl.BlockSpec((tq, LANES), lambda h, i: (i, C_MISC // LANES + h)),
                  pl.BlockSpec((tq, gw), lambda h, i: (i, h)),
                  gain, gain],
        out_specs=pl.BlockSpec((tq, gw), lambda h, i: (i, h)),
        out_shape=jax.ShapeDtypeStruct((s, NSA_QW), F32),
        compiler_params=_params(("arbitrary", "arbitrary")),
        name="win_attn",
    )(qn, kwn, vwa, p, prev, gq, gk)


def _sel_attn_kernel(q_ref, bias_ref, ka_ref, va_ref, misc_ref, prev_ref, gq_ref, gk_ref, o_ref,
                     lhs_ref, m_ref, acc_ref, sa_ref, sb_ref, *, tk):
    tq = q_ref.shape[0]
    dh = NSA_HEAD_DIM
    n_super = lhs_ref.shape[0]
    tiles_per_super = LANES * SEL_BLOCK // tk
    s0 = pl.program_id(1) * tq
    fixed, shift = _softmax_shift(gq_ref, gk_ref)
    for sup in range(n_super):
        for g in range(NSA_GROUP):
            lhs_ref[sup, g * tq:(g + 1) * tq, 0:dh] = q_ref[:, g * dh:(g + 1) * dh]
            lhs_ref[sup, g * tq:(g + 1) * tq, dh:2 * dh] = (
                bias_ref[:, sup * LANES:(sup + 1) * LANES].astype(F32) - shift).astype(BF16)
    m_ref[...] = jnp.full_like(m_ref, -jnp.inf)
    acc_ref[...] = jnp.zeros_like(acc_ref)

    def scores(kt, sc_ref):
        k0 = pl.multiple_of(kt * tk, tk)
        k = ka_ref[pl.ds(k0, tk), :]
        sup = kt // tiles_per_super
        for g in range(NSA_GROUP):
            rows = slice(g * tq, (g + 1) * tq)
            sc_ref[rows, :] = _nt_dot(lhs_ref[sup, rows, :], k)

    def accumulate(online, kt, sc_ref, diagonal):
        k0 = pl.multiple_of(kt * tk, tk)
        v = va_ref[pl.ds(k0, tk), :]
        for g in range(NSA_GROUP):
            rows = slice(g * tq, (g + 1) * tq)
            sc = sc_ref[rows, :]
            if diagonal:
                row = lax.broadcasted_iota(jnp.int32, sc.shape, 0)
                col = lax.broadcasted_iota(jnp.int32, sc.shape, 1)
                sc = jnp.where(k0 + col <= s0 + row, sc, NEG)
            if online:
                m_old = m_ref[rows, :]
                m_new = jnp.maximum(m_old, jnp.max(sc, axis=-1, keepdims=True))
                alpha = jnp.exp(m_old - m_new)
                pr = jnp.exp(sc - jnp.tile(m_new, (1, tk // LANES)))
                acc_ref[rows, :] = (jnp.tile(alpha, (1, 2)) * acc_ref[rows, :]
                                    + jnp.dot(pr.astype(BF16), v, preferred_element_type=F32))
                m_ref[rows, :] = m_new
            else:
                acc_ref[rows, :] += jnp.dot(jnp.exp(sc).astype(BF16), v, preferred_element_type=F32)

    n_full = s0 // tk

    def run(online):
        acc_fn = functools.partial(accumulate, online)

        def body(j, carry):
            scores(2 * j + 1, sb_ref)
            acc_fn(2 * j, sa_ref, False)
            scores(2 * j + 2, sa_ref)
            acc_fn(2 * j + 1, sb_ref, False)
            return carry

        scores(0, sa_ref)
        lax.fori_loop(0, n_full // 2, body, 0)

        @pl.when(n_full % 2 == 0)
        def _():
            acc_fn(n_full, sa_ref, True)

        @pl.when(n_full % 2 == 1)
        def _():
            scores(n_full, sb_ref)
            acc_fn(n_full - 1, sa_ref, False)
            acc_fn(n_full, sb_ref, True)

    pl.when(fixed)(functools.partial(run, False))
    pl.when(jnp.logical_not(fixed))(functools.partial(run, True))

    acc = acc_ref[...]
    o = acc[:, :dh] * (_branch_gate(misc_ref[...], 1, tq) / acc[:, dh:])
    o_ref[...] = prev_ref[...] + _unstack_heads(o, tq)


def _sel_attn(qn, bias, ksa, vsa, p, prev, gq, gk, tq, tk):
    s = qn.shape[0]
    dh = NSA_HEAD_DIM
    gw = NSA_GROUP * dh
    nbp = bias.shape[1] // NSA_KV_HEADS
    n_super = nbp // LANES
    assert tk % tq == 0 and (LANES * SEL_BLOCK) % tk == 0 and s % tk == 0
    resident = lambda: pl.BlockSpec((s, 2 * dh), lambda h, i: (0, h), pipeline_mode=pl.Buffered(1))
    gain = pl.BlockSpec((1, dh), lambda h, i: (0, 0))
    return pl.pallas_call(
        functools.partial(_sel_attn_kernel, tk=tk),
        grid=(NSA_KV_HEADS, s // tq),
        in_specs=[pl.BlockSpec((tq, gw), lambda h, i: (i, h)),
                  pl.BlockSpec((tq, nbp), lambda h, i: (i, h)),
                  resident(), resident(),
                  pl.BlockSpec((tq, LANES), lambda h, i: (i, C_MISC // LANES + h)),
                  pl.BlockSpec((tq, gw), lambda h, i: (i, h)),
                  gain, gain],
        out_specs=pl.BlockSpec((tq, gw), lambda h, i: (i, h)),
        out_shape=jax.ShapeDtypeStruct((s, NSA_QW), F32),
        scratch_shapes=[pltpu.VMEM((n_super, NSA_GROUP * tq, 2 * dh), BF16),
                        pltpu.VMEM((NSA_GROUP * tq, LANES), F32),
                        pltpu.VMEM((NSA_GROUP * tq, 2 * dh), F32),
                        pltpu.VMEM((NSA_GROUP * tq, tk), F32),
                        pltpu.VMEM((NSA_GROUP * tq, tk), F32)],
        compiler_params=_params(("arbitrary", "arbitrary")),
        name="sel_attn",
    )(qn, bias, ksa, vsa, p, prev, gq, gk)


def _out_proj_kernel(x_ref, a_ref, b_ref, wa_ref, wb_ref, o_ref):
    o_ref[...] = (x_ref[...]
                  + jnp.dot(a_ref[...].astype(BF16), wa_ref[...], preferred_element_type=F32)
                  + jnp.dot(b_ref[...].astype(BF16), wb_ref[...], preferred_element_type=F32))


def _out_proj(x, a, b, w, tm):
    s, d = x.shape
    ka, kb = a.shape[1], b.shape[1]
    assert ka == kb
    return pl.pallas_call(
        _out_proj_kernel,
        grid=(s // tm,),
        in_specs=[pl.BlockSpec((tm, d), lambda i: (i, 0)),
                  pl.BlockSpec((tm, ka), lambda i: (i, 0)),
                  pl.BlockSpec((tm, kb), lambda i: (i, 0)),
                  pl.BlockSpec((ka, d), lambda i: (0, 0)),
                  pl.BlockSpec((kb, d), lambda i: (1, 0))],
        out_specs=pl.BlockSpec((tm, d), lambda i: (i, 0)),
        out_shape=jax.ShapeDtypeStruct((s, d), F32),
        compiler_params=_params(("arbitrary",)),
        name="out_proj",
    )(x, a, b, w, w)


def _ffn_kernel(h_ref, g_ref, wg_ref, wu_ref, wd_ref, o_ref, hn_ref):
    @pl.when(pl.program_id(1) == 0)
    def _():
        h = h_ref[...]
        hn_ref[...] = _rms(h, g_ref[...]).astype(BF16)
        o_ref[...] = h

    hn = hn_ref[...]
    a = jnp.dot(hn, wg_ref[...].astype(BF16), preferred_element_type=F32)
    u = jnp.dot(hn, wu_ref[...].astype(BF16), preferred_element_type=F32)
    z = (a * _sigmoid(a) * u).astype(BF16)
    o_ref[...] += jnp.dot(z, wd_ref[...].astype(BF16), preferred_element_type=F32)


def _ffn(h, g, wg, wu, wd, tm, tf):
    s, d = h.shape
    f = wg.shape[1]
    return pl.pallas_call(
        _ffn_kernel,
        grid=(s // tm, f // tf),
        in_specs=[pl.BlockSpec((tm, d), lambda i, j: (i, 0)),
                  pl.BlockSpec((1, d), lambda i, j: (0, 0)),
                  pl.BlockSpec((d, tf), lambda i, j: (0, j)),
                  pl.BlockSpec((d, tf), lambda i, j: (0, j)),
                  pl.BlockSpec((tf, d), lambda i, j: (j, 0))],
        out_specs=pl.BlockSpec((tm, d), lambda i, j: (i, 0)),
        out_shape=jax.ShapeDtypeStruct((s, d), F32),
        scratch_shapes=[pltpu.VMEM((tm, d), BF16)],
        compiler_params=_params(("arbitrary", "arbitrary")),
        name="ffn",
    )(h, g, wg, wu, wd)


_W_IN_LR = GLA_QKV
_W_IN_GO = _W_IN_LR + GLA_GATE_RANK
_W_IN_NG = _W_IN_GO + GLA_VW + NSA_QW + 6 * NSA_KVW


def _regroup_kernel(w_ref, misc_ref, o_ref):
    o_ref[0:GLA_QKV, :] = w_ref[0:GLA_QKV, :].astype(BF16)
    o_ref[GLA_QKV:C_MISC, :] = w_ref[_W_IN_GO:_W_IN_NG, :].astype(BF16)
    o_ref[C_MISC:PROJ_PAD, :] = misc_ref[...]


def _regroup_w_in(w_in, tile=256):
    d = w_in.shape[0]
    wt = w_in.T
    lr = wt[_W_IN_LR:_W_IN_GO]
    ng = wt[_W_IN_NG:_W_IN_NG + 3 * NSA_HEADS].reshape(NSA_KV_HEADS, NSA_GROUP, 3, d)
    ng = ng.transpose(0, 2, 1, 3).reshape(NSA_KV_HEADS, 3 * NSA_GROUP, d)
    pad = lambda n: jnp.zeros((n, d), w_in.dtype)
    tail = pad(LANES - GLA_GATE_RANK - 3 * NSA_GROUP)
    misc = jnp.concatenate([lr, ng[0], tail, pad(GLA_GATE_RANK), ng[1], tail], axis=0).astype(BF16)
    return pl.pallas_call(
        _regroup_kernel,
        grid=(d // tile,),
        in_specs=[pl.BlockSpec((wt.shape[0], tile), lambda i: (0, i)),
                  pl.BlockSpec((2 * LANES, tile), lambda i: (0, i))],
        out_specs=pl.BlockSpec((PROJ_PAD, tile), lambda i: (0, i)),
        out_shape=jax.ShapeDtypeStruct((PROJ_PAD, d), BF16),
        compiler_params=_params(("arbitrary",)),
        name="regroup_w_in",
    )(wt, misc)


def _overlap_t(s, nbp):
    n = s // CMP_STRIDE
    c0 = np.arange(n)[None, :] * CMP_STRIDE
    s0 = np.arange(nbp)[:, None] * SEL_BLOCK
    ov = np.clip(np.minimum(c0 + CMP_BLOCK, s0 + SEL_BLOCK) - np.maximum(c0, s0), 0, None) / CMP_STRIDE
    ov[:, n - 1] = 0.0
    ov[s // SEL_BLOCK:] = 0.0
    return jnp.asarray(ov, BF16)


def _layer(x, attn_norm_g, w_in, gla_conv_w, gla_gate_w2, gla_gate_b, gla_norm_g,
           nsa_q_norm_g, nsa_kc_norm_g, nsa_ks_norm_g, nsa_kw_norm_g,
           cmp_k_pos, cmp_k_w1, cmp_k_w2, cmp_v_pos, cmp_v_w1, cmp_v_w2,
           w_out, ffn_norm_g, w_gate, w_up, w_down):
    s = x.shape[0]
    dh = NSA_HEAD_DIM
    row = lambda v: v.reshape(1, -1)
    big = s >= 4096
    tm = 1024 if big else 256

    p = _norm_matmul(x, row(attn_norm_g), _regroup_w_in(w_in), 512 if big else 256,
                     PROJ_PAD // 2 if big else LANES)

    gla_out = _gla(p, gla_conv_w, gla_gate_w2, row(gla_gate_b), row(gla_norm_g), 1024 if big else 512)

    qn, ksa, vsa, kwn, vwa = _nsa_prep(p, row(nsa_q_norm_g), row(nsa_ks_norm_g), row(nsa_kw_norm_g), 512)

    w1 = jnp.stack([cmp_k_w1, cmp_v_w1])
    w1s = (w1.reshape(2, 2, CMP_STRIDE, dh, CMP_HIDDEN).transpose(0, 2, 3, 1, 4)
           .reshape(2, CMP_STRIDE, dh, 2 * CMP_HIDDEN).astype(BF16))
    pos = jnp.stack([cmp_k_pos, cmp_v_pos]).reshape(2, 1, CMP_BLOCK * dh)
    w2 = jnp.stack([cmp_k_w2, cmp_v_w2]).astype(BF16)
    cmp, cmp_a = _compress(p, w1s, pos, w1, w2, row(nsa_kc_norm_g))

    nb = s // SEL_BLOCK
    nbp = -(-nb // LANES) * LANES
    gq = row(nsa_q_norm_g)
    o_cmp, bias = _cmp_attn(qn, cmp, cmp_a, _overlap_t(s, nbp), p, gq, row(nsa_kc_norm_g),
                            512 if big else 128, min(SEL_TOPK, nb))
    o_cw = _win_attn(qn, kwn, vwa, p, o_cmp, gq, row(nsa_kw_norm_g), 256)
    nsa_out = _sel_attn(qn, bias, ksa, vsa, p, o_cw, gq, row(nsa_ks_norm_g),
                        512 if big else 256, 1024 if big else 512)

    h = _out_proj(x, gla_out, nsa_out, w_out.astype(BF16), 512 if big else 256)
    return _ffn(h, row(ffn_norm_g), w_gate, w_up, w_down, 1024 if big else 256, 256)


def kernel(x, attn_norm_g, w_in, gla_conv_w, gla_gate_w2, gla_gate_b, gla_norm_g, nsa_q_norm_g, nsa_kc_norm_g, nsa_ks_norm_g, nsa_kw_norm_g, cmp_k_pos, cmp_k_w1, cmp_k_w2, cmp_v_pos, cmp_v_w1, cmp_v_w2, w_out, ffn_norm_g, w_gate, w_up, w_down):
    assert x.shape[0] == 1 and attn_norm_g.shape[0] == 1
    y = _layer(x[0], attn_norm_g[0], w_in[0], gla_conv_w[0], gla_gate_w2[0], gla_gate_b[0],
               gla_norm_g[0], nsa_q_norm_g[0], nsa_kc_norm_g[0], nsa_ks_norm_g[0], nsa_kw_norm_g[0],
               cmp_k_pos[0], cmp_k_w1[0], cmp_k_w2[0], cmp_v_pos[0], cmp_v_w1[0], cmp_v_w2[0],
               w_out[0], ffn_norm_g[0], w_gate[0], w_up[0], w_down[0])
    return y[None]
```

```python
import functools

import jax
import jax.numpy as jnp
import numpy as np
from jax import lax
from jax.experimental import pallas as pl
from jax.experimental.pallas import tpu as pltpu

D_MODEL = 2048
GLA_HEADS = 4
GLA_DK = 128
GLA_DV = 256
GLA_GATE_RANK = 16
GLA_GATE_TAU = 16.0
GLA_CHUNK = 64
GLA_SUB = 16
GLA_SAFE_LOG_DECAY = 60.0
GLA_HEADS_PER_STEP = 1
SOFTMAX_SHIFT_LIMIT = 40.0
CONV_WIDTH = 4

NSA_HEADS = 8
NSA_KV_HEADS = 2
NSA_GROUP = 4
NSA_HEAD_DIM = 128
CMP_BLOCK = 32
CMP_STRIDE = 16
CMP_HIDDEN = 128
SEL_BLOCK = 64
SEL_TOPK = 16
WINDOW = 512
D_FF = 5632
EPS = 1e-6
NEG = -1e30

GLA_QK = GLA_HEADS * GLA_DK
GLA_VW = GLA_HEADS * GLA_DV
GLA_QKV = 2 * GLA_QK + GLA_VW
NSA_QW = NSA_HEADS * NSA_HEAD_DIM
NSA_KVW = NSA_KV_HEADS * NSA_HEAD_DIM

LANES = 128
VMEM_LIMIT = 56 * 1024 * 1024

C_GQ = 0
C_GK = GLA_QK
C_GV = 2 * GLA_QK
C_GO = GLA_QKV
C_NQ = C_GO + GLA_VW
C_KC = C_NQ + NSA_QW
C_VC = C_KC + NSA_KVW
C_KS = C_VC + NSA_KVW
C_VS = C_KS + NSA_KVW
C_KW = C_VS + NSA_KVW
C_VW = C_KW + NSA_KVW
C_MISC = C_VW + NSA_KVW
PROJ_PAD = C_MISC + 2 * LANES
GATE_LANE = GLA_GATE_RANK

F32 = jnp.float32
BF16 = jnp.bfloat16


def _params(sem):
    return pltpu.CompilerParams(dimension_semantics=sem, vmem_limit_bytes=VMEM_LIMIT)


def _nt_dot(a, b):
    return lax.dot_general(a, b, (((1,), (1,)), ((), ())), preferred_element_type=F32)


def _sigmoid(x):
    return 1.0 / (1.0 + jnp.exp(-x))


def _rms(x, g):
    return x * lax.rsqrt(jnp.mean(x * x, axis=-1, keepdims=True) + EPS) * g


def _norm_matmul_kernel(x_ref, g_ref, w_ref, o_ref, xn_ref):
    @pl.when(pl.program_id(1) == 0)
    def _():
        xn_ref[...] = _rms(x_ref[...], g_ref[...]).astype(BF16)

    o_ref[...] = _nt_dot(xn_ref[...], w_ref[...])


def _norm_matmul(x, g, wt, tm, tn):
    s, d = x.shape
    n = wt.shape[0]
    return pl.pallas_call(
        _norm_matmul_kernel,
        grid=(s // tm, n // tn),
        in_specs=[pl.BlockSpec((tm, d), lambda i, j: (i, 0)),
                  pl.BlockSpec((1, d), lambda i, j: (0, 0)),
                  pl.BlockSpec((tn, d), lambda i, j: (j, 0))],
        out_specs=pl.BlockSpec((tm, tn), lambda i, j: (i, j)),
        out_shape=jax.ShapeDtypeStruct((s, n), F32),
        scratch_shapes=[pltpu.VMEM((tm, d), BF16)],
        compiler_params=_params(("arbitrary", "arbitrary")),
        name="in_proj",
    )(x, g, wt)


def _gla_kernel(q_ref, qp_ref, k_ref, kp_ref, v_ref, vp_ref, lr_ref, go_ref,
                cwq_ref, cwk_ref, cwv_ref, w2_ref, gb_ref, ng_ref, o_ref,
                st_ref, sq_ref, sk_ref, sv_ref):
    t_rows = q_ref.shape[0]
    pad = GLA_SUB
    first = pl.program_id(1) == 0

    @pl.when(first)
    def _():
        st_ref[...] = jnp.zeros_like(st_ref)

    def cols(ref, hh, width):
        return ref.at[:, hh * width:(hh + 1) * width]

    def conv_silu(u_ref, p_ref, w_ref, s_ref):
        w = w_ref[...]
        s_ref[0:8, :] = jnp.where(first, 0.0, p_ref[...])
        s_ref[8:16, :] = u_ref[0:8, :]
        head = u_ref[0:8, :] * w[CONV_WIDTH - 1:CONV_WIDTH]
        body = u_ref[8:, :] * w[CONV_WIDTH - 1:CONV_WIDTH]
        for d in range(1, CONV_WIDTH):
            wd = w[CONV_WIDTH - 1 - d:CONV_WIDTH - d]
            head = head + s_ref[pl.ds(8 - d, 8), :] * wd
            body = body + u_ref[pl.ds(8 - d, t_rows - 8), :] * wd
        acc = jnp.concatenate([head, body], axis=0)
        return acc * _sigmoid(acc)

    row = lax.broadcasted_iota(jnp.int32, (t_rows, GLA_DK), 0)
    lane = lax.broadcasted_iota(jnp.int32, (t_rows, GLA_DK), 1)
    rc = row & (GLA_CHUNK - 1)

    def front(hh):
        dk, dv = GLA_DK, GLA_DV
        q = conv_silu(cols(q_ref, hh, dk), cols(qp_ref, hh, dk), cols(cwq_ref, hh, dk), sq_ref.at[hh])
        k = conv_silu(cols(k_ref, hh, dk), cols(kp_ref, hh, dk), cols(cwk_ref, hh, dk), sk_ref.at[hh])
        v = conv_silu(cols(v_ref, hh, dv), cols(vp_ref, hh, dv), cols(cwv_ref, hh, dv), sv_ref.at[hh])
        z = jnp.dot(lr_ref[:, :GLA_GATE_RANK], w2_ref[:, hh * dk:(hh + 1) * dk], preferred_element_type=F32,
                    precision=lax.Precision.HIGHEST) + gb_ref[:, hh * dk:(hh + 1) * dk]
        b = (jnp.minimum(z, 0.0) - jnp.log(1.0 + jnp.exp(-jnp.abs(z)))) * (1.0 / GLA_GATE_TAU)
        d = 1
        while d < GLA_CHUNK:
            b = b + jnp.where(rc >= d, pltpu.roll(b, d, 0), 0.0)
            d *= 2
        return q * (dk ** -0.5), k, v, b

    heads = [front(hh) for hh in range(GLA_HEADS_PER_STEP)]

    srow = lax.broadcasted_iota(jnp.int32, (GLA_CHUNK, GLA_DK), 0)
    scol = lax.broadcasted_iota(jnp.int32, (GLA_CHUNK, GLA_DK), 1)
    zk = jnp.zeros((LANES - GLA_CHUNK, GLA_DK), BF16)
    zv = jnp.zeros((LANES - GLA_CHUNK, GLA_DV), BF16)
    za = jnp.zeros((GLA_SUB, LANES), F32)
    gain = ng_ref[...]

    def chunk_loop(intra_scores):
        for c in range(t_rows // GLA_CHUNK):
            lo = c * GLA_CHUNK
            for hh, (q, k, v, b) in enumerate(heads):
                bc = b[lo:lo + GLA_CHUNK]
                qc = q[lo:lo + GLA_CHUNK]
                kc = k[lo:lo + GLA_CHUNK]
                vc = v[lo:lo + GLA_CHUNK].astype(BF16)
                st = st_ref[hh]
                b_last = bc[GLA_CHUNK - 1:GLA_CHUNK]
                qd = (qc * jnp.exp(bc)).astype(BF16)
                scores = intra_scores(hh, lo, bc, qc, kc, qd)
                o = _nt_dot(qd, st.astype(BF16)) + jnp.dot(
                    scores.astype(BF16), jnp.concatenate([vc, zv], axis=0), preferred_element_type=F32)
                go = go_ref[lo:lo + GLA_CHUNK, hh * GLA_DV:(hh + 1) * GLA_DV]
                o_ref[lo:lo + GLA_CHUNK, hh * GLA_DV:(hh + 1) * GLA_DV] = _rms(o, gain) * (go * _sigmoid(go))
                kt = (kc * jnp.exp(b_last - bc)).astype(BF16)
                upd = lax.dot_general(vc, kt, (((0,), (0,)), ((), ())), preferred_element_type=F32)
                st_ref[hh] = st * jnp.exp(b_last) + upd

    b_min = heads[0][3]
    for hd in heads[1:]:
        b_min = jnp.minimum(b_min, hd[3])
    small_decay = jnp.min(b_min) >= -GLA_SAFE_LOG_DECAY

    @pl.when(small_decay)
    def _():
        def intra_scores(hh, lo, bc, qc, kc, qd):
            kd = (kc * jnp.exp(-bc)).astype(BF16)
            a = _nt_dot(qd, jnp.concatenate([kd, zk], axis=0))
            return jnp.where(scol <= srow, a, 0.0)

        chunk_loop(intra_scores)

    @pl.when(jnp.logical_not(small_decay))
    def _():
        zero_pad = jnp.zeros((pad, GLA_DK), F32)
        rs = row & (GLA_SUB - 1)
        dl = rc - lane
        bands = []
        for hh, (q, k, v, b) in enumerate(heads):
            sq_ref[hh, 0:pad, :] = zero_pad
            sk_ref[hh, 0:pad, :] = zero_pad
            sq_ref[hh, pad:, :] = b
            sk_ref[hh, pad:, :] = k
            band = jnp.where(dl == 0, jnp.sum(q * k, axis=-1, keepdims=True), 0.0)
            for delta in range(1, GLA_SUB):
                e = jnp.exp(jnp.where(rs >= delta, b - sq_ref[hh, pl.ds(pad - delta, t_rows), :], -jnp.inf))
                sc = jnp.sum(q * sk_ref[hh, pl.ds(pad - delta, t_rows), :] * e, axis=-1, keepdims=True)
                band = jnp.where(dl == delta, sc, band)
            bands.append(band)

        def intra_scores(hh, lo, bc, qc, kc, qd):
            blocks = [za]
            for sub in range(1, GLA_CHUNK // GLA_SUB):
                r0 = sub * GLA_SUB
                ref_b = bc[r0:r0 + 1]
                qq = (qc[r0:r0 + GLA_SUB] * jnp.exp(bc[r0:r0 + GLA_SUB] - ref_b)).astype(BF16)
                kk = (kc * jnp.exp(jnp.where(srow < r0, ref_b - bc, -jnp.inf))).astype(BF16)
                blocks.append(_nt_dot(qq, jnp.concatenate([kk, zk], axis=0)))
            return bands[hh][lo:lo + GLA_CHUNK] + jnp.concatenate(blocks, axis=0)

        chunk_loop(intra_scores)


def _gla(p, conv_w, w2, gb, ng, tile):
    s = p.shape[0]
    t8 = tile // 8
    nh = GLA_HEADS_PER_STEP
    dk, dv = nh * GLA_DK, nh * GLA_DV

    def prev(col):
        return lambda h, i: (jnp.maximum(i * t8 - 1, 0), col(h))

    qcol = lambda h: C_GQ // dk + h
    kcol = lambda h: C_GK // dk + h
    vcol = lambda h: C_GV // dv + h
    return pl.pallas_call(
        _gla_kernel,
        grid=(GLA_HEADS // nh, s // tile),
        in_specs=[
            pl.BlockSpec((tile, dk), lambda h, i: (i, qcol(h))),
            pl.BlockSpec((8, dk), prev(qcol)),
            pl.BlockSpec((tile, dk), lambda h, i: (i, kcol(h))),
            pl.BlockSpec((8, dk), prev(kcol)),
            pl.BlockSpec((tile, dv), lambda h, i: (i, vcol(h))),
            pl.BlockSpec((8, dv), prev(vcol)),
            pl.BlockSpec((tile, LANES), lambda h, i: (i, C_MISC // LANES)),
            pl.BlockSpec((tile, dv), lambda h, i: (i, C_GO // dv + h)),
            pl.BlockSpec((CONV_WIDTH, dk), lambda h, i: (0, qcol(h))),
            pl.BlockSpec((CONV_WIDTH, dk), lambda h, i: (0, kcol(h))),
            pl.BlockSpec((CONV_WIDTH, dv), lambda h, i: (0, vcol(h))),
            pl.BlockSpec((GLA_GATE_RANK, dk), lambda h, i: (0, h)),
            pl.BlockSpec((1, dk), lambda h, i: (0, h)),
            pl.BlockSpec((1, GLA_DV), lambda h, i: (0, 0)),
        ],
        out_specs=pl.BlockSpec((tile, dv), lambda h, i: (i, h)),
        out_shape=jax.ShapeDtypeStruct((s, GLA_VW), F32),
        scratch_shapes=[pltpu.VMEM((nh, GLA_DV, GLA_DK), F32),
                        pltpu.VMEM((nh, tile + GLA_SUB, GLA_DK), F32),
                        pltpu.VMEM((nh, tile + GLA_SUB, GLA_DK), F32),
                        pltpu.VMEM((nh, GLA_SUB, GLA_DV), F32)],
        compiler_params=_params(("arbitrary", "arbitrary")),
        name="gla",
    )(p, p, p, p, p, p, p, p, conv_w, conv_w, conv_w, w2, gb, ng)


def _nsa_prep_kernel(q_ref, ks_ref, vs_ref, kw_ref, vw_ref, qg_ref, ksg_ref, kwg_ref,
                     qn_ref, ksa_ref, vsa_ref, kwn_ref, vwa_ref):
    t_rows = q_ref.shape[0]
    dh = NSA_HEAD_DIM
    scale = dh ** -0.5
    ones = jnp.ones((t_rows, dh), BF16)
    for h in range(NSA_HEADS):
        sl = slice(h * dh, (h + 1) * dh)
        qn_ref[:, sl] = (_rms(q_ref[:, sl], qg_ref[...]) * scale).astype(BF16)
    pos = pl.program_id(0) * t_rows + lax.broadcasted_iota(jnp.int32, (t_rows, LANES), 0)
    lane = lax.broadcasted_iota(jnp.int32, (t_rows, LANES), 1)
    onehot = jnp.where(lane == ((pos // SEL_BLOCK) & (LANES - 1)), 1.0, 0.0).astype(BF16)
    for h in range(NSA_KV_HEADS):
        sl = slice(h * dh, (h + 1) * dh)
        ksa_ref[:, 2 * h * dh:(2 * h + 1) * dh] = _rms(ks_ref[:, sl], ksg_ref[...]).astype(BF16)
        ksa_ref[:, (2 * h + 1) * dh:(2 * h + 2) * dh] = onehot
        kwn_ref[:, sl] = _rms(kw_ref[:, sl], kwg_ref[...]).astype(BF16)
        vsa_ref[:, 2 * h * dh:(2 * h + 1) * dh] = vs_ref[:, sl].astype(BF16)
        vsa_ref[:, (2 * h + 1) * dh:(2 * h + 2) * dh] = ones
        vwa_ref[:, 2 * h * dh:(2 * h + 1) * dh] = vw_ref[:, sl].astype(BF16)
        vwa_ref[:, (2 * h + 1) * dh:(2 * h + 2) * dh] = ones


def _nsa_prep(p, qg, ksg, kwg, tile):
    s = p.shape[0]
    kv = NSA_KVW
    col = lambda c, w: (lambda i: (i, c // w))
    row = lambda w: pl.BlockSpec((tile, w), lambda i: (i, 0))
    gain = pl.BlockSpec((1, NSA_HEAD_DIM), lambda i: (0, 0))
    return pl.pallas_call(
        _nsa_prep_kernel,
        grid=(s // tile,),
        in_specs=[pl.BlockSpec((tile, NSA_QW), col(C_NQ, NSA_QW)),
                  pl.BlockSpec((tile, kv), col(C_KS, kv)),
                  pl.BlockSpec((tile, kv), col(C_VS, kv)),
                  pl.BlockSpec((tile, kv), col(C_KW, kv)),
                  pl.BlockSpec((tile, kv), col(C_VW, kv)),
                  gain, gain, gain],
        out_specs=[row(NSA_QW), row(2 * kv), row(2 * kv), row(kv), row(2 * kv)],
        out_shape=[jax.ShapeDtypeStruct((s, NSA_QW), BF16),
                   jax.ShapeDtypeStruct((s, 2 * kv), BF16),
                   jax.ShapeDtypeStruct((s, 2 * kv), BF16),
                   jax.ShapeDtypeStruct((s, kv), BF16),
                   jax.ShapeDtypeStruct((s, 2 * kv), BF16)],
        compiler_params=_params(("arbitrary",)),
        name="nsa_prep",
    )(p, p, p, p, p, qg, ksg, kwg)


def _compress_kernel(u_ref, w1_ref, pos_ref, w1f_ref, w2_ref, g_ref, o_ref, oa_ref):
    n = o_ref.shape[1]
    half = CMP_BLOCK // CMP_STRIDE
    assert half == 2
    acc = jnp.zeros((n, 2 * CMP_HIDDEN), F32)
    for l in range(CMP_STRIDE):
        x = u_ref[pl.ds(l, n, stride=CMP_STRIDE), :].astype(BF16)
        acc = acc + jnp.dot(x, w1_ref[0, l], preferred_element_type=F32)
    posb = jnp.dot(jnp.broadcast_to(pos_ref[0], (8, pos_ref.shape[2])), w1f_ref[0],
                   preferred_element_type=F32, precision=lax.Precision.HIGHEST)[0:1]
    hid = acc[:, :CMP_HIDDEN] + pltpu.roll(acc[:, CMP_HIDDEN:], n - 1, 0) + posb
    hid = hid * _sigmoid(hid)
    out = jnp.dot(hid.astype(BF16), w2_ref[0], preferred_element_type=F32)
    is_k = pl.program_id(0) < NSA_KV_HEADS
    out = jnp.where(is_k, _rms(out, g_ref[...]), out)
    row = lax.broadcasted_iota(jnp.int32, out.shape, 0)
    out = jnp.where(row < n - 1, out, 0.0).astype(BF16)
    o_ref[0] = out
    oa_ref[0] = jnp.concatenate([out, jnp.ones_like(out)], axis=1)


def _compress(p, w1s, pos, w1f, w2, g):
    s = p.shape[0]
    n = s // CMP_STRIDE
    dh = NSA_HEAD_DIM
    return pl.pallas_call(
        _compress_kernel,
        grid=(2 * NSA_KV_HEADS,),
        in_specs=[pl.BlockSpec((s, dh), lambda j: (0, C_KC // dh + j)),
                  pl.BlockSpec((1, CMP_STRIDE, dh, 2 * CMP_HIDDEN), lambda j: (j // 2, 0, 0, 0)),
                  pl.BlockSpec((1, 1, CMP_BLOCK * dh), lambda j: (j // 2, 0, 0)),
                  pl.BlockSpec((1, CMP_BLOCK * dh, CMP_HIDDEN), lambda j: (j // 2, 0, 0)),
                  pl.BlockSpec((1, CMP_HIDDEN, dh), lambda j: (j // 2, 0, 0)),
                  pl.BlockSpec((1, dh), lambda j: (0, 0))],
        out_specs=[pl.BlockSpec((1, n, dh), lambda j: (j, 0, 0)),
                   pl.BlockSpec((1, n, 2 * dh), lambda j: (j, 0, 0))],
        out_shape=[jax.ShapeDtypeStruct((2 * NSA_KV_HEADS, n, dh), BF16),
                   jax.ShapeDtypeStruct((2 * NSA_KV_HEADS, n, 2 * dh), BF16)],
        compiler_params=_params(("arbitrary",)),
        name="compress",
    )(p, w1s, pos, w1f, w2, g)


def _softmax_shift(gq_ref, gk_ref):
    bound = (NSA_HEAD_DIM ** 0.5) * jnp.max(jnp.abs(gq_ref[...])) * jnp.max(jnp.abs(gk_ref[...]))
    usable = bound <= SOFTMAX_SHIFT_LIMIT
    return usable, jnp.where(usable, bound, 0.0)


def _unstack_heads(o, tq):
    return jnp.concatenate([o[g * tq:(g + 1) * tq] for g in range(NSA_GROUP)], axis=1)


def _branch_gate(misc, branch, tq):
    sg = _sigmoid(misc)
    base = GATE_LANE + branch * NSA_GROUP
    return jnp.concatenate([sg[:, base + g:base + g + 1] for g in range(NSA_GROUP)], axis=0)


def _cmp_attn_kernel(q_ref, kc_ref, va_ref, ovl_ref, misc_ref, gq_ref, gk_ref, o_ref, bias_ref, e_ref,
                     *, n_sel):
    tq = q_ref.shape[0]
    dh = NSA_HEAD_DIM
    nc = kc_ref.shape[1]
    nbp = ovl_ref.shape[0]
    i = pl.program_id(0)
    s0 = i * tq
    col_step = max(LANES, nc // 4)
    n_var = nc // col_step
    tiles_per_var = (nc * CMP_STRIDE // tq) // n_var
    assert tiles_per_var * tq == col_step * CMP_STRIDE

    fixed, shift = _softmax_shift(gq_ref, gk_ref)

    def variant(ncv, nbv, online):
        q = q_ref[...]
        row = lax.broadcasted_iota(jnp.int32, (tq, ncv), 0)
        col = lax.broadcasted_iota(jnp.int32, (tq, ncv), 1)
        maskb = jnp.where(col * CMP_STRIDE + (CMP_BLOCK - 1) <= s0 + row, -shift, NEG)
        has_valid = (s0 + lax.broadcasted_iota(jnp.int32, (tq, 1), 0)) >= CMP_BLOCK - 1
        sg = _sigmoid(misc_ref[...])
        ovl = ovl_ref[0:nbv, 0:ncv]
        imps = []
        for h in range(NSA_KV_HEADS):
            kc = kc_ref[h, 0:ncv, :]
            va = va_ref[h, 0:ncv, :]
            invs = []
            for g in range(NSA_GROUP):
                hd = h * NSA_GROUP + g
                sm = _nt_dot(q[:, hd * dh:(hd + 1) * dh], kc) + maskb
                if online:
                    sm = sm - jnp.max(sm, axis=-1, keepdims=True)
                e = jnp.exp(sm)
                e_ref[hd, :, 0:ncv] = e
                acc = jnp.dot(e.astype(BF16), va, preferred_element_type=F32)
                inv = jnp.where(has_valid, 1.0 / acc[:, dh:], 0.0)
                gl = h * LANES + GATE_LANE + g
                o_ref[:, hd * dh:(hd + 1) * dh] = acc[:, :dh] * inv * sg[:, gl:gl + 1]
                invs.append(inv)
            ps = None
            for g in range(NSA_GROUP):
                pc = e_ref[h * NSA_GROUP + g, :, 0:ncv] * jnp.tile(invs[g], (1, ncv // LANES))
                ps = pc if ps is None else ps + pc
            hi = ps.astype(BF16)
            r1 = ps - hi.astype(F32)
            mid = r1.astype(BF16)
            lo = (r1 - mid.astype(F32)).astype(BF16)
            imps.append(_nt_dot(ovl, hi) + _nt_dot(ovl, mid) + _nt_dot(ovl, lo))

        jb = lax.broadcasted_iota(jnp.int32, (nbv, tq), 0)
        jt = (s0 + lax.broadcasted_iota(jnp.int32, (nbv, tq), 1)) // SEL_BLOCK
        forced = (jb == 0) | (jb == jt) | (jb == jt - 1)
        future = jb > jt
        curs = [jnp.where(future | forced, -jnp.inf, imp) for imp in imps]
        for _ in range(n_sel - 3):
            for h in range(NSA_KV_HEADS):
                mx = jnp.max(curs[h], axis=0, keepdims=True)
                idx = jnp.min(jnp.where(curs[h] == mx, jb, nbv), axis=0, keepdims=True)
                curs[h] = jnp.where(jb == idx, -jnp.inf, curs[h])
        for h in range(NSA_KV_HEADS):
            bias_t = jnp.where((curs[h] == -jnp.inf) & jnp.logical_not(future), 0.0, NEG)
            bias_ref[:, h * nbp:h * nbp + nbv] = bias_t.T.astype(BF16)
            if nbv < nbp:
                bias_ref[:, h * nbp + nbv:(h + 1) * nbp] = jnp.full((tq, nbp - nbv), NEG, BF16)

    for v in range(n_var):
        ncv = col_step * (v + 1)
        nbv = min(nbp, -(-(ncv // (SEL_BLOCK // CMP_STRIDE)) // LANES) * LANES)
        pl.when(fixed & (i // tiles_per_var == v))(functools.partial(variant, ncv, nbv, False))
    pl.when(jnp.logical_not(fixed))(functools.partial(variant, nc, nbp, True))


def _cmp_attn(qn, cmp, cmp_a, ovl_t, p, gq, gk, tq, n_sel):
    s = qn.shape[0]
    nc = cmp.shape[1]
    nbp = ovl_t.shape[0]
    assert n_sel >= 3
    misc_w = NSA_KV_HEADS * LANES
    gain = pl.BlockSpec((1, NSA_HEAD_DIM), lambda i: (0, 0))
    return pl.pallas_call(
        functools.partial(_cmp_attn_kernel, n_sel=n_sel),
        grid=(s // tq,),
        in_specs=[pl.BlockSpec((tq, NSA_QW), lambda i: (i, 0)),
                  pl.BlockSpec((NSA_KV_HEADS, nc, NSA_HEAD_DIM), lambda i: (0, 0, 0)),
                  pl.BlockSpec((NSA_KV_HEADS, nc, 2 * NSA_HEAD_DIM), lambda i: (1, 0, 0)),
                  pl.BlockSpec((nbp, nc), lambda i: (0, 0)),
                  pl.BlockSpec((tq, misc_w), lambda i: (i, C_MISC // misc_w)),
                  gain, gain],
        out_specs=[pl.BlockSpec((tq, NSA_QW), lambda i: (i, 0)),
                   pl.BlockSpec((tq, NSA_KV_HEADS * nbp), lambda i: (i, 0))],
        out_shape=[jax.ShapeDtypeStruct((s, NSA_QW), F32),
                   jax.ShapeDtypeStruct((s, NSA_KV_HEADS * nbp), BF16)],
        scratch_shapes=[pltpu.VMEM((NSA_HEADS, tq, nc), F32)],
        compiler_params=_params(("arbitrary",)),
        name="cmp_attn",
    )(qn, cmp, cmp_a, ovl_t, p, gq, gk)


def _win_attn_kernel(q_ref, k_ref, v_ref, misc_ref, prev_ref, gq_ref, gk_ref, o_ref, *, span):
    tq = q_ref.shape[0]
    dh = NSA_HEAD_DIM
    s0 = pl.program_id(1) * tq
    start = pl.multiple_of(jnp.maximum(s0 - WINDOW, 0), tq)
    fixed, shift = _softmax_shift(gq_ref, gk_ref)
    row = lax.broadcasted_iota(jnp.int32, (tq, span), 0)
    col = lax.broadcasted_iota(jnp.int32, (tq, span), 1)
    dist = (s0 - start) + row - col
    maskb = jnp.where((dist >= 0) & (dist < WINDOW), -shift, NEG)

    def run(online):
        q = q_ref[...]
        k = k_ref[pl.ds(start, span), :]
        v = v_ref[pl.ds(start, span), :]
        sg = _sigmoid(misc_ref[...])
        base = GATE_LANE + 2 * NSA_GROUP
        for g in range(NSA_GROUP):
            sm = _nt_dot(q[:, g * dh:(g + 1) * dh], k) + maskb
            if online:
                sm = sm - jnp.max(sm, axis=-1, keepdims=True)
            acc = jnp.dot(jnp.exp(sm).astype(BF16), v, preferred_element_type=F32)
            o = acc[:, :dh] * (sg[:, base + g:base + g + 1] / acc[:, dh:])
            o_ref[:, g * dh:(g + 1) * dh] = prev_ref[:, g * dh:(g + 1) * dh] + o

    pl.when(fixed)(functools.partial(run, False))
    pl.when(jnp.logical_not(fixed))(functools.partial(run, True))


def _win_attn(qn, kwn, vwa, p, prev, gq, gk, tq):
    s = qn.shape[0]
    gw = NSA_GROUP * NSA_HEAD_DIM
    span = min(s, WINDOW + tq)
    gain = pl.BlockSpec((1, NSA_HEAD_DIM), lambda h, i: (0, 0))
    return pl.pallas_call(
        functools.partial(_win_attn_kernel, span=span),
        grid=(NSA_KV_HEADS, s // tq),
        in_specs=[pl.BlockSpec((tq, gw), lambda h, i: (i, h)),
                  pl.BlockSpec((s, NSA_HEAD_DIM), lambda h, i: (0, h)),
                  pl.BlockSpec((s, 2 * NSA_HEAD_DIM), lambda h, i: (0, h)),
                  pl.BlockSpec((tq, LANES), lambda h, i: (i, C_MISC // LANES + h)),
                  pl.BlockSpec((tq, gw), lambda h, i: (i, h)),
                  gain, gain],
        out_specs=pl.BlockSpec((tq, gw), lambda h, i: (i, h)),
        out_shape=jax.ShapeDtypeStruct((s, NSA_QW), F32),
        compiler_params=_params(("arbitrary", "arbitrary")),
        name="win_attn",
    )(qn, kwn, vwa, p, prev, gq, gk)


def _sel_attn_kernel(q_ref, bias_ref, ka_ref, va_ref, misc_ref, prev_ref, gq_ref, gk_ref, o_ref,
                     lhs_ref, m_ref, acc_ref, sa_ref, sb_ref, *, tk):
    tq = q_ref.shape[0]
    dh = NSA_HEAD_DIM
    n_super = lhs_ref.shape[0]
    tiles_per_super = LANES * SEL_BLOCK // tk
    s0 = pl.program_id(1) * tq
    fixed, shift = _softmax_shift(gq_ref, gk_ref)
    for sup in range(n_super):
        for g in range(NSA_GROUP):
            lhs_ref[sup, g * tq:(g + 1) * tq, 0:dh] = q_ref[:, g * dh:(g + 1) * dh]
            lhs_ref[sup, g * tq:(g + 1) * tq, dh:2 * dh] = (
                bias_ref[:, sup * LANES:(sup + 1) * LANES].astype(F32) - shift).astype(BF16)
    m_ref[...] = jnp.full_like(m_ref, -jnp.inf)
    acc_ref[...] = jnp.zeros_like(acc_ref)

    def scores(kt, sc_ref):
        k0 = pl.multiple_of(kt * tk, tk)
        k = ka_ref[pl.ds(k0, tk), :]
        sup = kt // tiles_per_super
        for g in range(NSA_GROUP):
            rows = slice(g * tq, (g + 1) * tq)
            sc_ref[rows, :] = _nt_dot(lhs_ref[sup, rows, :], k)

    def accumulate(online, kt, sc_ref, diagonal):
        k0 = pl.multiple_of(kt * tk, tk)
        v = va_ref[pl.ds(k0, tk), :]
        for g in range(NSA_GROUP):
            rows = slice(g * tq, (g + 1) * tq)
            sc = sc_ref[rows, :]
            if diagonal:
                row = lax.broadcasted_iota(jnp.int32, sc.shape, 0)
                col = lax.broadcasted_iota(jnp.int32, sc.shape, 1)
                sc = jnp.where(k0 + col <= s0 + row, sc, NEG)
            if online:
                m_old = m_ref[rows, :]
                m_new = jnp.maximum(m_old, jnp.max(sc, axis=-1, keepdims=True))
                alpha = jnp.exp(m_old - m_new)
                pr = jnp.exp(sc - jnp.tile(m_new, (1, tk // LANES)))
                acc_ref[rows, :] = (jnp.tile(alpha, (1, 2)) * acc_ref[rows, :]
                                    + jnp.dot(pr.astype(BF16), v, preferred_element_type=F32))
                m_ref[rows, :] = m_new
            else:
                acc_ref[rows, :] += jnp.dot(jnp.exp(sc).astype(BF16), v, preferred_element_type=F32)

    n_full = s0 // tk

    def run(online):
        acc_fn = functools.partial(accumulate, online)

        def body(j, carry):
            scores(2 * j + 1, sb_ref)
            acc_fn(2 * j, sa_ref, False)
            scores(2 * j + 2, sa_ref)
            acc_fn(2 * j + 1, sb_ref, False)
            return carry

        scores(0, sa_ref)
        lax.fori_loop(0, n_full // 2, body, 0)

        @pl.when(n_full % 2 == 0)
        def _():
            acc_fn(n_full, sa_ref, True)

        @pl.when(n_full % 2 == 1)
        def _():
            scores(n_full, sb_ref)
            acc_fn(n_full - 1, sa_ref, False)
            acc_fn(n_full, sb_ref, True)

    pl.when(fixed)(functools.partial(run, False))
    pl.when(jnp.logical_not(fixed))(functools.partial(run, True))

    acc = acc_ref[...]
    o = acc[:, :dh] * (_branch_gate(misc_ref[...], 1, tq) / acc[:, dh:])
    o_ref[...] = prev_ref[...] + _unstack_heads(o, tq)


def _sel_attn(qn, bias, ksa, vsa, p, prev, gq, gk, tq, tk):
    s = qn.shape[0]
    dh = NSA_HEAD_DIM
    gw = NSA_GROUP * dh
    nbp = bias.shape[1] // NSA_KV_HEADS
    n_super = nbp // LANES
    assert tk % tq == 0 and (LANES * SEL_BLOCK) % tk == 0 and s % tk == 0
    resident = lambda: pl.BlockSpec((s, 2 * dh), lambda h, i: (0, h), pipeline_mode=pl.Buffered(1))
    gain = pl.BlockSpec((1, dh), lambda h, i: (0, 0))
    return pl.pallas_call(
        functools.partial(_sel_attn_kernel, tk=tk),
        grid=(NSA_KV_HEADS, s // tq),
        in_specs=[pl.BlockSpec((tq, gw), lambda h, i: (i, h)),
                  pl.BlockSpec((tq, nbp), lambda h, i: (i, h)),
                  resident(), resident(),
                  pl.BlockSpec((tq, LANES), lambda h, i: (i, C_MISC // LANES + h)),
                  pl.BlockSpec((tq, gw), lambda h, i: (i, h)),
                  gain, gain],
        out_specs=pl.BlockSpec((tq, gw), lambda h, i: (i, h)),
        out_shape=jax.ShapeDtypeStruct((s, NSA_QW), F32),
        scratch_shapes=[pltpu.VMEM((n_super, NSA_GROUP * tq, 2 * dh), BF16),
                        pltpu.VMEM((NSA_GROUP * tq, LANES), F32),
                        pltpu.VMEM((NSA_GROUP * tq, 2 * dh), F32),
                        pltpu.VMEM((NSA_GROUP * tq, tk), F32),
                        pltpu.VMEM((NSA_GROUP * tq, tk), F32)],
        compiler_params=_params(("arbitrary", "arbitrary")),
        name="sel_attn",
    )(qn, bias, ksa, vsa, p, prev, gq, gk)


def _out_proj_kernel(x_ref, a_ref, b_ref, wa_ref, wb_ref, o_ref):
    o_ref[...] = (x_ref[...]
                  + jnp.dot(a_ref[...].astype(BF16), wa_ref[...], preferred_element_type=F32)
                  + jnp.dot(b_ref[...].astype(BF16), wb_ref[...], preferred_element_type=F32))


def _out_proj(x, a, b, w, tm):
    s, d = x.shape
    ka, kb = a.shape[1], b.shape[1]
    assert ka == kb
    return pl.pallas_call(
        _out_proj_kernel,
        grid=(s // tm,),
        in_specs=[pl.BlockSpec((tm, d), lambda i: (i, 0)),
                  pl.BlockSpec((tm, ka), lambda i: (i, 0)),
                  pl.BlockSpec((tm, kb), lambda i: (i, 0)),
                  pl.BlockSpec((ka, d), lambda i: (0, 0)),
                  pl.BlockSpec((kb, d), lambda i: (1, 0))],
        out_specs=pl.BlockSpec((tm, d), lambda i: (i, 0)),
        out_shape=jax.ShapeDtypeStruct((s, d), F32),
        compiler_params=_params(("arbitrary",)),
        name="out_proj",
    )(x, a, b, w, w)


def _ffn_kernel(h_ref, g_ref, wg_ref, wu_ref, wd_ref, o_ref, hn_ref):
    @pl.when(pl.program_id(1) == 0)
    def _():
        h = h_ref[...]
        hn_ref[...] = _rms(h, g_ref[...]).astype(BF16)
        o_ref[...] = h

    hn = hn_ref[...]
    a = jnp.dot(hn, wg_ref[...].astype(BF16), preferred_element_type=F32)
    u = jnp.dot(hn, wu_ref[...].astype(BF16), preferred_element_type=F32)
    z = (a * _sigmoid(a) * u).astype(BF16)
    o_ref[...] += jnp.dot(z, wd_ref[...].astype(BF16), preferred_element_type=F32)


def _ffn(h, g, wg, wu, wd, tm, tf):
    s, d = h.shape
    f = wg.shape[1]
    return pl.pallas_call(
        _ffn_kernel,
        grid=(s // tm, f // tf),
        in_specs=[pl.BlockSpec((tm, d), lambda i, j: (i, 0)),
                  pl.BlockSpec((1, d), lambda i, j: (0, 0)),
                  pl.BlockSpec((d, tf), lambda i, j: (0, j)),
                  pl.BlockSpec((d, tf), lambda i, j: (0, j)),
                  pl.BlockSpec((tf, d), lambda i, j: (j, 0))],
        out_specs=pl.BlockSpec((tm, d), lambda i, j: (i, 0)),
        out_shape=jax.ShapeDtypeStruct((s, d), F32),
        scratch_shapes=[pltpu.VMEM((tm, d), BF16)],
        compiler_params=_params(("arbitrary", "arbitrary")),
        name="ffn",
    )(h, g, wg, wu, wd)


_W_IN_LR = GLA_QKV
_W_IN_GO = _W_IN_LR + GLA_GATE_RANK
_W_IN_NG = _W_IN_GO + GLA_VW + NSA_QW + 6 * NSA_KVW


def _regroup_kernel(w_ref, misc_ref, o_ref):
    o_ref[0:GLA_QKV, :] = w_ref[0:GLA_QKV, :].astype(BF16)
    o_ref[GLA_QKV:C_MISC, :] = w_ref[_W_IN_GO:_W_IN_NG, :].astype(BF16)
    o_ref[C_MISC:PROJ_PAD, :] = misc_ref[...]


def _regroup_w_in(w_in, tile=256):
    d = w_in.shape[0]
    wt = w_in.T
    lr = wt[_W_IN_LR:_W_IN_GO]
    ng = wt[_W_IN_NG:_W_IN_NG + 3 * NSA_HEADS].reshape(NSA_KV_HEADS, NSA_GROUP, 3, d)
    ng = ng.transpose(0, 2, 1, 3).reshape(NSA_KV_HEADS, 3 * NSA_GROUP, d)
    pad = lambda n: jnp.zeros((n, d), w_in.dtype)
    tail = pad(LANES - GLA_GATE_RANK - 3 * NSA_GROUP)
    misc = jnp.concatenate([lr, ng[0], tail, pad(GLA_GATE_RANK), ng[1], tail], axis=0).astype(BF16)
    return pl.pallas_call(
        _regroup_kernel,
        grid=(d // tile,),
        in_specs=[pl.BlockSpec((wt.shape[0], tile), lambda i: (0, i)),
                  pl.BlockSpec((2 * LANES, tile), lambda i: (0, i))],
        out_specs=pl.BlockSpec((PROJ_PAD, tile), lambda i: (0, i)),
        out_shape=jax.ShapeDtypeStruct((PROJ_PAD, d), BF16),
        compiler_params=_params(("arbitrary",)),
        name="regroup_w_in",
    )(wt, misc)


def _overlap_t(s, nbp):
    n = s // CMP_STRIDE
    c0 = np.arange(n)[None, :] * CMP_STRIDE
    s0 = np.arange(nbp)[:, None] * SEL_BLOCK
    ov = np.clip(np.minimum(c0 + CMP_BLOCK, s0 + SEL_BLOCK) - np.maximum(c0, s0), 0, None) / CMP_STRIDE
    ov[:, n - 1] = 0.0
    ov[s // SEL_BLOCK:] = 0.0
    return jnp.asarray(ov, BF16)


def _layer(x, attn_norm_g, w_in, gla_conv_w, gla_gate_w2, gla_gate_b, gla_norm_g,
           nsa_q_norm_g, nsa_kc_norm_g, nsa_ks_norm_g, nsa_kw_norm_g,
           cmp_k_pos, cmp_k_w1, cmp_k_w2, cmp_v_pos, cmp_v_w1, cmp_v_w2,
           w_out, ffn_norm_g, w_gate, w_up, w_down):
    s = x.shape[0]
    dh = NSA_HEAD_DIM
    row = lambda v: v.reshape(1, -1)
    big = s >= 4096
    tm = 1024 if big else 256

    p = _norm_matmul(x, row(attn_norm_g), _regroup_w_in(w_in), 512 if big else 256,
                     PROJ_PAD // 2 if big else LANES)

    gla_out = _gla(p, gla_conv_w, gla_gate_w2, row(gla_gate_b), row(gla_norm_g), 1024 if big else 512)

    qn, ksa, vsa, kwn, vwa = _nsa_prep(p, row(nsa_q_norm_g), row(nsa_ks_norm_g), row(nsa_kw_norm_g), 512)

    w1 = jnp.stack([cmp_k_w1, cmp_v_w1])
    w1s = (w1.reshape(2, 2, CMP_STRIDE, dh, CMP_HIDDEN).transpose(0, 2, 3, 1, 4)
           .reshape(2, CMP_STRIDE, dh, 2 * CMP_HIDDEN).astype(BF16))
    pos = jnp.stack([cmp_k_pos, cmp_v_pos]).reshape(2, 1, CMP_BLOCK * dh)
    w2 = jnp.stack([cmp_k_w2, cmp_v_w2]).astype(BF16)
    cmp, cmp_a = _compress(p, w1s, pos, w1, w2, row(nsa_kc_norm_g))

    nb = s // SEL_BLOCK
    nbp = -(-nb // LANES) * LANES
    gq = row(nsa_q_norm_g)
    o_cmp, bias = _cmp_attn(qn, cmp, cmp_a, _overlap_t(s, nbp), p, gq, row(nsa_kc_norm_g),
                            256 if big else 128, min(SEL_TOPK, nb))
    o_cw = _win_attn(qn, kwn, vwa, p, o_cmp, gq, row(nsa_kw_norm_g), 256)
    nsa_out = _sel_attn(qn, bias, ksa, vsa, p, o_cw, gq, row(nsa_ks_norm_g),
                        512 if big else 256, 1024 if big else 512)

    h = _out_proj(x, gla_out, nsa_out, w_out.astype(BF16), 512 if big else 256)
    return _ffn(h, row(ffn_norm_g), w_gate, w_up, w_down, 1024 if big else 256, 256)


def kernel(x, attn_norm_g, w_in, gla_conv_w, gla_gate_w2, gla_gate_b, gla_norm_g, nsa_q_norm_g, nsa_kc_norm_g, nsa_ks_norm_g, nsa_kw_norm_g, cmp_k_pos, cmp_k_w1, cmp_k_w2, cmp_v_pos, cmp_v_w1, cmp_v_w2, w_out, ffn_norm_g, w_gate, w_up, w_down):
    assert x.shape[0] == 1 and attn_norm_g.shape[0] == 1
    y = _layer(x[0], attn_norm_g[0], w_in[0], gla_conv_w[0], gla_gate_w2[0], gla_gate_b[0],
               gla_norm_g[0], nsa_q_norm_g[0], nsa_kc_norm_g[0], nsa_ks_norm_g[0], nsa_kw_norm_g[0],
               cmp_k_pos[0], cmp_k_w1[0], cmp_k_w2[0], cmp_v_pos[0], cmp_v_w1[0], cmp_v_w2[0],
               w_out[0], ffn_norm_g[0], w_gate[0], w_up[0], w_down[0])
    return y[None]
```

```python
import functools
from typing import NamedTuple

import jax
import jax.numpy as jnp
import numpy as np
from jax import lax
from jax.experimental import pallas as pl
from jax.experimental.pallas import tpu as pltpu

D_MODEL = 2048
GLA_HEADS = 4
GLA_DK = 128
GLA_DV = 256
GLA_GATE_RANK = 16
GLA_GATE_TAU = 16.0
GLA_CHUNK = 64
GLA_SUB = 16
GLA_SAFE_LOG_DECAY = 60.0
GLA_HEADS_PER_STEP = 1
SOFTMAX_SHIFT_LIMIT = 40.0
CONV_WIDTH = 4

NSA_HEADS = 8
NSA_KV_HEADS = 2
NSA_GROUP = 4
NSA_HEAD_DIM = 128
CMP_BLOCK = 32
CMP_STRIDE = 16
CMP_HIDDEN = 128
SEL_BLOCK = 64
SEL_TOPK = 16
WINDOW = 512
D_FF = 5632
EPS = 1e-6
NEG = -1e30

GLA_QK = GLA_HEADS * GLA_DK
GLA_VW = GLA_HEADS * GLA_DV
GLA_QKV = 2 * GLA_QK + GLA_VW
NSA_QW = NSA_HEADS * NSA_HEAD_DIM
NSA_KVW = NSA_KV_HEADS * NSA_HEAD_DIM

LANES = 128
VMEM_LIMIT = 56 * 1024 * 1024

C_GQ = 0
C_GK = GLA_QK
C_GV = 2 * GLA_QK
C_GO = GLA_QKV
C_NQ = C_GO + GLA_VW
C_KC = C_NQ + NSA_QW
C_VC = C_KC + NSA_KVW
C_KS = C_VC + NSA_KVW
C_VS = C_KS + NSA_KVW
C_KW = C_VS + NSA_KVW
C_VW = C_KW + NSA_KVW
C_MISC = C_VW + NSA_KVW
PROJ_PAD = C_MISC + 2 * LANES
GATE_LANE = GLA_GATE_RANK

F32 = jnp.float32
BF16 = jnp.bfloat16


def _params(sem):
    return pltpu.CompilerParams(dimension_semantics=sem, vmem_limit_bytes=VMEM_LIMIT)


def _nt_dot(a, b):
    return lax.dot_general(a, b, (((1,), (1,)), ((), ())), preferred_element_type=F32)


def _sigmoid(x):
    return 1.0 / (1.0 + jnp.exp(-x))


def _rms(x, g):
    return x * lax.rsqrt(jnp.mean(x * x, axis=-1, keepdims=True) + EPS) * g


def _norm_matmul_kernel(x_ref, g_ref, w_ref, o_ref, xn_ref):
    @pl.when(pl.program_id(1) == 0)
    def _():
        xn_ref[...] = _rms(x_ref[...], g_ref[...]).astype(BF16)

    o_ref[...] = _nt_dot(xn_ref[...], w_ref[...])


def _norm_matmul(x, g, wt, tm, tn):
    s, d = x.shape
    n = wt.shape[0]
    return pl.pallas_call(
        _norm_matmul_kernel,
        grid=(s // tm, n // tn),
        in_specs=[pl.BlockSpec((tm, d), lambda i, j: (i, 0)),
                  pl.BlockSpec((1, d), lambda i, j: (0, 0)),
                  pl.BlockSpec((tn, d), lambda i, j: (j, 0))],
        out_specs=pl.BlockSpec((tm, tn), lambda i, j: (i, j)),
        out_shape=jax.ShapeDtypeStruct((s, n), F32),
        scratch_shapes=[pltpu.VMEM((tm, d), BF16)],
        compiler_params=_params(("arbitrary", "arbitrary")),
        name="in_proj",
    )(x, g, wt)


def _gla_kernel(q_ref, qp_ref, k_ref, kp_ref, v_ref, vp_ref, lr_ref, go_ref,
                cwq_ref, cwk_ref, cwv_ref, w2_ref, gb_ref, ng_ref, o_ref,
                st_ref, sq_ref, sk_ref, sv_ref):
    t_rows = q_ref.shape[0]
    pad = GLA_SUB
    first = pl.program_id(1) == 0

    @pl.when(first)
    def _():
        st_ref[...] = jnp.zeros_like(st_ref)

    def cols(ref, hh, width):
        return ref.at[:, hh * width:(hh + 1) * width]

    def conv_silu(u_ref, p_ref, w_ref, s_ref):
        w = w_ref[...]
        s_ref[0:8, :] = jnp.where(first, 0.0, p_ref[...])
        s_ref[8:16, :] = u_ref[0:8, :]
        head = u_ref[0:8, :] * w[CONV_WIDTH - 1:CONV_WIDTH]
        body = u_ref[8:, :] * w[CONV_WIDTH - 1:CONV_WIDTH]
        for d in range(1, CONV_WIDTH):
            wd = w[CONV_WIDTH - 1 - d:CONV_WIDTH - d]
            head = head + s_ref[pl.ds(8 - d, 8), :] * wd
            body = body + u_ref[pl.ds(8 - d, t_rows - 8), :] * wd
        acc = jnp.concatenate([head, body], axis=0)
        return acc * _sigmoid(acc)

    row = lax.broadcasted_iota(jnp.int32, (t_rows, GLA_DK), 0)
    lane = lax.broadcasted_iota(jnp.int32, (t_rows, GLA_DK), 1)
    rc = row & (GLA_CHUNK - 1)

    def front(hh):
        dk, dv = GLA_DK, GLA_DV
        q = conv_silu(cols(q_ref, hh, dk), cols(qp_ref, hh, dk), cols(cwq_ref, hh, dk), sq_ref.at[hh])
        k = conv_silu(cols(k_ref, hh, dk), cols(kp_ref, hh, dk), cols(cwk_ref, hh, dk), sk_ref.at[hh])
        v = conv_silu(cols(v_ref, hh, dv), cols(vp_ref, hh, dv), cols(cwv_ref, hh, dv), sv_ref.at[hh])
        z = jnp.dot(lr_ref[:, :GLA_GATE_RANK], w2_ref[:, hh * dk:(hh + 1) * dk], preferred_element_type=F32,
                    precision=lax.Precision.HIGHEST) + gb_ref[:, hh * dk:(hh + 1) * dk]
        b = (jnp.minimum(z, 0.0) - jnp.log(1.0 + jnp.exp(-jnp.abs(z)))) * (1.0 / GLA_GATE_TAU)
        d = 1
        while d < GLA_CHUNK:
            b = b + jnp.where(rc >= d, pltpu.roll(b, d, 0), 0.0)
            d *= 2
        return q * (dk ** -0.5), k, v, b

    heads = [front(hh) for hh in range(GLA_HEADS_PER_STEP)]

    srow = lax.broadcasted_iota(jnp.int32, (GLA_CHUNK, GLA_DK), 0)
    scol = lax.broadcasted_iota(jnp.int32, (GLA_CHUNK, GLA_DK), 1)
    zk = jnp.zeros((LANES - GLA_CHUNK, GLA_DK), BF16)
    zv = jnp.zeros((LANES - GLA_CHUNK, GLA_DV), BF16)
    za = jnp.zeros((GLA_SUB, LANES), F32)
    gain = ng_ref[...]

    def chunk_loop(intra_scores):
        for c in range(t_rows // GLA_CHUNK):
            lo = c * GLA_CHUNK
            for hh, (q, k, v, b) in enumerate(heads):
                bc = b[lo:lo + GLA_CHUNK]
                qc = q[lo:lo + GLA_CHUNK]
                kc = k[lo:lo + GLA_CHUNK]
                vc = v[lo:lo + GLA_CHUNK].astype(BF16)
                st = st_ref[hh]
                b_last = bc[GLA_CHUNK - 1:GLA_CHUNK]
                qd = (qc * jnp.exp(bc)).astype(BF16)
                scores = intra_scores(hh, lo, bc, qc, kc, qd)
                o = _nt_dot(qd, st.astype(BF16)) + jnp.dot(
                    scores.astype(BF16), jnp.concatenate([vc, zv], axis=0), preferred_element_type=F32)
                go = go_ref[lo:lo + GLA_CHUNK, hh * GLA_DV:(hh + 1) * GLA_DV]
                o_ref[lo:lo + GLA_CHUNK, hh * GLA_DV:(hh + 1) * GLA_DV] = _rms(o, gain) * (go * _sigmoid(go))
                kt = (kc * jnp.exp(b_last - bc)).astype(BF16)
                upd = lax.dot_general(vc, kt, (((0,), (0,)), ((), ())), preferred_element_type=F32)
                st_ref[hh] = st * jnp.exp(b_last) + upd

    b_min = heads[0][3]
    for hd in heads[1:]:
        b_min = jnp.minimum(b_min, hd[3])
    small_decay = jnp.min(b_min) >= -GLA_SAFE_LOG_DECAY

    @pl.when(small_decay)
    def _():
        def intra_scores(hh, lo, bc, qc, kc, qd):
            kd = (kc * jnp.exp(-bc)).astype(BF16)
            a = _nt_dot(qd, jnp.concatenate([kd, zk], axis=0))
            return jnp.where(scol <= srow, a, 0.0)

        chunk_loop(intra_scores)

    @pl.when(jnp.logical_not(small_decay))
    def _():
        zero_pad = jnp.zeros((pad, GLA_DK), F32)
        rs = row & (GLA_SUB - 1)
        dl = rc - lane
        bands = []
        for hh, (q, k, v, b) in enumerate(heads):
            sq_ref[hh, 0:pad, :] = zero_pad
            sk_ref[hh, 0:pad, :] = zero_pad
            sq_ref[hh, pad:, :] = b
            sk_ref[hh, pad:, :] = k
            band = jnp.where(dl == 0, jnp.sum(q * k, axis=-1, keepdims=True), 0.0)
            for delta in range(1, GLA_SUB):
                e = jnp.exp(jnp.where(rs >= delta, b - sq_ref[hh, pl.ds(pad - delta, t_rows), :], -jnp.inf))
                sc = jnp.sum(q * sk_ref[hh, pl.ds(pad - delta, t_rows), :] * e, axis=-1, keepdims=True)
                band = jnp.where(dl == delta, sc, band)
            bands.append(band)

        def intra_scores(hh, lo, bc, qc, kc, qd):
            blocks = [za]
            for sub in range(1, GLA_CHUNK // GLA_SUB):
                r0 = sub * GLA_SUB
                ref_b = bc[r0:r0 + 1]
                qq = (qc[r0:r0 + GLA_SUB] * jnp.exp(bc[r0:r0 + GLA_SUB] - ref_b)).astype(BF16)
                kk = (kc * jnp.exp(jnp.where(srow < r0, ref_b - bc, -jnp.inf))).astype(BF16)
                blocks.append(_nt_dot(qq, jnp.concatenate([kk, zk], axis=0)))
            return bands[hh][lo:lo + GLA_CHUNK] + jnp.concatenate(blocks, axis=0)

        chunk_loop(intra_scores)


def _gla(p, conv_w, w2, gb, ng, tile):
    s = p.shape[0]
    t8 = tile // 8
    nh = GLA_HEADS_PER_STEP
    dk, dv = nh * GLA_DK, nh * GLA_DV

    def prev(col):
        return lambda h, i: (jnp.maximum(i * t8 - 1, 0), col(h))

    qcol = lambda h: C_GQ // dk + h
    kcol = lambda h: C_GK // dk + h
    vcol = lambda h: C_GV // dv + h
    return pl.pallas_call(
        _gla_kernel,
        grid=(GLA_HEADS // nh, s // tile),
        in_specs=[
            pl.BlockSpec((tile, dk), lambda h, i: (i, qcol(h))),
            pl.BlockSpec((8, dk), prev(qcol)),
            pl.BlockSpec((tile, dk), lambda h, i: (i, kcol(h))),
            pl.BlockSpec((8, dk), prev(kcol)),
            pl.BlockSpec((tile, dv), lambda h, i: (i, vcol(h))),
            pl.BlockSpec((8, dv), prev(vcol)),
            pl.BlockSpec((tile, LANES), lambda h, i: (i, C_MISC // LANES)),
            pl.BlockSpec((tile, dv), lambda h, i: (i, C_GO // dv + h)),
            pl.BlockSpec((CONV_WIDTH, dk), lambda h, i: (0, qcol(h))),
            pl.BlockSpec((CONV_WIDTH, dk), lambda h, i: (0, kcol(h))),
            pl.BlockSpec((CONV_WIDTH, dv), lambda h, i: (0, vcol(h))),
            pl.BlockSpec((GLA_GATE_RANK, dk), lambda h, i: (0, h)),
            pl.BlockSpec((1, dk), lambda h, i: (0, h)),
            pl.BlockSpec((1, GLA_DV), lambda h, i: (0, 0)),
        ],
        out_specs=pl.BlockSpec((tile, dv), lambda h, i: (i, h)),
        out_shape=jax.ShapeDtypeStruct((s, GLA_VW), F32),
        scratch_shapes=[pltpu.VMEM((nh, GLA_DV, GLA_DK), F32),
                        pltpu.VMEM((nh, tile + GLA_SUB, GLA_DK), F32),
                        pltpu.VMEM((nh, tile + GLA_SUB, GLA_DK), F32),
                        pltpu.VMEM((nh, GLA_SUB, GLA_DV), F32)],
        compiler_params=_params(("arbitrary", "arbitrary")),
        name="gla",
    )(p, p, p, p, p, p, p, p, conv_w, conv_w, conv_w, w2, gb, ng)


def _nsa_prep_kernel(q_ref, ks_ref, vs_ref, kw_ref, vw_ref, qg_ref, ksg_ref, kwg_ref,
                     qn_ref, ksa_ref, vsa_ref, kwn_ref, vwa_ref):
    t_rows = q_ref.shape[0]
    dh = NSA_HEAD_DIM
    scale = dh ** -0.5
    ones = jnp.ones((t_rows, dh), BF16)
    for h in range(NSA_HEADS):
        sl = slice(h * dh, (h + 1) * dh)
        qn_ref[:, sl] = (_rms(q_ref[:, sl], qg_ref[...]) * scale).astype(BF16)
    pos = pl.program_id(0) * t_rows + lax.broadcasted_iota(jnp.int32, (t_rows, LANES), 0)
    lane = lax.broadcasted_iota(jnp.int32, (t_rows, LANES), 1)
    onehot = jnp.where(lane == ((pos // SEL_BLOCK) & (LANES - 1)), 1.0, 0.0).astype(BF16)
    for h in range(NSA_KV_HEADS):
        sl = slice(h * dh, (h + 1) * dh)
        ksa_ref[:, 2 * h * dh:(2 * h + 1) * dh] = _rms(ks_ref[:, sl], ksg_ref[...]).astype(BF16)
        ksa_ref[:, (2 * h + 1) * dh:(2 * h + 2) * dh] = onehot
        kwn_ref[:, sl] = _rms(kw_ref[:, sl], kwg_ref[...]).astype(BF16)
        vsa_ref[:, 2 * h * dh:(2 * h + 1) * dh] = vs_ref[:, sl].astype(BF16)
        vsa_ref[:, (2 * h + 1) * dh:(2 * h + 2) * dh] = ones
        vwa_ref[:, 2 * h * dh:(2 * h + 1) * dh] = vw_ref[:, sl].astype(BF16)
        vwa_ref[:, (2 * h + 1) * dh:(2 * h + 2) * dh] = ones


def _nsa_prep(p, qg, ksg, kwg, tile):
    s = p.shape[0]
    kv = NSA_KVW
    col = lambda c, w: (lambda i: (i, c // w))
    row = lambda w: pl.BlockSpec((tile, w), lambda i: (i, 0))
    gain = pl.BlockSpec((1, NSA_HEAD_DIM), lambda i: (0, 0))
    return pl.pallas_call(
        _nsa_prep_kernel,
        grid=(s // tile,),
        in_specs=[pl.BlockSpec((tile, NSA_QW), col(C_NQ, NSA_QW)),
                  pl.BlockSpec((tile, kv), col(C_KS, kv)),
                  pl.BlockSpec((tile, kv), col(C_VS, kv)),
                  pl.BlockSpec((tile, kv), col(C_KW, kv)),
                  pl.BlockSpec((tile, kv), col(C_VW, kv)),
                  gain, gain, gain],
        out_specs=[row(NSA_QW), row(2 * kv), row(2 * kv), row(kv), row(2 * kv)],
        out_shape=[jax.ShapeDtypeStruct((s, NSA_QW), BF16),
                   jax.ShapeDtypeStruct((s, 2 * kv), BF16),
                   jax.ShapeDtypeStruct((s, 2 * kv), BF16),
                   jax.ShapeDtypeStruct((s, kv), BF16),
                   jax.ShapeDtypeStruct((s, 2 * kv), BF16)],
        compiler_params=_params(("arbitrary",)),
        name="nsa_prep",
    )(p, p, p, p, p, qg, ksg, kwg)


def _compress_kernel(u_ref, w1_ref, pos_ref, w1f_ref, w2_ref, g_ref, o_ref, oa_ref):
    n = o_ref.shape[1]
    half = CMP_BLOCK // CMP_STRIDE
    assert half == 2
    acc = jnp.zeros((n, 2 * CMP_HIDDEN), F32)
    for l in range(CMP_STRIDE):
        x = u_ref[pl.ds(l, n, stride=CMP_STRIDE), :].astype(BF16)
        acc = acc + jnp.dot(x, w1_ref[0, l], preferred_element_type=F32)
    posb = jnp.dot(jnp.broadcast_to(pos_ref[0], (8, pos_ref.shape[2])), w1f_ref[0],
                   preferred_element_type=F32, precision=lax.Precision.HIGHEST)[0:1]
    hid = acc[:, :CMP_HIDDEN] + pltpu.roll(acc[:, CMP_HIDDEN:], n - 1, 0) + posb
    hid = hid * _sigmoid(hid)
    out = jnp.dot(hid.astype(BF16), w2_ref[0], preferred_element_type=F32)
    is_k = pl.program_id(0) < NSA_KV_HEADS
    out = jnp.where(is_k, _rms(out, g_ref[...]), out)
    row = lax.broadcasted_iota(jnp.int32, out.shape, 0)
    out = jnp.where(row < n - 1, out, 0.0).astype(BF16)
    o_ref[0] = out
    oa_ref[0] = jnp.concatenate([out, jnp.ones_like(out)], axis=1)


def _compress(p, w1s, pos, w1f, w2, g):
    s = p.shape[0]
    n = s // CMP_STRIDE
    dh = NSA_HEAD_DIM
    return pl.pallas_call(
        _compress_kernel,
        grid=(2 * NSA_KV_HEADS,),
        in_specs=[pl.BlockSpec((s, dh), lambda j: (0, C_KC // dh + j)),
                  pl.BlockSpec((1, CMP_STRIDE, dh, 2 * CMP_HIDDEN), lambda j: (j // 2, 0, 0, 0)),
                  pl.BlockSpec((1, 1, CMP_BLOCK * dh), lambda j: (j // 2, 0, 0)),
                  pl.BlockSpec((1, CMP_BLOCK * dh, CMP_HIDDEN), lambda j: (j // 2, 0, 0)),
                  pl.BlockSpec((1, CMP_HIDDEN, dh), lambda j: (j // 2, 0, 0)),
                  pl.BlockSpec((1, dh), lambda j: (0, 0))],
        out_specs=[pl.BlockSpec((1, n, dh), lambda j: (j, 0, 0)),
                   pl.BlockSpec((1, n, 2 * dh), lambda j: (j, 0, 0))],
        out_shape=[jax.ShapeDtypeStruct((2 * NSA_KV_HEADS, n, dh), BF16),
                   jax.ShapeDtypeStruct((2 * NSA_KV_HEADS, n, 2 * dh), BF16)],
        compiler_params=_params(("arbitrary",)),
        name="compress",
    )(p, w1s, pos, w1f, w2, g)


def _softmax_shift(gq_ref, gk_ref):
    bound = (NSA_HEAD_DIM ** 0.5) * jnp.max(jnp.abs(gq_ref[...])) * jnp.max(jnp.abs(gk_ref[...]))
    usable = bound <= SOFTMAX_SHIFT_LIMIT
    return usable, jnp.where(usable, bound, 0.0)


def _unstack_heads(o, tq):
    return jnp.concatenate([o[g * tq:(g + 1) * tq] for g in range(NSA_GROUP)], axis=1)


def _branch_gate(misc, branch, tq):
    sg = _sigmoid(misc)
    base = GATE_LANE + branch * NSA_GROUP
    return jnp.concatenate([sg[:, base + g:base + g + 1] for g in range(NSA_GROUP)], axis=0)


def _cmp_attn_kernel(q_ref, kc_ref, va_ref, ovl_ref, misc_ref, gq_ref, gk_ref, gks_ref, o_ref, bias_ref,
                     e_ref, *, n_sel):
    tq = q_ref.shape[0]
    dh = NSA_HEAD_DIM
    nc = kc_ref.shape[1]
    nbp = ovl_ref.shape[0]
    i = pl.program_id(0)
    s0 = i * tq
    col_step = max(LANES, nc // 4)
    n_var = nc // col_step
    tiles_per_var = (nc * CMP_STRIDE // tq) // n_var
    assert tiles_per_var * tq == col_step * CMP_STRIDE

    fixed, shift = _softmax_shift(gq_ref, gk_ref)
    _, sel_shift = _softmax_shift(gq_ref, gks_ref)

    def variant(ncv, nbv, online):
        q = q_ref[...]
        row = lax.broadcasted_iota(jnp.int32, (tq, ncv), 0)
        col = lax.broadcasted_iota(jnp.int32, (tq, ncv), 1)
        maskb = jnp.where(col * CMP_STRIDE + (CMP_BLOCK - 1) <= s0 + row, -shift, NEG)
        has_valid = (s0 + lax.broadcasted_iota(jnp.int32, (tq, 1), 0)) >= CMP_BLOCK - 1
        sg = _sigmoid(misc_ref[...])
        ovl = ovl_ref[0:nbv, 0:ncv]
        imps = []
        for h in range(NSA_KV_HEADS):
            kc = kc_ref[h, 0:ncv, :]
            va = va_ref[h, 0:ncv, :]
            invs = []
            for g in range(NSA_GROUP):
                hd = h * NSA_GROUP + g
                sm = _nt_dot(q[:, hd * dh:(hd + 1) * dh], kc) + maskb
                if online:
                    sm = sm - jnp.max(sm, axis=-1, keepdims=True)
                e = jnp.exp(sm)
                e_ref[hd, :, 0:ncv] = e
                acc = jnp.dot(e.astype(BF16), va, preferred_element_type=F32)
                inv = jnp.where(has_valid, 1.0 / acc[:, dh:], 0.0)
                gl = h * LANES + GATE_LANE + g
                o_ref[:, hd * dh:(hd + 1) * dh] = acc[:, :dh] * inv * sg[:, gl:gl + 1]
                invs.append(inv)
            ps = None
            for g in range(NSA_GROUP):
                pc = e_ref[h * NSA_GROUP + g, :, 0:ncv] * jnp.tile(invs[g], (1, ncv // LANES))
                ps = pc if ps is None else ps + pc
            hi = ps.astype(BF16)
            r1 = ps - hi.astype(F32)
            mid = r1.astype(BF16)
            lo = (r1 - mid.astype(F32)).astype(BF16)
            imps.append(_nt_dot(ovl, hi) + _nt_dot(ovl, mid) + _nt_dot(ovl, lo))

        jb = lax.broadcasted_iota(jnp.int32, (nbv, tq), 0)
        jt = (s0 + lax.broadcasted_iota(jnp.int32, (nbv, tq), 1)) // SEL_BLOCK
        forced = (jb == 0) | (jb == jt) | (jb == jt - 1)
        future = jb > jt
        curs = [jnp.where(future | forced, -jnp.inf, imp) for imp in imps]
        for _ in range(n_sel - 3):
            for h in range(NSA_KV_HEADS):
                mx = jnp.max(curs[h], axis=0, keepdims=True)
                idx = jnp.min(jnp.where(curs[h] == mx, jb, nbv), axis=0, keepdims=True)
                curs[h] = jnp.where(jb == idx, -jnp.inf, curs[h])
        for h in range(NSA_KV_HEADS):
            bias_t = jnp.where((curs[h] == -jnp.inf) & jnp.logical_not(future), -sel_shift, NEG)
            bias_ref[:, h * nbp:h * nbp + nbv] = bias_t.T.astype(BF16)
            if nbv < nbp:
                bias_ref[:, h * nbp + nbv:(h + 1) * nbp] = jnp.full((tq, nbp - nbv), NEG, BF16)

    for v in range(n_var):
        ncv = col_step * (v + 1)
        nbv = min(nbp, -(-(ncv // (SEL_BLOCK // CMP_STRIDE)) // LANES) * LANES)
        pl.when(fixed & (i // tiles_per_var == v))(functools.partial(variant, ncv, nbv, False))
    pl.when(jnp.logical_not(fixed))(functools.partial(variant, nc, nbp, True))


def _cmp_attn(qn, cmp, cmp_a, ovl_t, p, gq, gk, gks, tq, n_sel):
    s = qn.shape[0]
    nc = cmp.shape[1]
    nbp = ovl_t.shape[0]
    assert n_sel >= 3
    misc_w = NSA_KV_HEADS * LANES
    gain = pl.BlockSpec((1, NSA_HEAD_DIM), lambda i: (0, 0))
    return pl.pallas_call(
        functools.partial(_cmp_attn_kernel, n_sel=n_sel),
        grid=(s // tq,),
        in_specs=[pl.BlockSpec((tq, NSA_QW), lambda i: (i, 0)),
                  pl.BlockSpec((NSA_KV_HEADS, nc, NSA_HEAD_DIM), lambda i: (0, 0, 0)),
                  pl.BlockSpec((NSA_KV_HEADS, nc, 2 * NSA_HEAD_DIM), lambda i: (1, 0, 0)),
                  pl.BlockSpec((nbp, nc), lambda i: (0, 0)),
                  pl.BlockSpec((tq, misc_w), lambda i: (i, C_MISC // misc_w)),
                  gain, gain, gain],
        out_specs=[pl.BlockSpec((tq, NSA_QW), lambda i: (i, 0)),
                   pl.BlockSpec((tq, NSA_KV_HEADS * nbp), lambda i: (i, 0))],
        out_shape=[jax.ShapeDtypeStruct((s, NSA_QW), F32),
                   jax.ShapeDtypeStruct((s, NSA_KV_HEADS * nbp), BF16)],
        scratch_shapes=[pltpu.VMEM((NSA_HEADS, tq, nc), F32)],
        compiler_params=_params(("arbitrary",)),
        name="cmp_attn",
    )(qn, cmp, cmp_a, ovl_t, p, gq, gk, gks)


def _win_attn_kernel(q_ref, k_ref, v_ref, misc_ref, prev_ref, gq_ref, gk_ref, o_ref, *, span):
    tq = q_ref.shape[0]
    dh = NSA_HEAD_DIM
    s0 = pl.program_id(1) * tq
    start = pl.multiple_of(jnp.maximum(s0 - WINDOW, 0), tq)
    fixed, shift = _softmax_shift(gq_ref, gk_ref)
    row = lax.broadcasted_iota(jnp.int32, (tq, span), 0)
    col = lax.broadcasted_iota(jnp.int32, (tq, span), 1)
    dist = (s0 - start) + row - col
    maskb = jnp.where((dist >= 0) & (dist < WINDOW), -shift, NEG)

    def run(online):
        q = q_ref[...]
        k = k_ref[pl.ds(start, span), :]
        v = v_ref[pl.ds(start, span), :]
        sg = _sigmoid(misc_ref[...])
        base = GATE_LANE + 2 * NSA_GROUP
        for g in range(NSA_GROUP):
            sm = _nt_dot(q[:, g * dh:(g + 1) * dh], k) + maskb
            if online:
                sm = sm - jnp.max(sm, axis=-1, keepdims=True)
            acc = jnp.dot(jnp.exp(sm).astype(BF16), v, preferred_element_type=F32)
            o = acc[:, :dh] * (sg[:, base + g:base + g + 1] / acc[:, dh:])
            o_ref[:, g * dh:(g + 1) * dh] = prev_ref[:, g * dh:(g + 1) * dh] + o

    pl.when(fixed)(functools.partial(run, False))
    pl.when(jnp.logical_not(fixed))(functools.partial(run, True))


def _win_attn(qn, kwn, vwa, p, prev, gq, gk, tq):
    s = qn.shape[0]
    gw = NSA_GROUP * NSA_HEAD_DIM
    span = min(s, WINDOW + tq)
    gain = pl.BlockSpec((1, NSA_HEAD_DIM), lambda h, i: (0, 0))
    return pl.pallas_call(
        functools.partial(_win_attn_kernel, span=span),
        grid=(NSA_KV_HEADS, s // tq),
        in_specs=[pl.BlockSpec((tq, gw), lambda h, i: (i, h)),
                  pl.BlockSpec((s, NSA_HEAD_DIM), lambda h, i: (0, h)),
                  pl.BlockSpec((s, 2 * NSA_HEAD_DIM), lambda h, i: (0, h)),
                  pl.BlockSpec((tq, LANES), lambda h, i: (i, C_MISC // LANES + h)),
                  pl.BlockSpec((tq, gw), lambda h, i: (i, h)),
                  gain, gain],
        out_specs=pl.BlockSpec((tq, gw), lambda h, i: (i, h)),
        out_shape=jax.ShapeDtypeStruct((s, NSA_QW), F32),
        compiler_params=_params(("arbitrary", "arbitrary")),
        name="win_attn",
    )(qn, kwn, vwa, p, prev, gq, gk)


def _sel_attn_kernel(q_ref, bias_ref, ka_ref, va_ref, misc_ref, prev_ref, gq_ref, gk_ref, o_ref,
                     lhs_ref, m_ref, acc_ref, sa_ref, sb_ref, *, tk):
    tq = q_ref.shape[0]
    dh = NSA_HEAD_DIM
    n_super = lhs_ref.shape[0]
    tiles_per_super = LANES * SEL_BLOCK // tk
    s0 = pl.program_id(1) * tq
    fixed, _ = _softmax_shift(gq_ref, gk_ref)
    for sup in range(n_super):
        for g in range(NSA_GROUP):
            lhs_ref[sup, g * tq:(g + 1) * tq, 0:dh] = q_ref[:, g * dh:(g + 1) * dh]
            lhs_ref[sup, g * tq:(g + 1) * tq, dh:2 * dh] = bias_ref[:, sup * LANES:(sup + 1) * LANES]
    m_ref[...] = jnp.full_like(m_ref, -jnp.inf)
    acc_ref[...] = jnp.zeros_like(acc_ref)

    def scores(kt, sc_ref):
        k0 = pl.multiple_of(kt * tk, tk)
        k = ka_ref[pl.ds(k0, tk), :]
        sup = kt // tiles_per_super
        for g in range(NSA_GROUP):
            rows = slice(g * tq, (g + 1) * tq)
            sc_ref[rows, :] = _nt_dot(lhs_ref[sup, rows, :], k)

    def accumulate(online, kt, sc_ref, diagonal):
        k0 = pl.multiple_of(kt * tk, tk)
        v = va_ref[pl.ds(k0, tk), :]
        for g in range(NSA_GROUP):
            rows = slice(g * tq, (g + 1) * tq)
            sc = sc_ref[rows, :]
            if diagonal:
                row = lax.broadcasted_iota(jnp.int32, sc.shape, 0)
                col = lax.broadcasted_iota(jnp.int32, sc.shape, 1)
                sc = jnp.where(k0 + col <= s0 + row, sc, NEG)
            if online:
                m_old = m_ref[rows, :]
                m_new = jnp.maximum(m_old, jnp.max(sc, axis=-1, keepdims=True))
                alpha = jnp.exp(m_old - m_new)
                pr = jnp.exp(sc - jnp.tile(m_new, (1, tk // LANES)))
                acc_ref[rows, :] = (jnp.tile(alpha, (1, 2)) * acc_ref[rows, :]
                                    + jnp.dot(pr.astype(BF16), v, preferred_element_type=F32))
                m_ref[rows, :] = m_new
            else:
                acc_ref[rows, :] += jnp.dot(jnp.exp(sc).astype(BF16), v, preferred_element_type=F32)

    n_full = s0 // tk

    def run(online):
        acc_fn = functools.partial(accumulate, online)

        def body(j, carry):
            scores(2 * j + 1, sb_ref)
            acc_fn(2 * j, sa_ref, False)
            scores(2 * j + 2, sa_ref)
            acc_fn(2 * j + 1, sb_ref, False)
            return carry

        scores(0, sa_ref)
        lax.fori_loop(0, n_full // 2, body, 0)

        @pl.when(n_full % 2 == 0)
        def _():
            acc_fn(n_full, sa_ref, True)

        @pl.when(n_full % 2 == 1)
        def _():
            scores(n_full, sb_ref)
            acc_fn(n_full - 1, sa_ref, False)
            acc_fn(n_full, sb_ref, True)

    pl.when(fixed)(functools.partial(run, False))
    pl.when(jnp.logical_not(fixed))(functools.partial(run, True))

    acc = acc_ref[...]
    o = acc[:, :dh] * (_branch_gate(misc_ref[...], 1, tq) / acc[:, dh:])
    o_ref[...] = prev_ref[...] + _unstack_heads(o, tq)


def _sel_attn(qn, bias, ksa, vsa, p, prev, gq, gk, tq, tk):
    s = qn.shape[0]
    dh = NSA_HEAD_DIM
    gw = NSA_GROUP * dh
    nbp = bias.shape[1] // NSA_KV_HEADS
    n_super = nbp // LANES
    assert tk % tq == 0 and (LANES * SEL_BLOCK) % tk == 0 and s % tk == 0
    resident = lambda: pl.BlockSpec((s, 2 * dh), lambda h, i: (0, h), pipeline_mode=pl.Buffered(1))
    gain = pl.BlockSpec((1, dh), lambda h, i: (0, 0))
    return pl.pallas_call(
        functools.partial(_sel_attn_kernel, tk=tk),
        grid=(NSA_KV_HEADS, s // tq),
        in_specs=[pl.BlockSpec((tq, gw), lambda h, i: (i, h)),
                  pl.BlockSpec((tq, nbp), lambda h, i: (i, h)),
                  resident(), resident(),
                  pl.BlockSpec((tq, LANES), lambda h, i: (i, C_MISC // LANES + h)),
                  pl.BlockSpec((tq, gw), lambda h, i: (i, h)),
                  gain, gain],
        out_specs=pl.BlockSpec((tq, gw), lambda h, i: (i, h)),
        out_shape=jax.ShapeDtypeStruct((s, NSA_QW), F32),
        scratch_shapes=[pltpu.VMEM((n_super, NSA_GROUP * tq, 2 * dh), BF16),
                        pltpu.VMEM((NSA_GROUP * tq, LANES), F32),
                        pltpu.VMEM((NSA_GROUP * tq, 2 * dh), F32),
                        pltpu.VMEM((NSA_GROUP * tq, tk), F32),
                        pltpu.VMEM((NSA_GROUP * tq, tk), F32)],
        compiler_params=_params(("arbitrary", "arbitrary")),
        name="sel_attn",
    )(qn, bias, ksa, vsa, p, prev, gq, gk)


def _out_proj_kernel(x_ref, a_ref, b_ref, wa_ref, wb_ref, o_ref):
    o_ref[...] = (x_ref[...]
                  + jnp.dot(a_ref[...].astype(BF16), wa_ref[...], preferred_element_type=F32)
                  + jnp.dot(b_ref[...].astype(BF16), wb_ref[...], preferred_element_type=F32))


def _out_proj(x, a, b, w, tm):
    s, d = x.shape
    ka, kb = a.shape[1], b.shape[1]
    assert ka == kb
    return pl.pallas_call(
        _out_proj_kernel,
        grid=(s // tm,),
        in_specs=[pl.BlockSpec((tm, d), lambda i: (i, 0)),
                  pl.BlockSpec((tm, ka), lambda i: (i, 0)),
                  pl.BlockSpec((tm, kb), lambda i: (i, 0)),
                  pl.BlockSpec((ka, d), lambda i: (0, 0)),
                  pl.BlockSpec((kb, d), lambda i: (1, 0))],
        out_specs=pl.BlockSpec((tm, d), lambda i: (i, 0)),
        out_shape=jax.ShapeDtypeStruct((s, d), F32),
        compiler_params=_params(("arbitrary",)),
        name="out_proj",
    )(x, a, b, w, w)


def _ffn_kernel(h_ref, g_ref, wg_ref, wu_ref, wd_ref, o_ref, hn_ref):
    @pl.when(pl.program_id(1) == 0)
    def _():
        h = h_ref[...]
        hn_ref[...] = _rms(h, g_ref[...]).astype(BF16)
        o_ref[...] = h

    hn = hn_ref[...]
    a = jnp.dot(hn, wg_ref[...].astype(BF16), preferred_element_type=F32)
    u = jnp.dot(hn, wu_ref[...].astype(BF16), preferred_element_type=F32)
    z = (a * _sigmoid(a) * u).astype(BF16)
    o_ref[...] += jnp.dot(z, wd_ref[...].astype(BF16), preferred_element_type=F32)


def _ffn(h, g, wg, wu, wd, tm, tf):
    s, d = h.shape
    f = wg.shape[1]
    return pl.pallas_call(
        _ffn_kernel,
        grid=(s // tm, f // tf),
        in_specs=[pl.BlockSpec((tm, d), lambda i, j: (i, 0)),
                  pl.BlockSpec((1, d), lambda i, j: (0, 0)),
                  pl.BlockSpec((d, tf), lambda i, j: (0, j)),
                  pl.BlockSpec((d, tf), lambda i, j: (0, j)),
                  pl.BlockSpec((tf, d), lambda i, j: (j, 0))],
        out_specs=pl.BlockSpec((tm, d), lambda i, j: (i, 0)),
        out_shape=jax.ShapeDtypeStruct((s, d), F32),
        scratch_shapes=[pltpu.VMEM((tm, d), BF16)],
        compiler_params=_params(("arbitrary", "arbitrary")),
        name="ffn",
    )(h, g, wg, wu, wd)


class _Tiles(NamedTuple):
    proj_rows: int
    proj_cols: int
    gla_rows: int
    prep_rows: int
    cmp_q: int
    win_q: int
    sel_q: int
    sel_k: int
    out_rows: int
    ffn_rows: int
    ffn_cols: int


def _tiles(s):
    if s >= 4096:
        return _Tiles(proj_rows=512, proj_cols=PROJ_PAD // 2, gla_rows=1024, prep_rows=512,
                      cmp_q=256, win_q=256, sel_q=512, sel_k=1024, out_rows=512, ffn_rows=1024, ffn_cols=256)
    return _Tiles(proj_rows=256, proj_cols=LANES, gla_rows=512, prep_rows=512,
                  cmp_q=128, win_q=256, sel_q=256, sel_k=512, out_rows=256, ffn_rows=256, ffn_cols=256)


_W_IN_LR = GLA_QKV
_W_IN_GO = _W_IN_LR + GLA_GATE_RANK
_W_IN_NG = _W_IN_GO + GLA_VW + NSA_QW + 6 * NSA_KVW


def _regroup_kernel(w_ref, misc_ref, o_ref):
    o_ref[0:GLA_QKV, :] = w_ref[0:GLA_QKV, :].astype(BF16)
    o_ref[GLA_QKV:C_MISC, :] = w_ref[_W_IN_GO:_W_IN_NG, :].astype(BF16)
    o_ref[C_MISC:PROJ_PAD, :] = misc_ref[...]


def _regroup_w_in(w_in, tile=256):
    d = w_in.shape[0]
    wt = w_in.T
    lr = wt[_W_IN_LR:_W_IN_GO]
    ng = wt[_W_IN_NG:_W_IN_NG + 3 * NSA_HEADS].reshape(NSA_KV_HEADS, NSA_GROUP, 3, d)
    ng = ng.transpose(0, 2, 1, 3).reshape(NSA_KV_HEADS, 3 * NSA_GROUP, d)
    pad = lambda n: jnp.zeros((n, d), w_in.dtype)
    tail = pad(LANES - GLA_GATE_RANK - 3 * NSA_GROUP)
    misc = jnp.concatenate([lr, ng[0], tail, pad(GLA_GATE_RANK), ng[1], tail], axis=0).astype(BF16)
    return pl.pallas_call(
        _regroup_kernel,
        grid=(d // tile,),
        in_specs=[pl.BlockSpec((wt.shape[0], tile), lambda i: (0, i)),
                  pl.BlockSpec((2 * LANES, tile), lambda i: (0, i))],
        out_specs=pl.BlockSpec((PROJ_PAD, tile), lambda i: (0, i)),
        out_shape=jax.ShapeDtypeStruct((PROJ_PAD, d), BF16),
        compiler_params=_params(("arbitrary",)),
        name="regroup_w_in",
    )(wt, misc)


def _overlap_t(s, nbp):
    n = s // CMP_STRIDE
    c0 = np.arange(n)[None, :] * CMP_STRIDE
    s0 = np.arange(nbp)[:, None] * SEL_BLOCK
    ov = np.clip(np.minimum(c0 + CMP_BLOCK, s0 + SEL_BLOCK) - np.maximum(c0, s0), 0, None) / CMP_STRIDE
    ov[:, n - 1] = 0.0
    ov[s // SEL_BLOCK:] = 0.0
    return jnp.asarray(ov, BF16)


def _layer(x, attn_norm_g, w_in, gla_conv_w, gla_gate_w2, gla_gate_b, gla_norm_g,
           nsa_q_norm_g, nsa_kc_norm_g, nsa_ks_norm_g, nsa_kw_norm_g,
           cmp_k_pos, cmp_k_w1, cmp_k_w2, cmp_v_pos, cmp_v_w1, cmp_v_w2,
           w_out, ffn_norm_g, w_gate, w_up, w_down):
    s = x.shape[0]
    dh = NSA_HEAD_DIM
    row = lambda v: v.reshape(1, -1)
    t = _tiles(s)

    p = _norm_matmul(x, row(attn_norm_g), _regroup_w_in(w_in), t.proj_rows, t.proj_cols)

    gla_out = _gla(p, gla_conv_w, gla_gate_w2, row(gla_gate_b), row(gla_norm_g), t.gla_rows)

    qn, ksa, vsa, kwn, vwa = _nsa_prep(p, row(nsa_q_norm_g), row(nsa_ks_norm_g), row(nsa_kw_norm_g),
                                       t.prep_rows)

    w1 = jnp.stack([cmp_k_w1, cmp_v_w1])
    w1s = (w1.reshape(2, 2, CMP_STRIDE, dh, CMP_HIDDEN).transpose(0, 2, 3, 1, 4)
           .reshape(2, CMP_STRIDE, dh, 2 * CMP_HIDDEN).astype(BF16))
    pos = jnp.stack([cmp_k_pos, cmp_v_pos]).reshape(2, 1, CMP_BLOCK * dh)
    w2 = jnp.stack([cmp_k_w2, cmp_v_w2]).astype(BF16)
    cmp, cmp_a = _compress(p, w1s, pos, w1, w2, row(nsa_kc_norm_g))

    nb = s // SEL_BLOCK
    nbp = -(-nb // LANES) * LANES
    gq = row(nsa_q_norm_g)
    gks = row(nsa_ks_norm_g)
    o_cmp, bias = _cmp_attn(qn, cmp, cmp_a, _overlap_t(s, nbp), p, gq, row(nsa_kc_norm_g), gks,
                            t.cmp_q, min(SEL_TOPK, nb))
    o_cw = _win_attn(qn, kwn, vwa, p, o_cmp, gq, row(nsa_kw_norm_g), t.win_q)
    nsa_out = _sel_attn(qn, bias, ksa, vsa, p, o_cw, gq, gks, t.sel_q, t.sel_k)

    h = _out_proj(x, gla_out, nsa_out, w_out.astype(BF16), t.out_rows)
    return _ffn(h, row(ffn_norm_g), w_gate, w_up, w_down, t.ffn_rows, t.ffn_cols)


def kernel(x, attn_norm_g, w_in, gla_conv_w, gla_gate_w2, gla_gate_b, gla_norm_g, nsa_q_norm_g, nsa_kc_norm_g, nsa_ks_norm_g, nsa_kw_norm_g, cmp_k_pos, cmp_k_w1, cmp_k_w2, cmp_v_pos, cmp_v_w1, cmp_v_w2, w_out, ffn_norm_g, w_gate, w_up, w_down):
    assert x.shape[0] == 1 and attn_norm_g.shape[0] == 1
    y = _layer(x[0], attn_norm_g[0], w_in[0], gla_conv_w[0], gla_gate_w2[0], gla_gate_b[0],
               gla_norm_g[0], nsa_q_norm_g[0], nsa_kc_norm_g[0], nsa_ks_norm_g[0], nsa_kw_norm_g[0],
               cmp_k_pos[0], cmp_k_w1[0], cmp_k_w2[0], cmp_v_pos[0], cmp_v_w1[0], cmp_v_w2[0],
               w_out[0], ffn_norm_g[0], w_gate[0], w_up[0], w_down[0])
    return y[None]
```

```python
import functools
from typing import NamedTuple

import jax
import jax.numpy as jnp
import numpy as np
from jax import lax
from jax.experimental import pallas as pl
from jax.experimental.pallas import tpu as pltpu

D_MODEL = 2048
GLA_HEADS = 4
GLA_DK = 128
GLA_DV = 256
GLA_GATE_RANK = 16
GLA_GATE_TAU = 16.0
GLA_CHUNK = 64
GLA_SUB = 16
GLA_SAFE_LOG_DECAY = 60.0
GLA_HEADS_PER_STEP = 2
SOFTMAX_SHIFT_LIMIT = 40.0
CONV_WIDTH = 4

NSA_HEADS = 8
NSA_KV_HEADS = 2
NSA_GROUP = 4
NSA_HEAD_DIM = 128
CMP_BLOCK = 32
CMP_STRIDE = 16
CMP_HIDDEN = 128
SEL_BLOCK = 64
SEL_TOPK = 16
WINDOW = 512
D_FF = 5632
EPS = 1e-6
NEG = -1e30

GLA_QK = GLA_HEADS * GLA_DK
GLA_VW = GLA_HEADS * GLA_DV
GLA_QKV = 2 * GLA_QK + GLA_VW
NSA_QW = NSA_HEADS * NSA_HEAD_DIM
NSA_KVW = NSA_KV_HEADS * NSA_HEAD_DIM

LANES = 128
VMEM_LIMIT = 56 * 1024 * 1024

C_GQ = 0
C_GK = GLA_QK
C_GV = 2 * GLA_QK
C_GO = GLA_QKV
C_NQ = C_GO + GLA_VW
C_KC = C_NQ + NSA_QW
C_VC = C_KC + NSA_KVW
C_KS = C_VC + NSA_KVW
C_VS = C_KS + NSA_KVW
C_KW = C_VS + NSA_KVW
C_VW = C_KW + NSA_KVW
C_MISC = C_VW + NSA_KVW
PROJ_PAD = C_MISC + 2 * LANES
GATE_LANE = GLA_GATE_RANK

F32 = jnp.float32
BF16 = jnp.bfloat16


def _params(sem):
    return pltpu.CompilerParams(dimension_semantics=sem, vmem_limit_bytes=VMEM_LIMIT)


def _nt_dot(a, b):
    return lax.dot_general(a, b, (((1,), (1,)), ((), ())), preferred_element_type=F32)


def _sigmoid(x):
    return 1.0 / (1.0 + jnp.exp(-x))


def _rms(x, g):
    return x * lax.rsqrt(jnp.mean(x * x, axis=-1, keepdims=True) + EPS) * g


def _norm_matmul_kernel(x_ref, g_ref, w_ref, o_ref, xn_ref):
    @pl.when(pl.program_id(1) == 0)
    def _():
        xn_ref[...] = _rms(x_ref[...], g_ref[...]).astype(BF16)

    o_ref[...] = _nt_dot(xn_ref[...], w_ref[...])


def _norm_matmul(x, g, wt, tm, tn):
    s, d = x.shape
    n = wt.shape[0]
    return pl.pallas_call(
        _norm_matmul_kernel,
        grid=(s // tm, n // tn),
        in_specs=[pl.BlockSpec((tm, d), lambda i, j: (i, 0)),
                  pl.BlockSpec((1, d), lambda i, j: (0, 0)),
                  pl.BlockSpec((tn, d), lambda i, j: (j, 0))],
        out_specs=pl.BlockSpec((tm, tn), lambda i, j: (i, j)),
        out_shape=jax.ShapeDtypeStruct((s, n), F32),
        scratch_shapes=[pltpu.VMEM((tm, d), BF16)],
        compiler_params=_params(("arbitrary", "arbitrary")),
        name="in_proj",
    )(x, g, wt)


_GLA_HEAD_OPERANDS = 12


def _gla_kernel(*refs):
    nh = GLA_HEADS_PER_STEP
    head_refs = [refs[h * _GLA_HEAD_OPERANDS:(h + 1) * _GLA_HEAD_OPERANDS] for h in range(nh)]
    lr_ref, ng_ref, o_ref, st_ref, sq_ref, sk_ref, sv_ref = refs[nh * _GLA_HEAD_OPERANDS:]
    q_ref = head_refs[0][0]
    t_rows = q_ref.shape[0]
    pad = GLA_SUB
    first = pl.program_id(1) == 0

    @pl.when(first)
    def _():
        st_ref[...] = jnp.zeros_like(st_ref)

    def conv_silu(u_ref, p_ref, w_ref, s_ref):
        w = w_ref[...]
        s_ref[0:8, :] = jnp.where(first, 0.0, p_ref[...])
        s_ref[8:16, :] = u_ref[0:8, :]
        head = u_ref[0:8, :] * w[CONV_WIDTH - 1:CONV_WIDTH]
        body = u_ref[8:, :] * w[CONV_WIDTH - 1:CONV_WIDTH]
        for d in range(1, CONV_WIDTH):
            wd = w[CONV_WIDTH - 1 - d:CONV_WIDTH - d]
            head = head + s_ref[pl.ds(8 - d, 8), :] * wd
            body = body + u_ref[pl.ds(8 - d, t_rows - 8), :] * wd
        acc = jnp.concatenate([head, body], axis=0)
        return acc * _sigmoid(acc)

    row = lax.broadcasted_iota(jnp.int32, (t_rows, GLA_DK), 0)
    lane = lax.broadcasted_iota(jnp.int32, (t_rows, GLA_DK), 1)
    rc = row & (GLA_CHUNK - 1)

    def front(hh):
        hq, hqp, hk, hkp, hv, hvp, _, cwq, cwk, cwv, w2_ref, gb_ref = head_refs[hh]
        q = conv_silu(hq, hqp, cwq, sq_ref.at[hh])
        k = conv_silu(hk, hkp, cwk, sk_ref.at[hh])
        v = conv_silu(hv, hvp, cwv, sv_ref.at[hh])
        z = jnp.dot(lr_ref[:, :GLA_GATE_RANK], w2_ref[...], preferred_element_type=F32,
                    precision=lax.Precision.HIGHEST) + gb_ref[...]
        b = (jnp.minimum(z, 0.0) - jnp.log(1.0 + jnp.exp(-jnp.abs(z)))) * (1.0 / GLA_GATE_TAU)
        d = 1
        while d < GLA_CHUNK:
            b = b + jnp.where(rc >= d, pltpu.roll(b, d, 0), 0.0)
            d *= 2
        return q * (GLA_DK ** -0.5), k, v, b

    heads = [front(hh) for hh in range(GLA_HEADS_PER_STEP)]

    srow = lax.broadcasted_iota(jnp.int32, (GLA_CHUNK, GLA_DK), 0)
    scol = lax.broadcasted_iota(jnp.int32, (GLA_CHUNK, GLA_DK), 1)
    zk = jnp.zeros((LANES - GLA_CHUNK, GLA_DK), BF16)
    zv = jnp.zeros((LANES - GLA_CHUNK, GLA_DV), BF16)
    za = jnp.zeros((GLA_SUB, LANES), F32)
    gain = ng_ref[...]

    def chunk_loop(intra_scores):
        for c in range(t_rows // GLA_CHUNK):
            lo = c * GLA_CHUNK
            for hh, (q, k, v, b) in enumerate(heads):
                bc = b[lo:lo + GLA_CHUNK]
                qc = q[lo:lo + GLA_CHUNK]
                kc = k[lo:lo + GLA_CHUNK]
                vc = v[lo:lo + GLA_CHUNK].astype(BF16)
                st = st_ref[hh]
                b_last = bc[GLA_CHUNK - 1:GLA_CHUNK]
                qd = (qc * jnp.exp(bc)).astype(BF16)
                scores = intra_scores(hh, lo, bc, qc, kc, qd)
                o = _nt_dot(qd, st.astype(BF16)) + jnp.dot(
                    scores.astype(BF16), jnp.concatenate([vc, zv], axis=0), preferred_element_type=F32)
                go = head_refs[hh][6][lo:lo + GLA_CHUNK, :]
                o_ref[lo:lo + GLA_CHUNK, hh * GLA_DV:(hh + 1) * GLA_DV] = _rms(o, gain) * (go * _sigmoid(go))
                kt = (kc * jnp.exp(b_last - bc)).astype(BF16)
                upd = lax.dot_general(vc, kt, (((0,), (0,)), ((), ())), preferred_element_type=F32)
                st_ref[hh] = st * jnp.exp(b_last) + upd

    b_min = heads[0][3]
    for hd in heads[1:]:
        b_min = jnp.minimum(b_min, hd[3])
    small_decay = jnp.min(b_min) >= -GLA_SAFE_LOG_DECAY

    @pl.when(small_decay)
    def _():
        def intra_scores(hh, lo, bc, qc, kc, qd):
            kd = (kc * jnp.exp(-bc)).astype(BF16)
            a = _nt_dot(qd, jnp.concatenate([kd, zk], axis=0))
            return jnp.where(scol <= srow, a, 0.0)

        chunk_loop(intra_scores)

    @pl.when(jnp.logical_not(small_decay))
    def _():
        zero_pad = jnp.zeros((pad, GLA_DK), F32)
        rs = row & (GLA_SUB - 1)
        dl = rc - lane
        bands = []
        for hh, (q, k, v, b) in enumerate(heads):
            sq_ref[hh, 0:pad, :] = zero_pad
            sk_ref[hh, 0:pad, :] = zero_pad
            sq_ref[hh, pad:, :] = b
            sk_ref[hh, pad:, :] = k
            band = jnp.where(dl == 0, jnp.sum(q * k, axis=-1, keepdims=True), 0.0)
            for delta in range(1, GLA_SUB):
                e = jnp.exp(jnp.where(rs >= delta, b - sq_ref[hh, pl.ds(pad - delta, t_rows), :], -jnp.inf))
                sc = jnp.sum(q * sk_ref[hh, pl.ds(pad - delta, t_rows), :] * e, axis=-1, keepdims=True)
                band = jnp.where(dl == delta, sc, band)
            bands.append(band)

        def intra_scores(hh, lo, bc, qc, kc, qd):
            blocks = [za]
            for sub in range(1, GLA_CHUNK // GLA_SUB):
                r0 = sub * GLA_SUB
                ref_b = bc[r0:r0 + 1]
                qq = (qc[r0:r0 + GLA_SUB] * jnp.exp(bc[r0:r0 + GLA_SUB] - ref_b)).astype(BF16)
                kk = (kc * jnp.exp(jnp.where(srow < r0, ref_b - bc, -jnp.inf))).astype(BF16)
                blocks.append(_nt_dot(qq, jnp.concatenate([kk, zk], axis=0)))
            return bands[hh][lo:lo + GLA_CHUNK] + jnp.concatenate(blocks, axis=0)

        chunk_loop(intra_scores)


def _gla(p, conv_w, w2, gb, ng, tile):
    s = p.shape[0]
    t8 = tile // 8
    nh = GLA_HEADS_PER_STEP
    dk, dv = GLA_DK, GLA_DV

    def head_specs(hh):
        head = lambda h: h * nh + hh
        qcol = lambda h: C_GQ // dk + head(h)
        kcol = lambda h: C_GK // dk + head(h)
        vcol = lambda h: C_GV // dv + head(h)
        prev = lambda col: (lambda h, i: (jnp.maximum(i * t8 - 1, 0), col(h)))
        return [
            pl.BlockSpec((tile, dk), lambda h, i: (i, qcol(h))),
            pl.BlockSpec((8, dk), prev(qcol)),
            pl.BlockSpec((tile, dk), lambda h, i: (i, kcol(h))),
            pl.BlockSpec((8, dk), prev(kcol)),
            pl.BlockSpec((tile, dv), lambda h, i: (i, vcol(h))),
            pl.BlockSpec((8, dv), prev(vcol)),
            pl.BlockSpec((tile, dv), lambda h, i: (i, C_GO // dv + head(h))),
            pl.BlockSpec((CONV_WIDTH, dk), lambda h, i: (0, qcol(h))),
            pl.BlockSpec((CONV_WIDTH, dk), lambda h, i: (0, kcol(h))),
            pl.BlockSpec((CONV_WIDTH, dv), lambda h, i: (0, vcol(h))),
            pl.BlockSpec((GLA_GATE_RANK, dk), lambda h, i: (0, head(h))),
            pl.BlockSpec((1, dk), lambda h, i: (0, head(h))),
        ]

    head_args = [p, p, p, p, p, p, p, conv_w, conv_w, conv_w, w2, gb]
    assert len(head_args) == _GLA_HEAD_OPERANDS
    return pl.pallas_call(
        _gla_kernel,
        grid=(GLA_HEADS // nh, s // tile),
        in_specs=[spec for hh in range(nh) for spec in head_specs(hh)] + [
            pl.BlockSpec((tile, LANES), lambda h, i: (i, C_MISC // LANES)),
            pl.BlockSpec((1, GLA_DV), lambda h, i: (0, 0)),
        ],
        out_specs=pl.BlockSpec((tile, nh * dv), lambda h, i: (i, h)),
        out_shape=jax.ShapeDtypeStruct((s, GLA_VW), F32),
        scratch_shapes=[pltpu.VMEM((nh, GLA_DV, GLA_DK), F32),
                        pltpu.VMEM((nh, tile + GLA_SUB, GLA_DK), F32),
                        pltpu.VMEM((nh, tile + GLA_SUB, GLA_DK), F32),
                        pltpu.VMEM((nh, GLA_SUB, GLA_DV), F32)],
        compiler_params=_params(("arbitrary", "arbitrary")),
        name="gla",
    )(*(head_args * nh), p, ng)


def _nsa_prep_kernel(q_ref, ks_ref, vs_ref, kw_ref, vw_ref, qg_ref, ksg_ref, kwg_ref,
                     qn_ref, ksa_ref, vsa_ref, kwn_ref, vwa_ref):
    t_rows = q_ref.shape[0]
    dh = NSA_HEAD_DIM
    scale = dh ** -0.5
    ones = jnp.ones((t_rows, dh), BF16)
    for h in range(NSA_HEADS):
        sl = slice(h * dh, (h + 1) * dh)
        qn_ref[:, sl] = (_rms(q_ref[:, sl], qg_ref[...]) * scale).astype(BF16)
    pos = pl.program_id(0) * t_rows + lax.broadcasted_iota(jnp.int32, (t_rows, LANES), 0)
    lane = lax.broadcasted_iota(jnp.int32, (t_rows, LANES), 1)
    onehot = jnp.where(lane == ((pos // SEL_BLOCK) & (LANES - 1)), 1.0, 0.0).astype(BF16)
    for h in range(NSA_KV_HEADS):
        sl = slice(h * dh, (h + 1) * dh)
        ksa_ref[:, 2 * h * dh:(2 * h + 1) * dh] = _rms(ks_ref[:, sl], ksg_ref[...]).astype(BF16)
        ksa_ref[:, (2 * h + 1) * dh:(2 * h + 2) * dh] = onehot
        kwn_ref[:, sl] = _rms(kw_ref[:, sl], kwg_ref[...]).astype(BF16)
        vsa_ref[:, 2 * h * dh:(2 * h + 1) * dh] = vs_ref[:, sl].astype(BF16)
        vsa_ref[:, (2 * h + 1) * dh:(2 * h + 2) * dh] = ones
        vwa_ref[:, 2 * h * dh:(2 * h + 1) * dh] = vw_ref[:, sl].astype(BF16)
        vwa_ref[:, (2 * h + 1) * dh:(2 * h + 2) * dh] = ones


def _nsa_prep(p, qg, ksg, kwg, tile):
    s = p.shape[0]
    kv = NSA_KVW
    col = lambda c, w: (lambda i: (i, c // w))
    row = lambda w: pl.BlockSpec((tile, w), lambda i: (i, 0))
    gain = pl.BlockSpec((1, NSA_HEAD_DIM), lambda i: (0, 0))
    return pl.pallas_call(
        _nsa_prep_kernel,
        grid=(s // tile,),
        in_specs=[pl.BlockSpec((tile, NSA_QW), col(C_NQ, NSA_QW)),
                  pl.BlockSpec((tile, kv), col(C_KS, kv)),
                  pl.BlockSpec((tile, kv), col(C_VS, kv)),
                  pl.BlockSpec((tile, kv), col(C_KW, kv)),
                  pl.BlockSpec((tile, kv), col(C_VW, kv)),
                  gain, gain, gain],
        out_specs=[row(NSA_QW), row(2 * kv), row(2 * kv), row(kv), row(2 * kv)],
        out_shape=[jax.ShapeDtypeStruct((s, NSA_QW), BF16),
                   jax.ShapeDtypeStruct((s, 2 * kv), BF16),
                   jax.ShapeDtypeStruct((s, 2 * kv), BF16),
                   jax.ShapeDtypeStruct((s, kv), BF16),
                   jax.ShapeDtypeStruct((s, 2 * kv), BF16)],
        compiler_params=_params(("arbitrary",)),
        name="nsa_prep",
    )(p, p, p, p, p, qg, ksg, kwg)


def _compress_kernel(u_ref, w1_ref, pos_ref, w1f_ref, w2_ref, g_ref, o_ref, oa_ref):
    n = o_ref.shape[1]
    half = CMP_BLOCK // CMP_STRIDE
    assert half == 2
    acc = jnp.zeros((n, 2 * CMP_HIDDEN), F32)
    for l in range(CMP_STRIDE):
        x = u_ref[pl.ds(l, n, stride=CMP_STRIDE), :].astype(BF16)
        acc = acc + jnp.dot(x, w1_ref[0, l], preferred_element_type=F32)
    posb = jnp.dot(jnp.broadcast_to(pos_ref[0], (8, pos_ref.shape[2])), w1f_ref[0],
                   preferred_element_type=F32, precision=lax.Precision.HIGHEST)[0:1]
    hid = acc[:, :CMP_HIDDEN] + pltpu.roll(acc[:, CMP_HIDDEN:], n - 1, 0) + posb
    hid = hid * _sigmoid(hid)
    out = jnp.dot(hid.astype(BF16), w2_ref[0], preferred_element_type=F32)
    is_k = pl.program_id(0) < NSA_KV_HEADS
    out = jnp.where(is_k, _rms(out, g_ref[...]), out)
    row = lax.broadcasted_iota(jnp.int32, out.shape, 0)
    out = jnp.where(row < n - 1, out, 0.0).astype(BF16)
    o_ref[0] = out
    oa_ref[0] = jnp.concatenate([out, jnp.ones_like(out)], axis=1)


def _compress(p, w1s, pos, w1f, w2, g):
    s = p.shape[0]
    n = s // CMP_STRIDE
    dh = NSA_HEAD_DIM
    return pl.pallas_call(
        _compress_kernel,
        grid=(2 * NSA_KV_HEADS,),
        in_specs=[pl.BlockSpec((s, dh), lambda j: (0, C_KC // dh + j)),
                  pl.BlockSpec((1, CMP_STRIDE, dh, 2 * CMP_HIDDEN), lambda j: (j // 2, 0, 0, 0)),
                  pl.BlockSpec((1, 1, CMP_BLOCK * dh), lambda j: (j // 2, 0, 0)),
                  pl.BlockSpec((1, CMP_BLOCK * dh, CMP_HIDDEN), lambda j: (j // 2, 0, 0)),
                  pl.BlockSpec((1, CMP_HIDDEN, dh), lambda j: (j // 2, 0, 0)),
                  pl.BlockSpec((1, dh), lambda j: (0, 0))],
        out_specs=[pl.BlockSpec((1, n, dh), lambda j: (j, 0, 0)),
                   pl.BlockSpec((1, n, 2 * dh), lambda j: (j, 0, 0))],
        out_shape=[jax.ShapeDtypeStruct((2 * NSA_KV_HEADS, n, dh), BF16),
                   jax.ShapeDtypeStruct((2 * NSA_KV_HEADS, n, 2 * dh), BF16)],
        compiler_params=_params(("arbitrary",)),
        name="compress",
    )(p, w1s, pos, w1f, w2, g)


def _softmax_shift(gq_ref, gk_ref):
    bound = (NSA_HEAD_DIM ** 0.5) * jnp.max(jnp.abs(gq_ref[...])) * jnp.max(jnp.abs(gk_ref[...]))
    usable = bound <= SOFTMAX_SHIFT_LIMIT
    return usable, jnp.where(usable, bound, 0.0)


def _unstack_heads(o, tq):
    return jnp.concatenate([o[g * tq:(g + 1) * tq] for g in range(NSA_GROUP)], axis=1)


def _branch_gate(misc, branch, tq):
    sg = _sigmoid(misc)
    base = GATE_LANE + branch * NSA_GROUP
    return jnp.concatenate([sg[:, base + g:base + g + 1] for g in range(NSA_GROUP)], axis=0)


def _cmp_attn_kernel(q_ref, kc_ref, va_ref, ovl_ref, misc_ref, gq_ref, gk_ref, gks_ref, o_ref, bias_ref,
                     e_ref, *, n_sel):
    tq = q_ref.shape[0]
    dh = NSA_HEAD_DIM
    nc = kc_ref.shape[1]
    nbp = ovl_ref.shape[0]
    i = pl.program_id(0)
    s0 = i * tq
    col_step = max(LANES, nc // 4)
    n_var = nc // col_step
    tiles_per_var = (nc * CMP_STRIDE // tq) // n_var
    assert tiles_per_var * tq == col_step * CMP_STRIDE

    fixed, shift = _softmax_shift(gq_ref, gk_ref)
    _, sel_shift = _softmax_shift(gq_ref, gks_ref)

    def variant(ncv, nbv, online):
        q = q_ref[...]
        row = lax.broadcasted_iota(jnp.int32, (tq, ncv), 0)
        col = lax.broadcasted_iota(jnp.int32, (tq, ncv), 1)
        maskb = jnp.where(col * CMP_STRIDE + (CMP_BLOCK - 1) <= s0 + row, -shift, NEG)
        has_valid = (s0 + lax.broadcasted_iota(jnp.int32, (tq, 1), 0)) >= CMP_BLOCK - 1
        sg = _sigmoid(misc_ref[...])
        ovl = ovl_ref[0:nbv, 0:ncv]
        imps = []
        for h in range(NSA_KV_HEADS):
            kc = kc_ref[h, 0:ncv, :]
            va = va_ref[h, 0:ncv, :]
            invs = []
            for g in range(NSA_GROUP):
                hd = h * NSA_GROUP + g
                sm = _nt_dot(q[:, hd * dh:(hd + 1) * dh], kc) + maskb
                if online:
                    sm = sm - jnp.max(sm, axis=-1, keepdims=True)
                e = jnp.exp(sm)
                e_ref[hd, :, 0:ncv] = e
                acc = jnp.dot(e.astype(BF16), va, preferred_element_type=F32)
                inv = jnp.where(has_valid, 1.0 / acc[:, dh:], 0.0)
                gl = h * LANES + GATE_LANE + g
                o_ref[:, hd * dh:(hd + 1) * dh] = acc[:, :dh] * inv * sg[:, gl:gl + 1]
                invs.append(inv)
            ps = None
            for g in range(NSA_GROUP):
                pc = e_ref[h * NSA_GROUP + g, :, 0:ncv] * jnp.tile(invs[g], (1, ncv // LANES))
                ps = pc if ps is None else ps + pc
            hi = ps.astype(BF16)
            r1 = ps - hi.astype(F32)
            mid = r1.astype(BF16)
            lo = (r1 - mid.astype(F32)).astype(BF16)
            imps.append(_nt_dot(ovl, hi) + _nt_dot(ovl, mid) + _nt_dot(ovl, lo))

        jb = lax.broadcasted_iota(jnp.int32, (nbv, tq), 0)
        jt = (s0 + lax.broadcasted_iota(jnp.int32, (nbv, tq), 1)) // SEL_BLOCK
        forced = (jb == 0) | (jb == jt) | (jb == jt - 1)
        future = jb > jt
        curs = [jnp.where(future | forced, -jnp.inf, imp) for imp in imps]
        for _ in range(n_sel - 3):
            for h in range(NSA_KV_HEADS):
                mx = jnp.max(curs[h], axis=0, keepdims=True)
                idx = jnp.min(jnp.where(curs[h] == mx, jb, nbv), axis=0, keepdims=True)
                curs[h] = jnp.where(jb == idx, -jnp.inf, curs[h])
        for h in range(NSA_KV_HEADS):
            bias_t = jnp.where((curs[h] == -jnp.inf) & jnp.logical_not(future), -sel_shift, NEG)
            bias_ref[:, h * nbp:h * nbp + nbv] = bias_t.T.astype(BF16)
            if nbv < nbp:
                bias_ref[:, h * nbp + nbv:(h + 1) * nbp] = jnp.full((tq, nbp - nbv), NEG, BF16)

    for v in range(n_var):
        ncv = col_step * (v + 1)
        nbv = min(nbp, -(-(ncv // (SEL_BLOCK // CMP_STRIDE)) // LANES) * LANES)
        pl.when(fixed & (i // tiles_per_var == v))(functools.partial(variant, ncv, nbv, False))
    pl.when(jnp.logical_not(fixed))(functools.partial(variant, nc, nbp, True))


def _cmp_attn(qn, cmp, cmp_a, ovl_t, p, gq, gk, gks, tq, n_sel):
    s = qn.shape[0]
    nc = cmp.shape[1]
    nbp = ovl_t.shape[0]
    assert n_sel >= 3
    misc_w = NSA_KV_HEADS * LANES
    gain = pl.BlockSpec((1, NSA_HEAD_DIM), lambda i: (0, 0))
    return pl.pallas_call(
        functools.partial(_cmp_attn_kernel, n_sel=n_sel),
        grid=(s // tq,),
        in_specs=[pl.BlockSpec((tq, NSA_QW), lambda i: (i, 0)),
                  pl.BlockSpec((NSA_KV_HEADS, nc, NSA_HEAD_DIM), lambda i: (0, 0, 0)),
                  pl.BlockSpec((NSA_KV_HEADS, nc, 2 * NSA_HEAD_DIM), lambda i: (1, 0, 0)),
                  pl.BlockSpec((nbp, nc), lambda i: (0, 0)),
                  pl.BlockSpec((tq, misc_w), lambda i: (i, C_MISC // misc_w)),
                  gain, gain, gain],
        out_specs=[pl.BlockSpec((tq, NSA_QW), lambda i: (i, 0)),
                   pl.BlockSpec((tq, NSA_KV_HEADS * nbp), lambda i: (i, 0))],
        out_shape=[jax.ShapeDtypeStruct((s, NSA_QW), F32),
                   jax.ShapeDtypeStruct((s, NSA_KV_HEADS * nbp), BF16)],
        scratch_shapes=[pltpu.VMEM((NSA_HEADS, tq, nc), F32)],
        compiler_params=_params(("arbitrary",)),
        name="cmp_attn",
    )(qn, cmp, cmp_a, ovl_t, p, gq, gk, gks)


def _win_attn_kernel(q_ref, k_ref, v_ref, misc_ref, prev_ref, gq_ref, gk_ref, o_ref, *, span):
    tq = q_ref.shape[0]
    dh = NSA_HEAD_DIM
    s0 = pl.program_id(1) * tq
    start = pl.multiple_of(jnp.maximum(s0 - WINDOW, 0), tq)
    fixed, shift = _softmax_shift(gq_ref, gk_ref)
    row = lax.broadcasted_iota(jnp.int32, (tq, span), 0)
    col = lax.broadcasted_iota(jnp.int32, (tq, span), 1)
    dist = (s0 - start) + row - col
    maskb = jnp.where((dist >= 0) & (dist < WINDOW), -shift, NEG)

    def run(online):
        q = q_ref[...]
        k = k_ref[pl.ds(start, span), :]
        v = v_ref[pl.ds(start, span), :]
        sg = _sigmoid(misc_ref[...])
        base = GATE_LANE + 2 * NSA_GROUP
        for g in range(NSA_GROUP):
            sm = _nt_dot(q[:, g * dh:(g + 1) * dh], k) + maskb
            if online:
                sm = sm - jnp.max(sm, axis=-1, keepdims=True)
            acc = jnp.dot(jnp.exp(sm).astype(BF16), v, preferred_element_type=F32)
            o = acc[:, :dh] * (sg[:, base + g:base + g + 1] / acc[:, dh:])
            o_ref[:, g * dh:(g + 1) * dh] = prev_ref[:, g * dh:(g + 1) * dh] + o

    pl.when(fixed)(functools.partial(run, False))
    pl.when(jnp.logical_not(fixed))(functools.partial(run, True))


def _win_attn(qn, kwn, vwa, p, prev, gq, gk, tq):
    s = qn.shape[0]
    gw = NSA_GROUP * NSA_HEAD_DIM
    span = min(s, WINDOW + tq)
    gain = pl.BlockSpec((1, NSA_HEAD_DIM), lambda h, i: (0, 0))
    return pl.pallas_call(
        functools.partial(_win_attn_kernel, span=span),
        grid=(NSA_KV_HEADS, s // tq),
        in_specs=[pl.BlockSpec((tq, gw), lambda h, i: (i, h)),
                  pl.BlockSpec((s, NSA_HEAD_DIM), lambda h, i: (0, h)),
                  pl.BlockSpec((s, 2 * NSA_HEAD_DIM), lambda h, i: (0, h)),
                  pl.BlockSpec((tq, LANES), lambda h, i: (i, C_MISC // LANES + h)),
                  pl.BlockSpec((tq, gw), lambda h, i: (i, h)),
                  gain, gain],
        out_specs=pl.BlockSpec((tq, gw), lambda h, i: (i, h)),
        out_shape=jax.ShapeDtypeStruct((s, NSA_QW), F32),
        compiler_params=_params(("arbitrary", "arbitrary")),
        name="win_attn",
    )(qn, kwn, vwa, p, prev, gq, gk)


def _sel_attn_kernel(q_ref, bias_ref, ka_ref, va_ref, misc_ref, prev_ref, gq_ref, gk_ref, o_ref,
                     lhs_ref, m_ref, acc_ref, sa_ref, sb_ref, *, tk):
    tq = q_ref.shape[0]
    dh = NSA_HEAD_DIM
    n_super = lhs_ref.shape[0]
    tiles_per_super = LANES * SEL_BLOCK // tk
    s0 = pl.program_id(1) * tq
    fixed, _ = _softmax_shift(gq_ref, gk_ref)
    for sup in range(n_super):
        for g in range(NSA_GROUP):
            lhs_ref[sup, g * tq:(g + 1) * tq, 0:dh] = q_ref[:, g * dh:(g + 1) * dh]
            lhs_ref[sup, g * tq:(g + 1) * tq, dh:2 * dh] = bias_ref[:, sup * LANES:(sup + 1) * LANES]
    m_ref[...] = jnp.full_like(m_ref, -jnp.inf)
    acc_ref[...] = jnp.zeros_like(acc_ref)

    def scores(kt, sc_ref):
        k0 = pl.multiple_of(kt * tk, tk)
        k = ka_ref[pl.ds(k0, tk), :]
        sup = kt // tiles_per_super
        for g in range(NSA_GROUP):
            rows = slice(g * tq, (g + 1) * tq)
            sc_ref[rows, :] = _nt_dot(lhs_ref[sup, rows, :], k)

    def accumulate(online, kt, sc_ref, diagonal):
        k0 = pl.multiple_of(kt * tk, tk)
        v = va_ref[pl.ds(k0, tk), :]
        for g in range(NSA_GROUP):
            rows = slice(g * tq, (g + 1) * tq)
            sc = sc_ref[rows, :]
            if diagonal:
                row = lax.broadcasted_iota(jnp.int32, sc.shape, 0)
                col = lax.broadcasted_iota(jnp.int32, sc.shape, 1)
                sc = jnp.where(k0 + col <= s0 + row, sc, NEG)
            if online:
                m_old = m_ref[rows, :]
                m_new = jnp.maximum(m_old, jnp.max(sc, axis=-1, keepdims=True))
                alpha = jnp.exp(m_old - m_new)
                pr = jnp.exp(sc - jnp.tile(m_new, (1, tk // LANES)))
                acc_ref[rows, :] = (jnp.tile(alpha, (1, 2)) * acc_ref[rows, :]
                                    + jnp.dot(pr.astype(BF16), v, preferred_element_type=F32))
                m_ref[rows, :] = m_new
            else:
                acc_ref[rows, :] += jnp.dot(jnp.exp(sc).astype(BF16), v, preferred_element_type=F32)

    n_full = s0 // tk

    def run(online):
        acc_fn = functools.partial(accumulate, online)

        def body(j, carry):
            scores(2 * j + 1, sb_ref)
            acc_fn(2 * j, sa_ref, False)
            scores(2 * j + 2, sa_ref)
            acc_fn(2 * j + 1, sb_ref, False)
            return carry

        scores(0, sa_ref)
        lax.fori_loop(0, n_full // 2, body, 0)

        @pl.when(n_full % 2 == 0)
        def _():
            acc_fn(n_full, sa_ref, True)

        @pl.when(n_full % 2 == 1)
        def _():
            scores(n_full, sb_ref)
            acc_fn(n_full - 1, sa_ref, False)
            acc_fn(n_full, sb_ref, True)

    pl.when(fixed)(functools.partial(run, False))
    pl.when(jnp.logical_not(fixed))(functools.partial(run, True))

    acc = acc_ref[...]
    o = acc[:, :dh] * (_branch_gate(misc_ref[...], 1, tq) / acc[:, dh:])
    o_ref[...] = prev_ref[...] + _unstack_heads(o, tq)


def _sel_attn(qn, bias, ksa, vsa, p, prev, gq, gk, tq, tk):
    s = qn.shape[0]
    dh = NSA_HEAD_DIM
    gw = NSA_GROUP * dh
    nbp = bias.shape[1] // NSA_KV_HEADS
    n_super = nbp // LANES
    assert tk % tq == 0 and (LANES * SEL_BLOCK) % tk == 0 and s % tk == 0
    resident = lambda: pl.BlockSpec((s, 2 * dh), lambda h, i: (0, h), pipeline_mode=pl.Buffered(1))
    gain = pl.BlockSpec((1, dh), lambda h, i: (0, 0))
    return pl.pallas_call(
        functools.partial(_sel_attn_kernel, tk=tk),
        grid=(NSA_KV_HEADS, s // tq),
        in_specs=[pl.BlockSpec((tq, gw), lambda h, i: (i, h)),
                  pl.BlockSpec((tq, nbp), lambda h, i: (i, h)),
                  resident(), resident(),
                  pl.BlockSpec((tq, LANES), lambda h, i: (i, C_MISC // LANES + h)),
                  pl.BlockSpec((tq, gw), lambda h, i: (i, h)),
                  gain, gain],
        out_specs=pl.BlockSpec((tq, gw), lambda h, i: (i, h)),
        out_shape=jax.ShapeDtypeStruct((s, NSA_QW), F32),
        scratch_shapes=[pltpu.VMEM((n_super, NSA_GROUP * tq, 2 * dh), BF16),
                        pltpu.VMEM((NSA_GROUP * tq, LANES), F32),
                        pltpu.VMEM((NSA_GROUP * tq, 2 * dh), F32),
                        pltpu.VMEM((NSA_GROUP * tq, tk), F32),
                        pltpu.VMEM((NSA_GROUP * tq, tk), F32)],
        compiler_params=_params(("arbitrary", "arbitrary")),
        name="sel_attn",
    )(qn, bias, ksa, vsa, p, prev, gq, gk)


def _out_proj_kernel(x_ref, a_ref, b_ref, wa_ref, wb_ref, o_ref):
    o_ref[...] = (x_ref[...]
                  + jnp.dot(a_ref[...].astype(BF16), wa_ref[...], preferred_element_type=F32)
                  + jnp.dot(b_ref[...].astype(BF16), wb_ref[...], preferred_element_type=F32))


def _out_proj(x, a, b, w, tm):
    s, d = x.shape
    ka, kb = a.shape[1], b.shape[1]
    assert ka == kb
    return pl.pallas_call(
        _out_proj_kernel,
        grid=(s // tm,),
        in_specs=[pl.BlockSpec((tm, d), lambda i: (i, 0)),
                  pl.BlockSpec((tm, ka), lambda i: (i, 0)),
                  pl.BlockSpec((tm, kb), lambda i: (i, 0)),
                  pl.BlockSpec((ka, d), lambda i: (0, 0)),
                  pl.BlockSpec((kb, d), lambda i: (1, 0))],
        out_specs=pl.BlockSpec((tm, d), lambda i: (i, 0)),
        out_shape=jax.ShapeDtypeStruct((s, d), F32),
        compiler_params=_params(("arbitrary",)),
        name="out_proj",
    )(x, a, b, w, w)


def _ffn_kernel(h_ref, g_ref, wg_ref, wu_ref, wd_ref, o_ref, hn_ref):
    @pl.when(pl.program_id(1) == 0)
    def _():
        h = h_ref[...]
        hn_ref[...] = _rms(h, g_ref[...]).astype(BF16)
        o_ref[...] = h

    hn = hn_ref[...]
    a = jnp.dot(hn, wg_ref[...].astype(BF16), preferred_element_type=F32)
    u = jnp.dot(hn, wu_ref[...].astype(BF16), preferred_element_type=F32)
    z = (a * _sigmoid(a) * u).astype(BF16)
    o_ref[...] += jnp.dot(z, wd_ref[...].astype(BF16), preferred_element_type=F32)


def _ffn(h, g, wg, wu, wd, tm, tf):
    s, d = h.shape
    f = wg.shape[1]
    return pl.pallas_call(
        _ffn_kernel,
        grid=(s // tm, f // tf),
        in_specs=[pl.BlockSpec((tm, d), lambda i, j: (i, 0)),
                  pl.BlockSpec((1, d), lambda i, j: (0, 0)),
                  pl.BlockSpec((d, tf), lambda i, j: (0, j)),
                  pl.BlockSpec((d, tf), lambda i, j: (0, j)),
                  pl.BlockSpec((tf, d), lambda i, j: (j, 0))],
        out_specs=pl.BlockSpec((tm, d), lambda i, j: (i, 0)),
        out_shape=jax.ShapeDtypeStruct((s, d), F32),
        scratch_shapes=[pltpu.VMEM((tm, d), BF16)],
        compiler_params=_params(("arbitrary", "arbitrary")),
        name="ffn",
    )(h, g, wg, wu, wd)


class _Tiles(NamedTuple):
    proj_rows: int
    proj_cols: int
    gla_rows: int
    prep_rows: int
    cmp_q: int
    win_q: int
    sel_q: int
    sel_k: int
    out_rows: int
    ffn_rows: int
    ffn_cols: int


def _tiles(s):
    if s >= 4096:
        return _Tiles(proj_rows=512, proj_cols=PROJ_PAD // 2, gla_rows=1024, prep_rows=512,
                      cmp_q=256, win_q=256, sel_q=512, sel_k=1024, out_rows=512, ffn_rows=1024, ffn_cols=256)
    return _Tiles(proj_rows=256, proj_cols=LANES, gla_rows=512, prep_rows=512,
                  cmp_q=128, win_q=256, sel_q=256, sel_k=512, out_rows=256, ffn_rows=256, ffn_cols=256)


_W_IN_LR = GLA_QKV
_W_IN_GO = _W_IN_LR + GLA_GATE_RANK
_W_IN_NG = _W_IN_GO + GLA_VW + NSA_QW + 6 * NSA_KVW


def _regroup_kernel(w_ref, misc_ref, o_ref):
    o_ref[0:GLA_QKV, :] = w_ref[0:GLA_QKV, :].astype(BF16)
    o_ref[GLA_QKV:C_MISC, :] = w_ref[_W_IN_GO:_W_IN_NG, :].astype(BF16)
    o_ref[C_MISC:PROJ_PAD, :] = misc_ref[...]


def _regroup_w_in(w_in, tile=256):
    d = w_in.shape[0]
    wt = w_in.T
    lr = wt[_W_IN_LR:_W_IN_GO]
    ng = wt[_W_IN_NG:_W_IN_NG + 3 * NSA_HEADS].reshape(NSA_KV_HEADS, NSA_GROUP, 3, d)
    ng = ng.transpose(0, 2, 1, 3).reshape(NSA_KV_HEADS, 3 * NSA_GROUP, d)
    pad = lambda n: jnp.zeros((n, d), w_in.dtype)
    tail = pad(LANES - GLA_GATE_RANK - 3 * NSA_GROUP)
    misc = jnp.concatenate([lr, ng[0], tail, pad(GLA_GATE_RANK), ng[1], tail], axis=0).astype(BF16)
    return pl.pallas_call(
        _regroup_kernel,
        grid=(d // tile,),
        in_specs=[pl.BlockSpec((wt.shape[0], tile), lambda i: (0, i)),
                  pl.BlockSpec((2 * LANES, tile), lambda i: (0, i))],
        out_specs=pl.BlockSpec((PROJ_PAD, tile), lambda i: (0, i)),
        out_shape=jax.ShapeDtypeStruct((PROJ_PAD, d), BF16),
        compiler_params=_params(("arbitrary",)),
        name="regroup_w_in",
    )(wt, misc)


def _overlap_t(s, nbp):
    n = s // CMP_STRIDE
    c0 = np.arange(n)[None, :] * CMP_STRIDE
    s0 = np.arange(nbp)[:, None] * SEL_BLOCK
    ov = np.clip(np.minimum(c0 + CMP_BLOCK, s0 + SEL_BLOCK) - np.maximum(c0, s0), 0, None) / CMP_STRIDE
    ov[:, n - 1] = 0.0
    ov[s // SEL_BLOCK:] = 0.0
    return jnp.asarray(ov, BF16)


def _layer(x, attn_norm_g, w_in, gla_conv_w, gla_gate_w2, gla_gate_b, gla_norm_g,
           nsa_q_norm_g, nsa_kc_norm_g, nsa_ks_norm_g, nsa_kw_norm_g,
           cmp_k_pos, cmp_k_w1, cmp_k_w2, cmp_v_pos, cmp_v_w1, cmp_v_w2,
           w_out, ffn_norm_g, w_gate, w_up, w_down):
    s = x.shape[0]
    dh = NSA_HEAD_DIM
    row = lambda v: v.reshape(1, -1)
    t = _tiles(s)

    p = _norm_matmul(x, row(attn_norm_g), _regroup_w_in(w_in), t.proj_rows, t.proj_cols)

    gla_out = _gla(p, gla_conv_w, gla_gate_w2, row(gla_gate_b), row(gla_norm_g), t.gla_rows)

    qn, ksa, vsa, kwn, vwa = _nsa_prep(p, row(nsa_q_norm_g), row(nsa_ks_norm_g), row(nsa_kw_norm_g),
                                       t.prep_rows)

    w1 = jnp.stack([cmp_k_w1, cmp_v_w1])
    w1s = (w1.reshape(2, 2, CMP_STRIDE, dh, CMP_HIDDEN).transpose(0, 2, 3, 1, 4)
           .reshape(2, CMP_STRIDE, dh, 2 * CMP_HIDDEN).astype(BF16))
    pos = jnp.stack([cmp_k_pos, cmp_v_pos]).reshape(2, 1, CMP_BLOCK * dh)
    w2 = jnp.stack([cmp_k_w2, cmp_v_w2]).astype(BF16)
    cmp, cmp_a = _compress(p, w1s, pos, w1, w2, row(nsa_kc_norm_g))

    nb = s // SEL_BLOCK
    nbp = -(-nb // LANES) * LANES
    gq = row(nsa_q_norm_g)
    gks = row(nsa_ks_norm_g)
    o_cmp, bias = _cmp_attn(qn, cmp, cmp_a, _overlap_t(s, nbp), p, gq, row(nsa_kc_norm_g), gks,
                            t.cmp_q, min(SEL_TOPK, nb))
    o_cw = _win_attn(qn, kwn, vwa, p, o_cmp, gq, row(nsa_kw_norm_g), t.win_q)
    nsa_out = _sel_attn(qn, bias, ksa, vsa, p, o_cw, gq, gks, t.sel_q, t.sel_k)

    h = _out_proj(x, gla_out, nsa_out, w_out.astype(BF16), t.out_rows)
    return _ffn(h, row(ffn_norm_g), w_gate, w_up, w_down, t.ffn_rows, t.ffn_cols)


def kernel(x, attn_norm_g, w_in, gla_conv_w, gla_gate_w2, gla_gate_b, gla_norm_g, nsa_q_norm_g, nsa_kc_norm_g, nsa_ks_norm_g, nsa_kw_norm_g, cmp_k_pos, cmp_k_w1, cmp_k_w2, cmp_v_pos, cmp_v_w1, cmp_v_w2, w_out, ffn_norm_g, w_gate, w_up, w_down):
    assert x.shape[0] == 1 and attn_norm_g.shape[0] == 1
    y = _layer(x[0], attn_norm_g[0], w_in[0], gla_conv_w[0], gla_gate_w2[0], gla_gate_b[0],
               gla_norm_g[0], nsa_q_norm_g[0], nsa_kc_norm_g[0], nsa_ks_norm_g[0], nsa_kw_norm_g[0],
               cmp_k_pos[0], cmp_k_w1[0], cmp_k_w2[0], cmp_v_pos[0], cmp_v_w1[0], cmp_v_w2[0],
               w_out[0], ffn_norm_g[0], w_gate[0], w_up[0], w_down[0])
    return y[None]
```

```python
import functools
from typing import NamedTuple

import jax
import jax.numpy as jnp
import numpy as np
from jax import lax
from jax.experimental import pallas as pl
from jax.experimental.pallas import tpu as pltpu

D_MODEL = 2048
GLA_HEADS = 4
GLA_DK = 128
GLA_DV = 256
GLA_GATE_RANK = 16
GLA_GATE_TAU = 16.0
GLA_CHUNK = 64
GLA_SUB = 16
GLA_SAFE_LOG_DECAY = 60.0
GLA_HEADS_PER_STEP = 2
SOFTMAX_SHIFT_LIMIT = 40.0
CONV_WIDTH = 4

NSA_HEADS = 8
NSA_KV_HEADS = 2
NSA_GROUP = 4
NSA_HEAD_DIM = 128
CMP_BLOCK = 32
CMP_STRIDE = 16
CMP_HIDDEN = 128
SEL_BLOCK = 64
SEL_TOPK = 16
WINDOW = 512
D_FF = 5632
EPS = 1e-6
NEG = -1e30

GLA_QK = GLA_HEADS * GLA_DK
GLA_VW = GLA_HEADS * GLA_DV
GLA_QKV = 2 * GLA_QK + GLA_VW
NSA_QW = NSA_HEADS * NSA_HEAD_DIM
NSA_KVW = NSA_KV_HEADS * NSA_HEAD_DIM

LANES = 128
VMEM_LIMIT = 56 * 1024 * 1024

C_GQ = 0
C_GK = GLA_QK
C_GV = 2 * GLA_QK
C_GO = GLA_QKV
C_NQ = C_GO + GLA_VW
C_KC = C_NQ + NSA_QW
C_VC = C_KC + NSA_KVW
C_KS = C_VC + NSA_KVW
C_VS = C_KS + NSA_KVW
C_KW = C_VS + NSA_KVW
C_VW = C_KW + NSA_KVW
C_MISC = C_VW + NSA_KVW
PROJ_PAD = C_MISC + 2 * LANES
GATE_LANE = GLA_GATE_RANK

F32 = jnp.float32
BF16 = jnp.bfloat16


def _params(sem):
    return pltpu.CompilerParams(dimension_semantics=sem, vmem_limit_bytes=VMEM_LIMIT)


def _nt_dot(a, b):
    return lax.dot_general(a, b, (((1,), (1,)), ((), ())), preferred_element_type=F32)


def _sigmoid(x):
    return 1.0 / (1.0 + jnp.exp(-x))


def _rms(x, g):
    return x * lax.rsqrt(jnp.mean(x * x, axis=-1, keepdims=True) + EPS) * g


def _in_proj_kernel(x_ref, g_ref, w_ref, qg_ref, ksg_ref, kwg_ref,
                    o_ref, qn_ref, ksa_ref, vsa_ref, kwn_ref, vwa_ref, xn_ref):
    t_rows, tn = o_ref.shape
    j = pl.program_id(1)

    @pl.when(j == 0)
    def _():
        xn_ref[...] = _rms(x_ref[...], g_ref[...]).astype(BF16)

    o_ref[...] = _nt_dot(xn_ref[...], w_ref[...])

    base = PROJ_PAD - tn
    assert base <= C_NQ

    @pl.when(j == pl.num_programs(1) - 1)
    def _():
        dh = NSA_HEAD_DIM
        col = lambda c, w=dh: o_ref[:, c - base:c - base + w]
        ones = jnp.ones((t_rows, dh), BF16)
        for h in range(NSA_HEADS):
            qn_ref[:, h * dh:(h + 1) * dh] = (
                _rms(col(C_NQ + h * dh), qg_ref[...]) * (dh ** -0.5)).astype(BF16)
        pos = pl.program_id(0) * t_rows + lax.broadcasted_iota(jnp.int32, (t_rows, LANES), 0)
        lane = lax.broadcasted_iota(jnp.int32, (t_rows, LANES), 1)
        onehot = jnp.where(lane == ((pos // SEL_BLOCK) & (LANES - 1)), 1.0, 0.0).astype(BF16)
        for h in range(NSA_KV_HEADS):
            lo, hi = 2 * h * dh, (2 * h + 1) * dh
            ksa_ref[:, lo:hi] = _rms(col(C_KS + h * dh), ksg_ref[...]).astype(BF16)
            ksa_ref[:, hi:hi + dh] = onehot
            kwn_ref[:, h * dh:(h + 1) * dh] = _rms(col(C_KW + h * dh), kwg_ref[...]).astype(BF16)
            vsa_ref[:, lo:hi] = col(C_VS + h * dh).astype(BF16)
            vsa_ref[:, hi:hi + dh] = ones
            vwa_ref[:, lo:hi] = col(C_VW + h * dh).astype(BF16)
            vwa_ref[:, hi:hi + dh] = ones


def _in_proj(x, g, wt, qg, ksg, kwg, tm, tn):
    s, d = x.shape
    n = wt.shape[0]
    kv = NSA_KVW
    gain = pl.BlockSpec((1, NSA_HEAD_DIM), lambda i, j: (0, 0))
    rows = lambda w: pl.BlockSpec((tm, w), lambda i, j: (i, 0))
    return pl.pallas_call(
        _in_proj_kernel,
        grid=(s // tm, n // tn),
        in_specs=[pl.BlockSpec((tm, d), lambda i, j: (i, 0)),
                  pl.BlockSpec((1, d), lambda i, j: (0, 0)),
                  pl.BlockSpec((tn, d), lambda i, j: (j, 0)),
                  gain, gain, gain],
        out_specs=[pl.BlockSpec((tm, tn), lambda i, j: (i, j)),
                   rows(NSA_QW), rows(2 * kv), rows(2 * kv), rows(kv), rows(2 * kv)],
        out_shape=[jax.ShapeDtypeStruct((s, n), F32),
                   jax.ShapeDtypeStruct((s, NSA_QW), BF16),
                   jax.ShapeDtypeStruct((s, 2 * kv), BF16),
                   jax.ShapeDtypeStruct((s, 2 * kv), BF16),
                   jax.ShapeDtypeStruct((s, kv), BF16),
                   jax.ShapeDtypeStruct((s, 2 * kv), BF16)],
        scratch_shapes=[pltpu.VMEM((tm, d), BF16)],
        compiler_params=_params(("arbitrary", "arbitrary")),
        name="in_proj",
    )(x, g, wt, qg, ksg, kwg)


_GLA_HEAD_OPERANDS = 12


def _gla_kernel(*refs):
    nh = GLA_HEADS_PER_STEP
    head_refs = [refs[h * _GLA_HEAD_OPERANDS:(h + 1) * _GLA_HEAD_OPERANDS] for h in range(nh)]
    lr_ref, ng_ref, o_ref, st_ref, sq_ref, sk_ref, sv_ref = refs[nh * _GLA_HEAD_OPERANDS:]
    q_ref = head_refs[0][0]
    t_rows = q_ref.shape[0]
    pad = GLA_SUB
    first = pl.program_id(1) == 0

    @pl.when(first)
    def _():
        st_ref[...] = jnp.zeros_like(st_ref)

    def conv_silu(u_ref, p_ref, w_ref, s_ref):
        w = w_ref[...]
        s_ref[0:8, :] = jnp.where(first, 0.0, p_ref[...])
        s_ref[8:16, :] = u_ref[0:8, :]
        head = u_ref[0:8, :] * w[CONV_WIDTH - 1:CONV_WIDTH]
        body = u_ref[8:, :] * w[CONV_WIDTH - 1:CONV_WIDTH]
        for d in range(1, CONV_WIDTH):
            wd = w[CONV_WIDTH - 1 - d:CONV_WIDTH - d]
            head = head + s_ref[pl.ds(8 - d, 8), :] * wd
            body = body + u_ref[pl.ds(8 - d, t_rows - 8), :] * wd
        acc = jnp.concatenate([head, body], axis=0)
        return acc * _sigmoid(acc)

    row = lax.broadcasted_iota(jnp.int32, (t_rows, GLA_DK), 0)
    lane = lax.broadcasted_iota(jnp.int32, (t_rows, GLA_DK), 1)
    rc = row & (GLA_CHUNK - 1)

    def front(hh):
        hq, hqp, hk, hkp, hv, hvp, _, cwq, cwk, cwv, w2_ref, gb_ref = head_refs[hh]
        q = conv_silu(hq, hqp, cwq, sq_ref.at[hh])
        k = conv_silu(hk, hkp, cwk, sk_ref.at[hh])
        v = conv_silu(hv, hvp, cwv, sv_ref.at[hh])
        z = jnp.dot(lr_ref[:, :GLA_GATE_RANK], w2_ref[...], preferred_element_type=F32,
                    precision=lax.Precision.HIGHEST) + gb_ref[...]
        b = (jnp.minimum(z, 0.0) - jnp.log(1.0 + jnp.exp(-jnp.abs(z)))) * (1.0 / GLA_GATE_TAU)
        d = 1
        while d < GLA_CHUNK:
            b = b + jnp.where(rc >= d, pltpu.roll(b, d, 0), 0.0)
            d *= 2
        return q * (GLA_DK ** -0.5), k, v, b

    heads = [front(hh) for hh in range(GLA_HEADS_PER_STEP)]

    srow = lax.broadcasted_iota(jnp.int32, (GLA_CHUNK, GLA_DK), 0)
    scol = lax.broadcasted_iota(jnp.int32, (GLA_CHUNK, GLA_DK), 1)
    zk = jnp.zeros((LANES - GLA_CHUNK, GLA_DK), BF16)
    zv = jnp.zeros((LANES - GLA_CHUNK, GLA_DV), BF16)
    za = jnp.zeros((GLA_SUB, LANES), F32)
    gain = ng_ref[...]

    def chunk_loop(intra_scores):
        for c in range(t_rows // GLA_CHUNK):
            lo = c * GLA_CHUNK
            for hh, (q, k, v, b) in enumerate(heads):
                bc = b[lo:lo + GLA_CHUNK]
                qc = q[lo:lo + GLA_CHUNK]
                kc = k[lo:lo + GLA_CHUNK]
                vc = v[lo:lo + GLA_CHUNK].astype(BF16)
                st = st_ref[hh]
                b_last = bc[GLA_CHUNK - 1:GLA_CHUNK]
                qd = (qc * jnp.exp(bc)).astype(BF16)
                scores = intra_scores(hh, lo, bc, qc, kc, qd)
                o = _nt_dot(qd, st.astype(BF16)) + jnp.dot(
                    scores.astype(BF16), jnp.concatenate([vc, zv], axis=0), preferred_element_type=F32)
                go = head_refs[hh][6][lo:lo + GLA_CHUNK, :]
                o_ref[lo:lo + GLA_CHUNK, hh * GLA_DV:(hh + 1) * GLA_DV] = _rms(o, gain) * (go * _sigmoid(go))
                kt = (kc * jnp.exp(b_last - bc)).astype(BF16)
                upd = lax.dot_general(vc, kt, (((0,), (0,)), ((), ())), preferred_element_type=F32)
                st_ref[hh] = st * jnp.exp(b_last) + upd

    b_min = heads[0][3]
    for hd in heads[1:]:
        b_min = jnp.minimum(b_min, hd[3])
    small_decay = jnp.min(b_min) >= -GLA_SAFE_LOG_DECAY

    @pl.when(small_decay)
    def _():
        def intra_scores(hh, lo, bc, qc, kc, qd):
            kd = (kc * jnp.exp(-bc)).astype(BF16)
            a = _nt_dot(qd, jnp.concatenate([kd, zk], axis=0))
            return jnp.where(scol <= srow, a, 0.0)

        chunk_loop(intra_scores)

    @pl.when(jnp.logical_not(small_decay))
    def _():
        zero_pad = jnp.zeros((pad, GLA_DK), F32)
        rs = row & (GLA_SUB - 1)
        dl = rc - lane
        bands = []
        for hh, (q, k, v, b) in enumerate(heads):
            sq_ref[hh, 0:pad, :] = zero_pad
            sk_ref[hh, 0:pad, :] = zero_pad
            sq_ref[hh, pad:, :] = b
            sk_ref[hh, pad:, :] = k
            band = jnp.where(dl == 0, jnp.sum(q * k, axis=-1, keepdims=True), 0.0)
            for delta in range(1, GLA_SUB):
                e = jnp.exp(jnp.where(rs >= delta, b - sq_ref[hh, pl.ds(pad - delta, t_rows), :], -jnp.inf))
                sc = jnp.sum(q * sk_ref[hh, pl.ds(pad - delta, t_rows), :] * e, axis=-1, keepdims=True)
                band = jnp.where(dl == delta, sc, band)
            bands.append(band)

        def intra_scores(hh, lo, bc, qc, kc, qd):
            blocks = [za]
            for sub in range(1, GLA_CHUNK // GLA_SUB):
                r0 = sub * GLA_SUB
                ref_b = bc[r0:r0 + 1]
                qq = (qc[r0:r0 + GLA_SUB] * jnp.exp(bc[r0:r0 + GLA_SUB] - ref_b)).astype(BF16)
                kk = (kc * jnp.exp(jnp.where(srow < r0, ref_b - bc, -jnp.inf))).astype(BF16)
                blocks.append(_nt_dot(qq, jnp.concatenate([kk, zk], axis=0)))
            return bands[hh][lo:lo + GLA_CHUNK] + jnp.concatenate(blocks, axis=0)

        chunk_loop(intra_scores)


def _gla(p, conv_w, w2, gb, ng, tile):
    s = p.shape[0]
    t8 = tile // 8
    nh = GLA_HEADS_PER_STEP
    dk, dv = GLA_DK, GLA_DV

    def head_specs(hh):
        head = lambda h: h * nh + hh
        qcol = lambda h: C_GQ // dk + head(h)
        kcol = lambda h: C_GK // dk + head(h)
        vcol = lambda h: C_GV // dv + head(h)
        prev = lambda col: (lambda h, i: (jnp.maximum(i * t8 - 1, 0), col(h)))
        return [
            pl.BlockSpec((tile, dk), lambda h, i: (i, qcol(h))),
            pl.BlockSpec((8, dk), prev(qcol)),
            pl.BlockSpec((tile, dk), lambda h, i: (i, kcol(h))),
            pl.BlockSpec((8, dk), prev(kcol)),
            pl.BlockSpec((tile, dv), lambda h, i: (i, vcol(h))),
            pl.BlockSpec((8, dv), prev(vcol)),
            pl.BlockSpec((tile, dv), lambda h, i: (i, C_GO // dv + head(h))),
            pl.BlockSpec((CONV_WIDTH, dk), lambda h, i: (0, qcol(h))),
            pl.BlockSpec((CONV_WIDTH, dk), lambda h, i: (0, kcol(h))),
            pl.BlockSpec((CONV_WIDTH, dv), lambda h, i: (0, vcol(h))),
            pl.BlockSpec((GLA_GATE_RANK, dk), lambda h, i: (0, head(h))),
            pl.BlockSpec((1, dk), lambda h, i: (0, head(h))),
        ]

    head_args = [p, p, p, p, p, p, p, conv_w, conv_w, conv_w, w2, gb]
    assert len(head_args) == _GLA_HEAD_OPERANDS
    return pl.pallas_call(
        _gla_kernel,
        grid=(GLA_HEADS // nh, s // tile),
        in_specs=[spec for hh in range(nh) for spec in head_specs(hh)] + [
            pl.BlockSpec((tile, LANES), lambda h, i: (i, C_MISC // LANES)),
            pl.BlockSpec((1, GLA_DV), lambda h, i: (0, 0)),
        ],
        out_specs=pl.BlockSpec((tile, nh * dv), lambda h, i: (i, h)),
        out_shape=jax.ShapeDtypeStruct((s, GLA_VW), F32),
        scratch_shapes=[pltpu.VMEM((nh, GLA_DV, GLA_DK), F32),
                        pltpu.VMEM((nh, tile + GLA_SUB, GLA_DK), F32),
                        pltpu.VMEM((nh, tile + GLA_SUB, GLA_DK), F32),
                        pltpu.VMEM((nh, GLA_SUB, GLA_DV), F32)],
        compiler_params=_params(("arbitrary", "arbitrary")),
        name="gla",
    )(*(head_args * nh), p, ng)


def _compress_kernel(u_ref, w1_ref, pos_ref, w1f_ref, w2_ref, g_ref, o_ref, oa_ref):
    n = o_ref.shape[1]
    half = CMP_BLOCK // CMP_STRIDE
    assert half == 2
    acc = jnp.zeros((n, 2 * CMP_HIDDEN), F32)
    for l in range(CMP_STRIDE):
        x = u_ref[pl.ds(l, n, stride=CMP_STRIDE), :].astype(BF16)
        acc = acc + jnp.dot(x, w1_ref[0, l], preferred_element_type=F32)
    posb = jnp.dot(jnp.broadcast_to(pos_ref[0], (8, pos_ref.shape[2])), w1f_ref[0],
                   preferred_element_type=F32, precision=lax.Precision.HIGHEST)[0:1]
    hid = acc[:, :CMP_HIDDEN] + pltpu.roll(acc[:, CMP_HIDDEN:], n - 1, 0) + posb
    hid = hid * _sigmoid(hid)
    out = jnp.dot(hid.astype(BF16), w2_ref[0], preferred_element_type=F32)
    is_k = pl.program_id(0) < NSA_KV_HEADS
    out = jnp.where(is_k, _rms(out, g_ref[...]), out)
    row = lax.broadcasted_iota(jnp.int32, out.shape, 0)
    out = jnp.where(row < n - 1, out, 0.0).astype(BF16)
    o_ref[0] = out
    oa_ref[0] = jnp.concatenate([out, jnp.ones_like(out)], axis=1)


def _compress(p, w1s, pos, w1f, w2, g):
    s = p.shape[0]
    n = s // CMP_STRIDE
    dh = NSA_HEAD_DIM
    return pl.pallas_call(
        _compress_kernel,
        grid=(2 * NSA_KV_HEADS,),
        in_specs=[pl.BlockSpec((s, dh), lambda j: (0, C_KC // dh + j)),
                  pl.BlockSpec((1, CMP_STRIDE, dh, 2 * CMP_HIDDEN), lambda j: (j // 2, 0, 0, 0)),
                  pl.BlockSpec((1, 1, CMP_BLOCK * dh), lambda j: (j // 2, 0, 0)),
                  pl.BlockSpec((1, CMP_BLOCK * dh, CMP_HIDDEN), lambda j: (j // 2, 0, 0)),
                  pl.BlockSpec((1, CMP_HIDDEN, dh), lambda j: (j // 2, 0, 0)),
                  pl.BlockSpec((1, dh), lambda j: (0, 0))],
        out_specs=[pl.BlockSpec((1, n, dh), lambda j: (j, 0, 0)),
                   pl.BlockSpec((1, n, 2 * dh), lambda j: (j, 0, 0))],
        out_shape=[jax.ShapeDtypeStruct((2 * NSA_KV_HEADS, n, dh), BF16),
                   jax.ShapeDtypeStruct((2 * NSA_KV_HEADS, n, 2 * dh), BF16)],
        compiler_params=_params(("arbitrary",)),
        name="compress",
    )(p, w1s, pos, w1f, w2, g)


def _softmax_shift(gq_ref, gk_ref):
    bound = (NSA_HEAD_DIM ** 0.5) * jnp.max(jnp.abs(gq_ref[...])) * jnp.max(jnp.abs(gk_ref[...]))
    usable = bound <= SOFTMAX_SHIFT_LIMIT
    return usable, jnp.where(usable, bound, 0.0)


def _unstack_heads(o, tq):
    return jnp.concatenate([o[g * tq:(g + 1) * tq] for g in range(NSA_GROUP)], axis=1)


def _branch_gate(misc, branch, tq):
    sg = _sigmoid(misc)
    base = GATE_LANE + branch * NSA_GROUP
    return jnp.concatenate([sg[:, base + g:base + g + 1] for g in range(NSA_GROUP)], axis=0)


def _cmp_attn_kernel(q_ref, kc_ref, va_ref, ovl_ref, misc_ref, gq_ref, gk_ref, gks_ref, o_ref, bias_ref,
                     e_ref, *, n_sel):
    tq = q_ref.shape[0]
    dh = NSA_HEAD_DIM
    nc = kc_ref.shape[1]
    nbp = ovl_ref.shape[0]
    i = pl.program_id(0)
    s0 = i * tq
    col_step = max(LANES, nc // 4)
    n_var = nc // col_step
    tiles_per_var = (nc * CMP_STRIDE // tq) // n_var
    assert tiles_per_var * tq == col_step * CMP_STRIDE

    fixed, shift = _softmax_shift(gq_ref, gk_ref)
    _, sel_shift = _softmax_shift(gq_ref, gks_ref)

    def variant(ncv, nbv, online):
        q = q_ref[...]
        row = lax.broadcasted_iota(jnp.int32, (tq, ncv), 0)
        col = lax.broadcasted_iota(jnp.int32, (tq, ncv), 1)
        maskb = jnp.where(col * CMP_STRIDE + (CMP_BLOCK - 1) <= s0 + row, -shift, NEG)
        has_valid = (s0 + lax.broadcasted_iota(jnp.int32, (tq, 1), 0)) >= CMP_BLOCK - 1
        sg = _sigmoid(misc_ref[...])
        ovl = ovl_ref[0:nbv, 0:ncv]
        imps = []
        for h in range(NSA_KV_HEADS):
            kc = kc_ref[h, 0:ncv, :]
            va = va_ref[h, 0:ncv, :]
            invs = []
            for g in range(NSA_GROUP):
                hd = h * NSA_GROUP + g
                sm = _nt_dot(q[:, hd * dh:(hd + 1) * dh], kc) + maskb
                if online:
                    sm = sm - jnp.max(sm, axis=-1, keepdims=True)
                e = jnp.exp(sm)
                e_ref[hd, :, 0:ncv] = e
                acc = jnp.dot(e.astype(BF16), va, preferred_element_type=F32)
                inv = jnp.where(has_valid, 1.0 / acc[:, dh:], 0.0)
                gl = h * LANES + GATE_LANE + g
                o_ref[:, hd * dh:(hd + 1) * dh] = acc[:, :dh] * inv * sg[:, gl:gl + 1]
                invs.append(inv)
            ps = None
            for g in range(NSA_GROUP):
                pc = e_ref[h * NSA_GROUP + g, :, 0:ncv] * jnp.tile(invs[g], (1, ncv // LANES))
                ps = pc if ps is None else ps + pc
            hi = ps.astype(BF16)
            r1 = ps - hi.astype(F32)
            mid = r1.astype(BF16)
            lo = (r1 - mid.astype(F32)).astype(BF16)
            imps.append(_nt_dot(ovl, hi) + _nt_dot(ovl, mid) + _nt_dot(ovl, lo))

        jb = lax.broadcasted_iota(jnp.int32, (nbv, tq), 0)
        jt = (s0 + lax.broadcasted_iota(jnp.int32, (nbv, tq), 1)) // SEL_BLOCK
        forced = (jb == 0) | (jb == jt) | (jb == jt - 1)
        future = jb > jt
        curs = [jnp.where(future | forced, -jnp.inf, imp) for imp in imps]
        for _ in range(n_sel - 3):
            for h in range(NSA_KV_HEADS):
                mx = jnp.max(curs[h], axis=0, keepdims=True)
                idx = jnp.min(jnp.where(curs[h] == mx, jb, nbv), axis=0, keepdims=True)
                curs[h] = jnp.where(jb == idx, -jnp.inf, curs[h])
        for h in range(NSA_KV_HEADS):
            bias_t = jnp.where((curs[h] == -jnp.inf) & jnp.logical_not(future), -sel_shift, NEG)
            bias_ref[:, h * nbp:h * nbp + nbv] = bias_t.T.astype(BF16)
            if nbv < nbp:
                bias_ref[:, h * nbp + nbv:(h + 1) * nbp] = jnp.full((tq, nbp - nbv), NEG, BF16)

    for v in range(n_var):
        ncv = col_step * (v + 1)
        nbv = min(nbp, -(-(ncv // (SEL_BLOCK // CMP_STRIDE)) // LANES) * LANES)
        pl.when(fixed & (i // tiles_per_var == v))(functools.partial(variant, ncv, nbv, False))
    pl.when(jnp.logical_not(fixed))(functools.partial(variant, nc, nbp, True))


def _cmp_attn(qn, cmp, cmp_a, ovl_t, p, gq, gk, gks, tq, n_sel):
    s = qn.shape[0]
    nc = cmp.shape[1]
    nbp = ovl_t.shape[0]
    assert n_sel >= 3
    misc_w = NSA_KV_HEADS * LANES
    gain = pl.BlockSpec((1, NSA_HEAD_DIM), lambda i: (0, 0))
    return pl.pallas_call(
        functools.partial(_cmp_attn_kernel, n_sel=n_sel),
        grid=(s // tq,),
        in_specs=[pl.BlockSpec((tq, NSA_QW), lambda i: (i, 0)),
                  pl.BlockSpec((NSA_KV_HEADS, nc, NSA_HEAD_DIM), lambda i: (0, 0, 0)),
                  pl.BlockSpec((NSA_KV_HEADS, nc, 2 * NSA_HEAD_DIM), lambda i: (1, 0, 0)),
                  pl.BlockSpec((nbp, nc), lambda i: (0, 0)),
                  pl.BlockSpec((tq, misc_w), lambda i: (i, C_MISC // misc_w)),
                  gain, gain, gain],
        out_specs=[pl.BlockSpec((tq, NSA_QW), lambda i: (i, 0)),
                   pl.BlockSpec((tq, NSA_KV_HEADS * nbp), lambda i: (i, 0))],
        out_shape=[jax.ShapeDtypeStruct((s, NSA_QW), F32),
                   jax.ShapeDtypeStruct((s, NSA_KV_HEADS * nbp), BF16)],
        scratch_shapes=[pltpu.VMEM((NSA_HEADS, tq, nc), F32)],
        compiler_params=_params(("arbitrary",)),
        name="cmp_attn",
    )(qn, cmp, cmp_a, ovl_t, p, gq, gk, gks)


def _win_attn_kernel(q_ref, k_ref, v_ref, misc_ref, prev_ref, gq_ref, gk_ref, o_ref, *, span):
    tq = q_ref.shape[0]
    dh = NSA_HEAD_DIM
    s0 = pl.program_id(1) * tq
    start = pl.multiple_of(jnp.maximum(s0 - WINDOW, 0), tq)
    fixed, shift = _softmax_shift(gq_ref, gk_ref)
    row = lax.broadcasted_iota(jnp.int32, (tq, span), 0)
    col = lax.broadcasted_iota(jnp.int32, (tq, span), 1)
    dist = (s0 - start) + row - col
    maskb = jnp.where((dist >= 0) & (dist < WINDOW), -shift, NEG)

    def run(online):
        q = q_ref[...]
        k = k_ref[pl.ds(start, span), :]
        v = v_ref[pl.ds(start, span), :]
        sg = _sigmoid(misc_ref[...])
        base = GATE_LANE + 2 * NSA_GROUP
        for g in range(NSA_GROUP):
            sm = _nt_dot(q[:, g * dh:(g + 1) * dh], k) + maskb
            if online:
                sm = sm - jnp.max(sm, axis=-1, keepdims=True)
            acc = jnp.dot(jnp.exp(sm).astype(BF16), v, preferred_element_type=F32)
            o = acc[:, :dh] * (sg[:, base + g:base + g + 1] / acc[:, dh:])
            o_ref[:, g * dh:(g + 1) * dh] = prev_ref[:, g * dh:(g + 1) * dh] + o

    pl.when(fixed)(functools.partial(run, False))
    pl.when(jnp.logical_not(fixed))(functools.partial(run, True))


def _win_attn(qn, kwn, vwa, p, prev, gq, gk, tq):
    s = qn.shape[0]
    gw = NSA_GROUP * NSA_HEAD_DIM
    span = min(s, WINDOW + tq)
    gain = pl.BlockSpec((1, NSA_HEAD_DIM), lambda h, i: (0, 0))
    return pl.pallas_call(
        functools.partial(_win_attn_kernel, span=span),
        grid=(NSA_KV_HEADS, s // tq),
        in_specs=[pl.BlockSpec((tq, gw), lambda h, i: (i, h)),
                  pl.BlockSpec((s, NSA_HEAD_DIM), lambda h, i: (0, h)),
                  pl.BlockSpec((s, 2 * NSA_HEAD_DIM), lambda h, i: (0, h)),
                  pl.BlockSpec((tq, LANES), lambda h, i: (i, C_MISC // LANES + h)),
                  pl.BlockSpec((tq, gw), lambda h, i: (i, h)),
                  gain, gain],
        out_specs=pl.BlockSpec((tq, gw), lambda h, i: (i, h)),
        out_shape=jax.ShapeDtypeStruct((s, NSA_QW), F32),
        compiler_params=_params(("arbitrary", "arbitrary")),
        name="win_attn",
    )(qn, kwn, vwa, p, prev, gq, gk)


def _sel_attn_kernel(q_ref, bias_ref, ka_ref, va_ref, misc_ref, prev_ref, gq_ref, gk_ref, o_ref,
                     lhs_ref, m_ref, acc_ref, sa_ref, sb_ref, *, tk):
    tq = q_ref.shape[0]
    dh = NSA_HEAD_DIM
    n_super = lhs_ref.shape[0]
    tiles_per_super = LANES * SEL_BLOCK // tk
    s0 = pl.program_id(1) * tq
    fixed, _ = _softmax_shift(gq_ref, gk_ref)
    def build_lhs(sup):
        for g in range(NSA_GROUP):
            lhs_ref[sup, g * tq:(g + 1) * tq, 0:dh] = q_ref[:, g * dh:(g + 1) * dh]
            lhs_ref[sup, g * tq:(g + 1) * tq, dh:2 * dh] = bias_ref[:, sup * LANES:(sup + 1) * LANES]

    for sup in range(n_super):
        pl.when(s0 + tq > sup * LANES * SEL_BLOCK)(functools.partial(build_lhs, sup))
    m_ref[...] = jnp.full_like(m_ref, -jnp.inf)
    acc_ref[...] = jnp.zeros_like(acc_ref)

    def scores(kt, sc_ref):
        k0 = pl.multiple_of(kt * tk, tk)
        k = ka_ref[pl.ds(k0, tk), :]
        sup = kt // tiles_per_super
        for g in range(NSA_GROUP):
            rows = slice(g * tq, (g + 1) * tq)
            sc_ref[rows, :] = _nt_dot(lhs_ref[sup, rows, :], k)

    def accumulate(online, kt, sc_ref, diagonal):
        k0 = pl.multiple_of(kt * tk, tk)
        v = va_ref[pl.ds(k0, tk), :]
        for g in range(NSA_GROUP):
            rows = slice(g * tq, (g + 1) * tq)
            sc = sc_ref[rows, :]
            if diagonal:
                row = lax.broadcasted_iota(jnp.int32, sc.shape, 0)
                col = lax.broadcasted_iota(jnp.int32, sc.shape, 1)
                sc = jnp.where(k0 + col <= s0 + row, sc, NEG)
            if online:
                m_old = m_ref[rows, :]
                m_new = jnp.maximum(m_old, jnp.max(sc, axis=-1, keepdims=True))
                alpha = jnp.exp(m_old - m_new)
                pr = jnp.exp(sc - jnp.tile(m_new, (1, tk // LANES)))
                acc_ref[rows, :] = (jnp.tile(alpha, (1, 2)) * acc_ref[rows, :]
                                    + jnp.dot(pr.astype(BF16), v, preferred_element_type=F32))
                m_ref[rows, :] = m_new
            else:
                acc_ref[rows, :] += jnp.dot(jnp.exp(sc).astype(BF16), v, preferred_element_type=F32)

    n_full = s0 // tk

    def run(online):
        acc_fn = functools.partial(accumulate, online)

        def body(j, carry):
            scores(2 * j + 1, sb_ref)
            acc_fn(2 * j, sa_ref, False)
            scores(2 * j + 2, sa_ref)
            acc_fn(2 * j + 1, sb_ref, False)
            return carry

        scores(0, sa_ref)
        lax.fori_loop(0, n_full // 2, body, 0)

        @pl.when(n_full % 2 == 0)
        def _():
            acc_fn(n_full, sa_ref, True)

        @pl.when(n_full % 2 == 1)
        def _():
            scores(n_full, sb_ref)
            acc_fn(n_full - 1, sa_ref, False)
            acc_fn(n_full, sb_ref, True)

    pl.when(fixed)(functools.partial(run, False))
    pl.when(jnp.logical_not(fixed))(functools.partial(run, True))

    acc = acc_ref[...]
    o = acc[:, :dh] * (_branch_gate(misc_ref[...], 1, tq) / acc[:, dh:])
    o_ref[...] = prev_ref[...] + _unstack_heads(o, tq)


def _sel_attn(qn, bias, ksa, vsa, p, prev, gq, gk, tq, tk):
    s = qn.shape[0]
    dh = NSA_HEAD_DIM
    gw = NSA_GROUP * dh
    nbp = bias.shape[1] // NSA_KV_HEADS
    n_super = nbp // LANES
    assert tk % tq == 0 and (LANES * SEL_BLOCK) % tk == 0 and s % tk == 0
    resident = lambda: pl.BlockSpec((s, 2 * dh), lambda h, i: (0, h), pipeline_mode=pl.Buffered(1))
    gain = pl.BlockSpec((1, dh), lambda h, i: (0, 0))
    return pl.pallas_call(
        functools.partial(_sel_attn_kernel, tk=tk),
        grid=(NSA_KV_HEADS, s // tq),
        in_specs=[pl.BlockSpec((tq, gw), lambda h, i: (i, h)),
                  pl.BlockSpec((tq, nbp), lambda h, i: (i, h)),
                  resident(), resident(),
                  pl.BlockSpec((tq, LANES), lambda h, i: (i, C_MISC // LANES + h)),
                  pl.BlockSpec((tq, gw), lambda h, i: (i, h)),
                  gain, gain],
        out_specs=pl.BlockSpec((tq, gw), lambda h, i: (i, h)),
        out_shape=jax.ShapeDtypeStruct((s, NSA_QW), F32),
        scratch_shapes=[pltpu.VMEM((n_super, NSA_GROUP * tq, 2 * dh), BF16),
                        pltpu.VMEM((NSA_GROUP * tq, LANES), F32),
                        pltpu.VMEM((NSA_GROUP * tq, 2 * dh), F32),
                        pltpu.VMEM((NSA_GROUP * tq, tk), F32),
                        pltpu.VMEM((NSA_GROUP * tq, tk), F32)],
        compiler_params=_params(("arbitrary", "arbitrary")),
        name="sel_attn",
    )(qn, bias, ksa, vsa, p, prev, gq, gk)


def _out_proj_kernel(x_ref, a_ref, b_ref, wa_ref, wb_ref, o_ref):
    o_ref[...] = (x_ref[...]
                  + jnp.dot(a_ref[...].astype(BF16), wa_ref[...], preferred_element_type=F32)
                  + jnp.dot(b_ref[...].astype(BF16), wb_ref[...], preferred_element_type=F32))


def _out_proj(x, a, b, w, tm):
    s, d = x.shape
    ka, kb = a.shape[1], b.shape[1]
    assert ka == kb
    return pl.pallas_call(
        _out_proj_kernel,
        grid=(s // tm,),
        in_specs=[pl.BlockSpec((tm, d), lambda i: (i, 0)),
                  pl.BlockSpec((tm, ka), lambda i: (i, 0)),
                  pl.BlockSpec((tm, kb), lambda i: (i, 0)),
                  pl.BlockSpec((ka, d), lambda i: (0, 0)),
                  pl.BlockSpec((kb, d), lambda i: (1, 0))],
        out_specs=pl.BlockSpec((tm, d), lambda i: (i, 0)),
        out_shape=jax.ShapeDtypeStruct((s, d), F32),
        compiler_params=_params(("arbitrary",)),
        name="out_proj",
    )(x, a, b, w, w)


def _ffn_kernel(h_ref, g_ref, wg_ref, wu_ref, wd_ref, o_ref, hn_ref):
    @pl.when(pl.program_id(1) == 0)
    def _():
        h = h_ref[...]
        hn_ref[...] = _rms(h, g_ref[...]).astype(BF16)
        o_ref[...] = h

    hn = hn_ref[...]
    a = jnp.dot(hn, wg_ref[...].astype(BF16), preferred_element_type=F32)
    u = jnp.dot(hn, wu_ref[...].astype(BF16), preferred_element_type=F32)
    z = (a * _sigmoid(a) * u).astype(BF16)
    o_ref[...] += jnp.dot(z, wd_ref[...].astype(BF16), preferred_element_type=F32)


def _ffn(h, g, wg, wu, wd, tm, tf):
    s, d = h.shape
    f = wg.shape[1]
    return pl.pallas_call(
        _ffn_kernel,
        grid=(s // tm, f // tf),
        in_specs=[pl.BlockSpec((tm, d), lambda i, j: (i, 0)),
                  pl.BlockSpec((1, d), lambda i, j: (0, 0)),
                  pl.BlockSpec((d, tf), lambda i, j: (0, j)),
                  pl.BlockSpec((d, tf), lambda i, j: (0, j)),
                  pl.BlockSpec((tf, d), lambda i, j: (j, 0))],
        out_specs=pl.BlockSpec((tm, d), lambda i, j: (i, 0)),
        out_shape=jax.ShapeDtypeStruct((s, d), F32),
        scratch_shapes=[pltpu.VMEM((tm, d), BF16)],
        compiler_params=_params(("arbitrary", "arbitrary")),
        name="ffn",
    )(h, g, wg, wu, wd)


class _Tiles(NamedTuple):
    proj_rows: int
    gla_rows: int
    cmp_q: int
    win_q: int
    sel_q: int
    sel_k: int
    out_rows: int
    ffn_rows: int
    ffn_cols: int


def _tiles(s):
    if s >= 4096:
        return _Tiles(proj_rows=512, gla_rows=1024, cmp_q=256, win_q=256, sel_q=512, sel_k=1024,
                      out_rows=512, ffn_rows=1024, ffn_cols=256)
    return _Tiles(proj_rows=256, gla_rows=512, cmp_q=128, win_q=256, sel_q=256, sel_k=512,
                  out_rows=256, ffn_rows=256, ffn_cols=256)


_W_IN_LR = GLA_QKV
_W_IN_GO = _W_IN_LR + GLA_GATE_RANK
_W_IN_NG = _W_IN_GO + GLA_VW + NSA_QW + 6 * NSA_KVW


def _regroup_kernel(w_ref, misc_ref, o_ref):
    o_ref[0:GLA_QKV, :] = w_ref[0:GLA_QKV, :].astype(BF16)
    o_ref[GLA_QKV:C_MISC, :] = w_ref[_W_IN_GO:_W_IN_NG, :].astype(BF16)
    o_ref[C_MISC:PROJ_PAD, :] = misc_ref[...]


def _regroup_w_in(w_in, tile=256):
    d = w_in.shape[0]
    wt = w_in.T
    lr = wt[_W_IN_LR:_W_IN_GO]
    ng = wt[_W_IN_NG:_W_IN_NG + 3 * NSA_HEADS].reshape(NSA_KV_HEADS, NSA_GROUP, 3, d)
    ng = ng.transpose(0, 2, 1, 3).reshape(NSA_KV_HEADS, 3 * NSA_GROUP, d)
    pad = lambda n: jnp.zeros((n, d), w_in.dtype)
    tail = pad(LANES - GLA_GATE_RANK - 3 * NSA_GROUP)
    misc = jnp.concatenate([lr, ng[0], tail, pad(GLA_GATE_RANK), ng[1], tail], axis=0).astype(BF16)
    return pl.pallas_call(
        _regroup_kernel,
        grid=(d // tile,),
        in_specs=[pl.BlockSpec((wt.shape[0], tile), lambda i: (0, i)),
                  pl.BlockSpec((2 * LANES, tile), lambda i: (0, i))],
        out_specs=pl.BlockSpec((PROJ_PAD, tile), lambda i: (0, i)),
        out_shape=jax.ShapeDtypeStruct((PROJ_PAD, d), BF16),
        compiler_params=_params(("arbitrary",)),
        name="regroup_w_in",
    )(wt, misc)


def _overlap_t(s, nbp):
    n = s // CMP_STRIDE
    c0 = np.arange(n)[None, :] * CMP_STRIDE
    s0 = np.arange(nbp)[:, None] * SEL_BLOCK
    ov = np.clip(np.minimum(c0 + CMP_BLOCK, s0 + SEL_BLOCK) - np.maximum(c0, s0), 0, None) / CMP_STRIDE
    ov[:, n - 1] = 0.0
    ov[s // SEL_BLOCK:] = 0.0
    return jnp.asarray(ov, BF16)


def _layer(x, attn_norm_g, w_in, gla_conv_w, gla_gate_w2, gla_gate_b, gla_norm_g,
           nsa_q_norm_g, nsa_kc_norm_g, nsa_ks_norm_g, nsa_kw_norm_g,
           cmp_k_pos, cmp_k_w1, cmp_k_w2, cmp_v_pos, cmp_v_w1, cmp_v_w2,
           w_out, ffn_norm_g, w_gate, w_up, w_down):
    s = x.shape[0]
    dh = NSA_HEAD_DIM
    row = lambda v: v.reshape(1, -1)
    t = _tiles(s)

    p, qn, ksa, vsa, kwn, vwa = _in_proj(
        x, row(attn_norm_g), _regroup_w_in(w_in), row(nsa_q_norm_g), row(nsa_ks_norm_g),
        row(nsa_kw_norm_g), t.proj_rows, PROJ_PAD // 2)

    gla_out = _gla(p, gla_conv_w, gla_gate_w2, row(gla_gate_b), row(gla_norm_g), t.gla_rows)

    w1 = jnp.stack([cmp_k_w1, cmp_v_w1])
    w1s = (w1.reshape(2, 2, CMP_STRIDE, dh, CMP_HIDDEN).transpose(0, 2, 3, 1, 4)
           .reshape(2, CMP_STRIDE, dh, 2 * CMP_HIDDEN).astype(BF16))
    pos = jnp.stack([cmp_k_pos, cmp_v_pos]).reshape(2, 1, CMP_BLOCK * dh)
    w2 = jnp.stack([cmp_k_w2, cmp_v_w2]).astype(BF16)
    cmp, cmp_a = _compress(p, w1s, pos, w1, w2, row(nsa_kc_norm_g))

    nb = s // SEL_BLOCK
    nbp = -(-nb // LANES) * LANES
    gq = row(nsa_q_norm_g)
    gks = row(nsa_ks_norm_g)
    o_cmp, bias = _cmp_attn(qn, cmp, cmp_a, _overlap_t(s, nbp), p, gq, row(nsa_kc_norm_g), gks,
                            t.cmp_q, min(SEL_TOPK, nb))
    o_cw = _win_attn(qn, kwn, vwa, p, o_cmp, gq, row(nsa_kw_norm_g), t.win_q)
    nsa_out = _sel_attn(qn, bias, ksa, vsa, p, o_cw, gq, gks, t.sel_q, t.sel_k)

    h = _out_proj(x, gla_out, nsa_out, w_out.astype(BF16), t.out_rows)
    return _ffn(h, row(ffn_norm_g), w_gate, w_up, w_down, t.ffn_rows, t.ffn_cols)


def kernel(x, attn_norm_g, w_in, gla_conv_w, gla_gate_w2, gla_gate_b, gla_norm_g, nsa_q_norm_g, nsa_kc_norm_g, nsa_ks_norm_g, nsa_kw_norm_g, cmp_k_pos, cmp_k_w1, cmp_k_w2, cmp_v_pos, cmp_v_w1, cmp_v_w2, w_out, ffn_norm_g, w_gate, w_up, w_down):
    assert x.shape[0] == 1 and attn_norm_g.shape[0] == 1
    y = _layer(x[0], attn_norm_g[0], w_in[0], gla_conv_w[0], gla_gate_w2[0], gla_gate_b[0],
               gla_norm_g[0], nsa_q_norm_g[0], nsa_kc_norm_g[0], nsa_ks_norm_g[0], nsa_kw_norm_g[0],
               cmp_k_pos[0], cmp_k_w1[0], cmp_k_w2[0], cmp_v_pos[0], cmp_v_w1[0], cmp_v_w2[0],
               w_out[0], ffn_norm_g[0], w_gate[0], w_up[0], w_down[0])
    return y[None]
```

```python
import functools
from typing import NamedTuple

import jax
import jax.numpy as jnp
import numpy as np
from jax import lax
from jax.experimental import pallas as pl
from jax.experimental.pallas import tpu as pltpu

D_MODEL = 2048
GLA_HEADS = 4
GLA_DK = 128
GLA_DV = 256
GLA_GATE_RANK = 16
GLA_GATE_TAU = 16.0
GLA_CHUNK = 64
GLA_SUB = 16
GLA_SAFE_LOG_DECAY = 60.0
GLA_HEADS_PER_STEP = 2
SOFTMAX_SHIFT_LIMIT = 40.0
CONV_WIDTH = 4

NSA_HEADS = 8
NSA_KV_HEADS = 2
NSA_GROUP = 4
NSA_HEAD_DIM = 128
CMP_BLOCK = 32
CMP_STRIDE = 16
CMP_HIDDEN = 128
SEL_BLOCK = 64
SEL_TOPK = 16
WINDOW = 512
D_FF = 5632
EPS = 1e-6
NEG = -1e30

GLA_QK = GLA_HEADS * GLA_DK
GLA_VW = GLA_HEADS * GLA_DV
GLA_QKV = 2 * GLA_QK + GLA_VW
NSA_QW = NSA_HEADS * NSA_HEAD_DIM
NSA_KVW = NSA_KV_HEADS * NSA_HEAD_DIM

LANES = 128
VMEM_LIMIT = 56 * 1024 * 1024

C_GQ = 0
C_GK = GLA_QK
C_GV = 2 * GLA_QK
C_GO = GLA_QKV
C_NQ = C_GO + GLA_VW
C_KC = C_NQ + NSA_QW
C_VC = C_KC + NSA_KVW
C_KS = C_VC + NSA_KVW
C_VS = C_KS + NSA_KVW
C_KW = C_VS + NSA_KVW
C_VW = C_KW + NSA_KVW
C_MISC = C_VW + NSA_KVW
PROJ_PAD = C_MISC + 2 * LANES
GATE_LANE = GLA_GATE_RANK

F32 = jnp.float32
BF16 = jnp.bfloat16


def _params(sem):
    return pltpu.CompilerParams(dimension_semantics=sem, vmem_limit_bytes=VMEM_LIMIT)


def _nt_dot(a, b):
    return lax.dot_general(a, b, (((1,), (1,)), ((), ())), preferred_element_type=F32)


def _sigmoid(x):
    return 1.0 / (1.0 + jnp.exp(-x))


def _rms(x, g):
    return x * lax.rsqrt(jnp.mean(x * x, axis=-1, keepdims=True) + EPS) * g


def _in_proj_kernel(x_ref, g_ref, w_ref, qg_ref, ksg_ref, kwg_ref,
                    o_ref, qn_ref, ksa_ref, vsa_ref, kwn_ref, vwa_ref, xn_ref):
    t_rows, tn = o_ref.shape
    j = pl.program_id(1)

    @pl.when(j == 0)
    def _():
        xn_ref[...] = _rms(x_ref[...], g_ref[...]).astype(BF16)

    o_ref[...] = _nt_dot(xn_ref[...], w_ref[...])

    base = PROJ_PAD - tn
    assert base <= C_NQ

    @pl.when(j == pl.num_programs(1) - 1)
    def _():
        dh = NSA_HEAD_DIM
        col = lambda c, w=dh: o_ref[:, c - base:c - base + w]
        ones = jnp.ones((t_rows, dh), BF16)
        for h in range(NSA_HEADS):
            qn_ref[:, h * dh:(h + 1) * dh] = (
                _rms(col(C_NQ + h * dh), qg_ref[...]) * (dh ** -0.5)).astype(BF16)
        pos = pl.program_id(0) * t_rows + lax.broadcasted_iota(jnp.int32, (t_rows, LANES), 0)
        lane = lax.broadcasted_iota(jnp.int32, (t_rows, LANES), 1)
        onehot = jnp.where(lane == ((pos // SEL_BLOCK) & (LANES - 1)), 1.0, 0.0).astype(BF16)
        for h in range(NSA_KV_HEADS):
            lo, hi = 2 * h * dh, (2 * h + 1) * dh
            ksa_ref[:, lo:hi] = _rms(col(C_KS + h * dh), ksg_ref[...]).astype(BF16)
            ksa_ref[:, hi:hi + dh] = onehot
            kwn_ref[:, h * dh:(h + 1) * dh] = _rms(col(C_KW + h * dh), kwg_ref[...]).astype(BF16)
            vsa_ref[:, lo:hi] = col(C_VS + h * dh).astype(BF16)
            vsa_ref[:, hi:hi + dh] = ones
            vwa_ref[:, lo:hi] = col(C_VW + h * dh).astype(BF16)
            vwa_ref[:, hi:hi + dh] = ones


def _in_proj(x, g, wt, qg, ksg, kwg, tm, tn):
    s, d = x.shape
    n = wt.shape[0]
    kv = NSA_KVW
    gain = pl.BlockSpec((1, NSA_HEAD_DIM), lambda i, j: (0, 0))
    rows = lambda w: pl.BlockSpec((tm, w), lambda i, j: (i, 0))
    return pl.pallas_call(
        _in_proj_kernel,
        grid=(s // tm, n // tn),
        in_specs=[pl.BlockSpec((tm, d), lambda i, j: (i, 0)),
                  pl.BlockSpec((1, d), lambda i, j: (0, 0)),
                  pl.BlockSpec((tn, d), lambda i, j: (j, 0)),
                  gain, gain, gain],
        out_specs=[pl.BlockSpec((tm, tn), lambda i, j: (i, j)),
                   rows(NSA_QW), rows(2 * kv), rows(2 * kv), rows(kv), rows(2 * kv)],
        out_shape=[jax.ShapeDtypeStruct((s, n), F32),
                   jax.ShapeDtypeStruct((s, NSA_QW), BF16),
                   jax.ShapeDtypeStruct((s, 2 * kv), BF16),
                   jax.ShapeDtypeStruct((s, 2 * kv), BF16),
                   jax.ShapeDtypeStruct((s, kv), BF16),
                   jax.ShapeDtypeStruct((s, 2 * kv), BF16)],
        scratch_shapes=[pltpu.VMEM((tm, d), BF16)],
        compiler_params=_params(("arbitrary", "arbitrary")),
        name="in_proj",
    )(x, g, wt, qg, ksg, kwg)


_GLA_HEAD_OPERANDS = 12


def _gla_kernel(*refs):
    nh = GLA_HEADS_PER_STEP
    head_refs = [refs[h * _GLA_HEAD_OPERANDS:(h + 1) * _GLA_HEAD_OPERANDS] for h in range(nh)]
    lr_ref, ng_ref, o_ref, st_ref, sq_ref, sk_ref, sv_ref = refs[nh * _GLA_HEAD_OPERANDS:]
    q_ref = head_refs[0][0]
    t_rows = q_ref.shape[0]
    pad = GLA_SUB
    first = pl.program_id(1) == 0

    @pl.when(first)
    def _():
        st_ref[...] = jnp.zeros_like(st_ref)

    def conv_silu(u_ref, p_ref, w_ref, s_ref):
        w = w_ref[...]
        s_ref[0:8, :] = jnp.where(first, 0.0, p_ref[...])
        s_ref[8:16, :] = u_ref[0:8, :]
        head = u_ref[0:8, :] * w[CONV_WIDTH - 1:CONV_WIDTH]
        body = u_ref[8:, :] * w[CONV_WIDTH - 1:CONV_WIDTH]
        for d in range(1, CONV_WIDTH):
            wd = w[CONV_WIDTH - 1 - d:CONV_WIDTH - d]
            head = head + s_ref[pl.ds(8 - d, 8), :] * wd
            body = body + u_ref[pl.ds(8 - d, t_rows - 8), :] * wd
        acc = jnp.concatenate([head, body], axis=0)
        return acc * _sigmoid(acc)

    row = lax.broadcasted_iota(jnp.int32, (t_rows, GLA_DK), 0)
    lane = lax.broadcasted_iota(jnp.int32, (t_rows, GLA_DK), 1)
    rc = row & (GLA_CHUNK - 1)

    def front(hh):
        hq, hqp, hk, hkp, hv, hvp, _, cwq, cwk, cwv, w2_ref, gb_ref = head_refs[hh]
        q = conv_silu(hq, hqp, cwq, sq_ref.at[hh])
        k = conv_silu(hk, hkp, cwk, sk_ref.at[hh])
        v = conv_silu(hv, hvp, cwv, sv_ref.at[hh])
        z = jnp.dot(lr_ref[:, :GLA_GATE_RANK], w2_ref[...], preferred_element_type=F32,
                    precision=lax.Precision.HIGHEST) + gb_ref[...]
        b = (jnp.minimum(z, 0.0) - jnp.log(1.0 + jnp.exp(-jnp.abs(z)))) * (1.0 / GLA_GATE_TAU)
        d = 1
        while d < GLA_CHUNK:
            b = b + jnp.where(rc >= d, pltpu.roll(b, d, 0), 0.0)
            d *= 2
        return q * (GLA_DK ** -0.5), k, v, b

    heads = [front(hh) for hh in range(GLA_HEADS_PER_STEP)]

    srow = lax.broadcasted_iota(jnp.int32, (GLA_CHUNK, GLA_DK), 0)
    scol = lax.broadcasted_iota(jnp.int32, (GLA_CHUNK, GLA_DK), 1)
    zk = jnp.zeros((LANES - GLA_CHUNK, GLA_DK), BF16)
    zv = jnp.zeros((LANES - GLA_CHUNK, GLA_DV), BF16)
    za = jnp.zeros((GLA_SUB, LANES), F32)
    gain = ng_ref[...]

    def chunk_loop(intra_scores):
        for c in range(t_rows // GLA_CHUNK):
            lo = c * GLA_CHUNK
            for hh, (q, k, v, b) in enumerate(heads):
                bc = b[lo:lo + GLA_CHUNK]
                qc = q[lo:lo + GLA_CHUNK]
                kc = k[lo:lo + GLA_CHUNK]
                vc = v[lo:lo + GLA_CHUNK].astype(BF16)
                st = st_ref[hh]
                b_last = bc[GLA_CHUNK - 1:GLA_CHUNK]
                qd = (qc * jnp.exp(bc)).astype(BF16)
                scores = intra_scores(hh, lo, bc, qc, kc, qd)
                o = _nt_dot(qd, st.astype(BF16)) + jnp.dot(
                    scores.astype(BF16), jnp.concatenate([vc, zv], axis=0), preferred_element_type=F32)
                go = head_refs[hh][6][lo:lo + GLA_CHUNK, :]
                o_ref[lo:lo + GLA_CHUNK, hh * GLA_DV:(hh + 1) * GLA_DV] = _rms(o, gain) * (go * _sigmoid(go))
                kt = (kc * jnp.exp(b_last - bc)).astype(BF16)
                upd = lax.dot_general(vc, kt, (((0,), (0,)), ((), ())), preferred_element_type=F32)
                st_ref[hh] = st * jnp.exp(b_last) + upd

    b_min = heads[0][3]
    for hd in heads[1:]:
        b_min = jnp.minimum(b_min, hd[3])
    small_decay = jnp.min(b_min) >= -GLA_SAFE_LOG_DECAY

    @pl.when(small_decay)
    def _():
        def intra_scores(hh, lo, bc, qc, kc, qd):
            kd = (kc * jnp.exp(-bc)).astype(BF16)
            a = _nt_dot(qd, jnp.concatenate([kd, zk], axis=0))
            return jnp.where(scol <= srow, a, 0.0)

        chunk_loop(intra_scores)

    @pl.when(jnp.logical_not(small_decay))
    def _():
        zero_pad = jnp.zeros((pad, GLA_DK), F32)
        rs = row & (GLA_SUB - 1)
        dl = rc - lane
        bands = []
        for hh, (q, k, v, b) in enumerate(heads):
            sq_ref[hh, 0:pad, :] = zero_pad
            sk_ref[hh, 0:pad, :] = zero_pad
            sq_ref[hh, pad:, :] = b
            sk_ref[hh, pad:, :] = k
            band = jnp.where(dl == 0, jnp.sum(q * k, axis=-1, keepdims=True), 0.0)
            for delta in range(1, GLA_SUB):
                e = jnp.exp(jnp.where(rs >= delta, b - sq_ref[hh, pl.ds(pad - delta, t_rows), :], -jnp.inf))
                sc = jnp.sum(q * sk_ref[hh, pl.ds(pad - delta, t_rows), :] * e, axis=-1, keepdims=True)
                band = jnp.where(dl == delta, sc, band)
            bands.append(band)

        def intra_scores(hh, lo, bc, qc, kc, qd):
            blocks = [za]
            for sub in range(1, GLA_CHUNK // GLA_SUB):
                r0 = sub * GLA_SUB
                ref_b = bc[r0:r0 + 1]
                qq = (qc[r0:r0 + GLA_SUB] * jnp.exp(bc[r0:r0 + GLA_SUB] - ref_b)).astype(BF16)
                kk = (kc * jnp.exp(jnp.where(srow < r0, ref_b - bc, -jnp.inf))).astype(BF16)
                blocks.append(_nt_dot(qq, jnp.concatenate([kk, zk], axis=0)))
            return bands[hh][lo:lo + GLA_CHUNK] + jnp.concatenate(blocks, axis=0)

        chunk_loop(intra_scores)


def _gla(p, conv_w, w2, gb, ng, tile):
    s = p.shape[0]
    t8 = tile // 8
    nh = GLA_HEADS_PER_STEP
    dk, dv = GLA_DK, GLA_DV

    def head_specs(hh):
        head = lambda h: h * nh + hh
        qcol = lambda h: C_GQ // dk + head(h)
        kcol = lambda h: C_GK // dk + head(h)
        vcol = lambda h: C_GV // dv + head(h)
        prev = lambda col: (lambda h, i: (jnp.maximum(i * t8 - 1, 0), col(h)))
        return [
            pl.BlockSpec((tile, dk), lambda h, i: (i, qcol(h))),
            pl.BlockSpec((8, dk), prev(qcol)),
            pl.BlockSpec((tile, dk), lambda h, i: (i, kcol(h))),
            pl.BlockSpec((8, dk), prev(kcol)),
            pl.BlockSpec((tile, dv), lambda h, i: (i, vcol(h))),
            pl.BlockSpec((8, dv), prev(vcol)),
            pl.BlockSpec((tile, dv), lambda h, i: (i, C_GO // dv + head(h))),
            pl.BlockSpec((CONV_WIDTH, dk), lambda h, i: (0, qcol(h))),
            pl.BlockSpec((CONV_WIDTH, dk), lambda h, i: (0, kcol(h))),
            pl.BlockSpec((CONV_WIDTH, dv), lambda h, i: (0, vcol(h))),
            pl.BlockSpec((GLA_GATE_RANK, dk), lambda h, i: (0, head(h))),
            pl.BlockSpec((1, dk), lambda h, i: (0, head(h))),
        ]

    head_args = [p, p, p, p, p, p, p, conv_w, conv_w, conv_w, w2, gb]
    assert len(head_args) == _GLA_HEAD_OPERANDS
    return pl.pallas_call(
        _gla_kernel,
        grid=(GLA_HEADS // nh, s // tile),
        in_specs=[spec for hh in range(nh) for spec in head_specs(hh)] + [
            pl.BlockSpec((tile, LANES), lambda h, i: (i, C_MISC // LANES)),
            pl.BlockSpec((1, GLA_DV), lambda h, i: (0, 0)),
        ],
        out_specs=pl.BlockSpec((tile, nh * dv), lambda h, i: (i, h)),
        out_shape=jax.ShapeDtypeStruct((s, GLA_VW), F32),
        scratch_shapes=[pltpu.VMEM((nh, GLA_DV, GLA_DK), F32),
                        pltpu.VMEM((nh, tile + GLA_SUB, GLA_DK), F32),
                        pltpu.VMEM((nh, tile + GLA_SUB, GLA_DK), F32),
                        pltpu.VMEM((nh, GLA_SUB, GLA_DV), F32)],
        compiler_params=_params(("arbitrary", "arbitrary")),
        name="gla",
    )(*(head_args * nh), p, ng)


def _compress_kernel(u_ref, w1_ref, pos_ref, w1f_ref, w2_ref, g_ref, o_ref, oa_ref):
    n = o_ref.shape[1]
    half = CMP_BLOCK // CMP_STRIDE
    assert half == 2
    acc = jnp.zeros((n, 2 * CMP_HIDDEN), F32)
    for l in range(CMP_STRIDE):
        x = u_ref[pl.ds(l, n, stride=CMP_STRIDE), :].astype(BF16)
        acc = acc + jnp.dot(x, w1_ref[0, l], preferred_element_type=F32)
    posb = jnp.dot(jnp.broadcast_to(pos_ref[0], (8, pos_ref.shape[2])), w1f_ref[0],
                   preferred_element_type=F32, precision=lax.Precision.HIGHEST)[0:1]
    hid = acc[:, :CMP_HIDDEN] + pltpu.roll(acc[:, CMP_HIDDEN:], n - 1, 0) + posb
    hid = hid * _sigmoid(hid)
    out = jnp.dot(hid.astype(BF16), w2_ref[0], preferred_element_type=F32)
    is_k = pl.program_id(0) < NSA_KV_HEADS
    out = jnp.where(is_k, _rms(out, g_ref[...]), out)
    row = lax.broadcasted_iota(jnp.int32, out.shape, 0)
    out = jnp.where(row < n - 1, out, 0.0).astype(BF16)
    o_ref[0] = out
    oa_ref[0] = jnp.concatenate([out, jnp.ones_like(out)], axis=1)


def _compress(p, w1s, pos, w1f, w2, g):
    s = p.shape[0]
    n = s // CMP_STRIDE
    dh = NSA_HEAD_DIM
    return pl.pallas_call(
        _compress_kernel,
        grid=(2 * NSA_KV_HEADS,),
        in_specs=[pl.BlockSpec((s, dh), lambda j: (0, C_KC // dh + j)),
                  pl.BlockSpec((1, CMP_STRIDE, dh, 2 * CMP_HIDDEN), lambda j: (j // 2, 0, 0, 0)),
                  pl.BlockSpec((1, 1, CMP_BLOCK * dh), lambda j: (j // 2, 0, 0)),
                  pl.BlockSpec((1, CMP_BLOCK * dh, CMP_HIDDEN), lambda j: (j // 2, 0, 0)),
                  pl.BlockSpec((1, CMP_HIDDEN, dh), lambda j: (j // 2, 0, 0)),
                  pl.BlockSpec((1, dh), lambda j: (0, 0))],
        out_specs=[pl.BlockSpec((1, n, dh), lambda j: (j, 0, 0)),
                   pl.BlockSpec((1, n, 2 * dh), lambda j: (j, 0, 0))],
        out_shape=[jax.ShapeDtypeStruct((2 * NSA_KV_HEADS, n, dh), BF16),
                   jax.ShapeDtypeStruct((2 * NSA_KV_HEADS, n, 2 * dh), BF16)],
        compiler_params=_params(("arbitrary",)),
        name="compress",
    )(p, w1s, pos, w1f, w2, g)


def _softmax_shift(gq_ref, gk_ref):
    bound = (NSA_HEAD_DIM ** 0.5) * jnp.max(jnp.abs(gq_ref[...])) * jnp.max(jnp.abs(gk_ref[...]))
    usable = bound <= SOFTMAX_SHIFT_LIMIT
    return usable, jnp.where(usable, bound, 0.0)


def _unstack_heads(o, tq):
    return jnp.concatenate([o[g * tq:(g + 1) * tq] for g in range(NSA_GROUP)], axis=1)


def _branch_gate(misc, branch, tq):
    sg = _sigmoid(misc)
    base = GATE_LANE + branch * NSA_GROUP
    return jnp.concatenate([sg[:, base + g:base + g + 1] for g in range(NSA_GROUP)], axis=0)


def _cmp_attn_kernel(q_ref, kc_ref, va_ref, ovl_ref, misc_ref, gq_ref, gk_ref, gks_ref, o_ref, bias_ref,
                     e_ref, *, n_sel):
    tq = q_ref.shape[0]
    dh = NSA_HEAD_DIM
    nc = kc_ref.shape[1]
    nbp = ovl_ref.shape[0]
    i = pl.program_id(0)
    s0 = i * tq
    col_step = max(LANES, nc // 4)
    n_var = nc // col_step
    tiles_per_var = (nc * CMP_STRIDE // tq) // n_var
    assert tiles_per_var * tq == col_step * CMP_STRIDE

    fixed, shift = _softmax_shift(gq_ref, gk_ref)
    _, sel_shift = _softmax_shift(gq_ref, gks_ref)

    def variant(ncv, nbv, online):
        q = q_ref[...]
        row = lax.broadcasted_iota(jnp.int32, (tq, ncv), 0)
        col = lax.broadcasted_iota(jnp.int32, (tq, ncv), 1)
        maskb = jnp.where(col * CMP_STRIDE + (CMP_BLOCK - 1) <= s0 + row, -shift, NEG)
        has_valid = (s0 + lax.broadcasted_iota(jnp.int32, (tq, 1), 0)) >= CMP_BLOCK - 1
        sg = _sigmoid(misc_ref[...])
        ovl = ovl_ref[0:nbv, 0:ncv]
        imps = []
        for h in range(NSA_KV_HEADS):
            kc = kc_ref[h, 0:ncv, :]
            va = va_ref[h, 0:ncv, :]
            invs = []
            for g in range(NSA_GROUP):
                hd = h * NSA_GROUP + g
                sm = _nt_dot(q[:, hd * dh:(hd + 1) * dh], kc) + maskb
                if online:
                    sm = sm - jnp.max(sm, axis=-1, keepdims=True)
                e = jnp.exp(sm)
                e_ref[hd, :, 0:ncv] = e
                acc = jnp.dot(e.astype(BF16), va, preferred_element_type=F32)
                inv = jnp.where(has_valid, 1.0 / acc[:, dh:], 0.0)
                gl = h * LANES + GATE_LANE + g
                o_ref[:, hd * dh:(hd + 1) * dh] = acc[:, :dh] * inv * sg[:, gl:gl + 1]
                invs.append(inv)
            ps = None
            for g in range(NSA_GROUP):
                pc = e_ref[h * NSA_GROUP + g, :, 0:ncv] * jnp.tile(invs[g], (1, ncv // LANES))
                ps = pc if ps is None else ps + pc
            hi = ps.astype(BF16)
            r1 = ps - hi.astype(F32)
            mid = r1.astype(BF16)
            lo = (r1 - mid.astype(F32)).astype(BF16)
            imps.append(_nt_dot(ovl, hi) + _nt_dot(ovl, mid) + _nt_dot(ovl, lo))

        jb = lax.broadcasted_iota(jnp.int32, (nbv, tq), 0)
        jt = (s0 + lax.broadcasted_iota(jnp.int32, (nbv, tq), 1)) // SEL_BLOCK
        forced = (jb == 0) | (jb == jt) | (jb == jt - 1)
        future = jb > jt
        curs = [jnp.where(future | forced, -jnp.inf, imp) for imp in imps]
        for _ in range(n_sel - 3):
            for h in range(NSA_KV_HEADS):
                mx = jnp.max(curs[h], axis=0, keepdims=True)
                idx = jnp.min(jnp.where(curs[h] == mx, jb, nbv), axis=0, keepdims=True)
                curs[h] = jnp.where(jb == idx, -jnp.inf, curs[h])
        for h in range(NSA_KV_HEADS):
            bias_t = jnp.where((curs[h] == -jnp.inf) & jnp.logical_not(future), -sel_shift, NEG)
            bias_ref[:, h * nbp:h * nbp + nbv] = bias_t.T.astype(BF16)
            if nbv < nbp:
                bias_ref[:, h * nbp + nbv:(h + 1) * nbp] = jnp.full((tq, nbp - nbv), NEG, BF16)

    for v in range(n_var):
        ncv = col_step * (v + 1)
        nbv = min(nbp, -(-(ncv // (SEL_BLOCK // CMP_STRIDE)) // LANES) * LANES)
        pl.when(fixed & (i // tiles_per_var == v))(functools.partial(variant, ncv, nbv, False))
    pl.when(jnp.logical_not(fixed))(functools.partial(variant, nc, nbp, True))


def _cmp_attn(qn, cmp, cmp_a, ovl_t, p, gq, gk, gks, tq, n_sel):
    s = qn.shape[0]
    nc = cmp.shape[1]
    nbp = ovl_t.shape[0]
    assert n_sel >= 3
    misc_w = NSA_KV_HEADS * LANES
    gain = pl.BlockSpec((1, NSA_HEAD_DIM), lambda i: (0, 0))
    return pl.pallas_call(
        functools.partial(_cmp_attn_kernel, n_sel=n_sel),
        grid=(s // tq,),
        in_specs=[pl.BlockSpec((tq, NSA_QW), lambda i: (i, 0)),
                  pl.BlockSpec((NSA_KV_HEADS, nc, NSA_HEAD_DIM), lambda i: (0, 0, 0)),
                  pl.BlockSpec((NSA_KV_HEADS, nc, 2 * NSA_HEAD_DIM), lambda i: (1, 0, 0)),
                  pl.BlockSpec((nbp, nc), lambda i: (0, 0)),
                  pl.BlockSpec((tq, misc_w), lambda i: (i, C_MISC // misc_w)),
                  gain, gain, gain],
        out_specs=[pl.BlockSpec((tq, NSA_QW), lambda i: (i, 0)),
                   pl.BlockSpec((tq, NSA_KV_HEADS * nbp), lambda i: (i, 0))],
        out_shape=[jax.ShapeDtypeStruct((s, NSA_QW), F32),
                   jax.ShapeDtypeStruct((s, NSA_KV_HEADS * nbp), BF16)],
        scratch_shapes=[pltpu.VMEM((NSA_HEADS, tq, nc), F32)],
        compiler_params=_params(("arbitrary",)),
        name="cmp_attn",
    )(qn, cmp, cmp_a, ovl_t, p, gq, gk, gks)


def _win_attn_kernel(q_ref, k_ref, v_ref, misc_ref, prev_ref, gq_ref, gk_ref, o_ref, *, span):
    tq = q_ref.shape[0]
    dh = NSA_HEAD_DIM
    s0 = pl.program_id(1) * tq
    start = pl.multiple_of(jnp.maximum(s0 - WINDOW, 0), tq)
    fixed, shift = _softmax_shift(gq_ref, gk_ref)
    row = lax.broadcasted_iota(jnp.int32, (tq, span), 0)
    col = lax.broadcasted_iota(jnp.int32, (tq, span), 1)
    dist = (s0 - start) + row - col
    maskb = jnp.where((dist >= 0) & (dist < WINDOW), -shift, NEG)

    def run(online):
        q = q_ref[...]
        k = k_ref[pl.ds(start, span), :]
        v = v_ref[pl.ds(start, span), :]
        sg = _sigmoid(misc_ref[...])
        base = GATE_LANE + 2 * NSA_GROUP
        for g in range(NSA_GROUP):
            sm = _nt_dot(q[:, g * dh:(g + 1) * dh], k) + maskb
            if online:
                sm = sm - jnp.max(sm, axis=-1, keepdims=True)
            acc = jnp.dot(jnp.exp(sm).astype(BF16), v, preferred_element_type=F32)
            o = acc[:, :dh] * (sg[:, base + g:base + g + 1] / acc[:, dh:])
            o_ref[:, g * dh:(g + 1) * dh] = prev_ref[:, g * dh:(g + 1) * dh] + o

    pl.when(fixed)(functools.partial(run, False))
    pl.when(jnp.logical_not(fixed))(functools.partial(run, True))


def _win_attn(qn, kwn, vwa, p, prev, gq, gk, tq):
    s = qn.shape[0]
    gw = NSA_GROUP * NSA_HEAD_DIM
    span = min(s, WINDOW + tq)
    gain = pl.BlockSpec((1, NSA_HEAD_DIM), lambda h, i: (0, 0))
    return pl.pallas_call(
        functools.partial(_win_attn_kernel, span=span),
        grid=(NSA_KV_HEADS, s // tq),
        in_specs=[pl.BlockSpec((tq, gw), lambda h, i: (i, h)),
                  pl.BlockSpec((s, NSA_HEAD_DIM), lambda h, i: (0, h)),
                  pl.BlockSpec((s, 2 * NSA_HEAD_DIM), lambda h, i: (0, h)),
                  pl.BlockSpec((tq, LANES), lambda h, i: (i, C_MISC // LANES + h)),
                  pl.BlockSpec((tq, gw), lambda h, i: (i, h)),
                  gain, gain],
        out_specs=pl.BlockSpec((tq, gw), lambda h, i: (i, h)),
        out_shape=jax.ShapeDtypeStruct((s, NSA_QW), F32),
        compiler_params=_params(("arbitrary", "arbitrary")),
        name="win_attn",
    )(qn, kwn, vwa, p, prev, gq, gk)


def _sel_attn_kernel(q_ref, bias_ref, ka_ref, va_ref, misc_ref, prev_ref, gq_ref, gk_ref, o_ref,
                     lhs_ref, m_ref, acc_ref, sa_ref, sb_ref, *, tk):
    tq = q_ref.shape[0]
    dh = NSA_HEAD_DIM
    n_super = lhs_ref.shape[0]
    tiles_per_super = LANES * SEL_BLOCK // tk
    s0 = pl.program_id(1) * tq
    fixed, _ = _softmax_shift(gq_ref, gk_ref)
    def build_lhs(sup):
        for g in range(NSA_GROUP):
            lhs_ref[sup, g * tq:(g + 1) * tq, 0:dh] = q_ref[:, g * dh:(g + 1) * dh]
            lhs_ref[sup, g * tq:(g + 1) * tq, dh:2 * dh] = bias_ref[:, sup * LANES:(sup + 1) * LANES]

    for sup in range(n_super):
        pl.when(s0 + tq > sup * LANES * SEL_BLOCK)(functools.partial(build_lhs, sup))
    m_ref[...] = jnp.full_like(m_ref, -jnp.inf)
    acc_ref[...] = jnp.zeros_like(acc_ref)

    def scores(kt, sc_ref):
        k0 = pl.multiple_of(kt * tk, tk)
        k = ka_ref[pl.ds(k0, tk), :]
        sup = kt // tiles_per_super
        for g in range(NSA_GROUP):
            rows = slice(g * tq, (g + 1) * tq)
            sc_ref[rows, :] = _nt_dot(lhs_ref[sup, rows, :], k)

    def accumulate(online, kt, sc_ref, diagonal):
        k0 = pl.multiple_of(kt * tk, tk)
        v = va_ref[pl.ds(k0, tk), :]
        for g in range(NSA_GROUP):
            rows = slice(g * tq, (g + 1) * tq)
            sc = sc_ref[rows, :]
            if diagonal:
                row = lax.broadcasted_iota(jnp.int32, sc.shape, 0)
                col = lax.broadcasted_iota(jnp.int32, sc.shape, 1)
                sc = jnp.where(k0 + col <= s0 + row, sc, NEG)
            if online:
                m_old = m_ref[rows, :]
                m_new = jnp.maximum(m_old, jnp.max(sc, axis=-1, keepdims=True))
                alpha = jnp.exp(m_old - m_new)
                pr = jnp.exp(sc - jnp.tile(m_new, (1, tk // LANES)))
                acc_ref[rows, :] = (jnp.tile(alpha, (1, 2)) * acc_ref[rows, :]
                                    + jnp.dot(pr.astype(BF16), v, preferred_element_type=F32))
                m_ref[rows, :] = m_new
            else:
                acc_ref[rows, :] += jnp.dot(jnp.exp(sc).astype(BF16), v, preferred_element_type=F32)

    n_full = s0 // tk

    def run(online):
        acc_fn = functools.partial(accumulate, online)

        def body(j, carry):
            scores(2 * j + 1, sb_ref)
            acc_fn(2 * j, sa_ref, False)
            scores(2 * j + 2, sa_ref)
            acc_fn(2 * j + 1, sb_ref, False)
            return carry

        scores(0, sa_ref)
        lax.fori_loop(0, n_full // 2, body, 0)

        @pl.when(n_full % 2 == 0)
        def _():
            acc_fn(n_full, sa_ref, True)

        @pl.when(n_full % 2 == 1)
        def _():
            scores(n_full, sb_ref)
            acc_fn(n_full - 1, sa_ref, False)
            acc_fn(n_full, sb_ref, True)

    pl.when(fixed)(functools.partial(run, False))
    pl.when(jnp.logical_not(fixed))(functools.partial(run, True))

    acc = acc_ref[...]
    o = acc[:, :dh] * (_branch_gate(misc_ref[...], 1, tq) / acc[:, dh:])
    o_ref[...] = prev_ref[...] + _unstack_heads(o, tq)


def _sel_attn(qn, bias, ksa, vsa, p, prev, gq, gk, tq, tk):
    s = qn.shape[0]
    dh = NSA_HEAD_DIM
    gw = NSA_GROUP * dh
    nbp = bias.shape[1] // NSA_KV_HEADS
    n_super = nbp // LANES
    assert tk % tq == 0 and (LANES * SEL_BLOCK) % tk == 0 and s % tk == 0
    resident = lambda: pl.BlockSpec((s, 2 * dh), lambda h, i: (0, h), pipeline_mode=pl.Buffered(1))
    gain = pl.BlockSpec((1, dh), lambda h, i: (0, 0))
    return pl.pallas_call(
        functools.partial(_sel_attn_kernel, tk=tk),
        grid=(NSA_KV_HEADS, s // tq),
        in_specs=[pl.BlockSpec((tq, gw), lambda h, i: (i, h)),
                  pl.BlockSpec((tq, nbp), lambda h, i: (i, h)),
                  resident(), resident(),
                  pl.BlockSpec((tq, LANES), lambda h, i: (i, C_MISC // LANES + h)),
                  pl.BlockSpec((tq, gw), lambda h, i: (i, h)),
                  gain, gain],
        out_specs=pl.BlockSpec((tq, gw), lambda h, i: (i, h)),
        out_shape=jax.ShapeDtypeStruct((s, NSA_QW), F32),
        scratch_shapes=[pltpu.VMEM((n_super, NSA_GROUP * tq, 2 * dh), BF16),
                        pltpu.VMEM((NSA_GROUP * tq, LANES), F32),
                        pltpu.VMEM((NSA_GROUP * tq, 2 * dh), F32),
                        pltpu.VMEM((NSA_GROUP * tq, tk), F32),
                        pltpu.VMEM((NSA_GROUP * tq, tk), F32)],
        compiler_params=_params(("arbitrary", "arbitrary")),
        name="sel_attn",
    )(qn, bias, ksa, vsa, p, prev, gq, gk)


def _out_proj_kernel(x_ref, a_ref, b_ref, w_ref, o_ref, wb_ref):
    ka = a_ref.shape[1]

    @pl.when(pl.program_id(0) == 0)
    def _():
        wb_ref[...] = w_ref[...].astype(BF16)

    o_ref[...] = (x_ref[...]
                  + jnp.dot(a_ref[...].astype(BF16), wb_ref[0:ka, :], preferred_element_type=F32)
                  + jnp.dot(b_ref[...].astype(BF16), wb_ref[ka:, :], preferred_element_type=F32))


def _out_proj(x, a, b, w, tm):
    s, d = x.shape
    ka, kb = a.shape[1], b.shape[1]
    assert w.shape == (ka + kb, d)
    return pl.pallas_call(
        _out_proj_kernel,
        grid=(s // tm,),
        in_specs=[pl.BlockSpec((tm, d), lambda i: (i, 0)),
                  pl.BlockSpec((tm, ka), lambda i: (i, 0)),
                  pl.BlockSpec((tm, kb), lambda i: (i, 0)),
                  pl.BlockSpec((ka + kb, d), lambda i: (0, 0), pipeline_mode=pl.Buffered(1))],
        out_specs=pl.BlockSpec((tm, d), lambda i: (i, 0)),
        out_shape=jax.ShapeDtypeStruct((s, d), F32),
        scratch_shapes=[pltpu.VMEM((ka + kb, d), BF16)],
        compiler_params=_params(("arbitrary",)),
        name="out_proj",
    )(x, a, b, w)


def _ffn_kernel(h_ref, g_ref, wg_ref, wu_ref, wd_ref, o_ref, hn_ref):
    @pl.when(pl.program_id(1) == 0)
    def _():
        h = h_ref[...]
        hn_ref[...] = _rms(h, g_ref[...]).astype(BF16)
        o_ref[...] = h

    hn = hn_ref[...]
    a = jnp.dot(hn, wg_ref[...].astype(BF16), preferred_element_type=F32)
    u = jnp.dot(hn, wu_ref[...].astype(BF16), preferred_element_type=F32)
    z = (a * _sigmoid(a) * u).astype(BF16)
    o_ref[...] += jnp.dot(z, wd_ref[...].astype(BF16), preferred_element_type=F32)


def _ffn(h, g, wg, wu, wd, tm, tf):
    s, d = h.shape
    f = wg.shape[1]
    return pl.pallas_call(
        _ffn_kernel,
        grid=(s // tm, f // tf),
        in_specs=[pl.BlockSpec((tm, d), lambda i, j: (i, 0)),
                  pl.BlockSpec((1, d), lambda i, j: (0, 0)),
                  pl.BlockSpec((d, tf), lambda i, j: (0, j)),
                  pl.BlockSpec((d, tf), lambda i, j: (0, j)),
                  pl.BlockSpec((tf, d), lambda i, j: (j, 0))],
        out_specs=pl.BlockSpec((tm, d), lambda i, j: (i, 0)),
        out_shape=jax.ShapeDtypeStruct((s, d), F32),
        scratch_shapes=[pltpu.VMEM((tm, d), BF16)],
        compiler_params=_params(("arbitrary", "arbitrary")),
        name="ffn",
    )(h, g, wg, wu, wd)


class _Tiles(NamedTuple):
    proj_rows: int
    gla_rows: int
    cmp_q: int
    win_q: int
    sel_q: int
    sel_k: int
    out_rows: int
    ffn_rows: int
    ffn_cols: int


def _tiles(s):
    if s >= 4096:
        return _Tiles(proj_rows=512, gla_rows=1024, cmp_q=256, win_q=256, sel_q=512, sel_k=1024,
                      out_rows=512, ffn_rows=1024, ffn_cols=256)
    return _Tiles(proj_rows=256, gla_rows=512, cmp_q=128, win_q=256, sel_q=256, sel_k=512,
                  out_rows=256, ffn_rows=256, ffn_cols=256)


_W_IN_LR = GLA_QKV
_W_IN_GO = _W_IN_LR + GLA_GATE_RANK
_W_IN_NG = _W_IN_GO + GLA_VW + NSA_QW + 6 * NSA_KVW


def _regroup_kernel(w_ref, misc_ref, o_ref):
    o_ref[0:GLA_QKV, :] = w_ref[0:GLA_QKV, :].astype(BF16)
    o_ref[GLA_QKV:C_MISC, :] = w_ref[_W_IN_GO:_W_IN_NG, :].astype(BF16)
    o_ref[C_MISC:PROJ_PAD, :] = misc_ref[...]


def _regroup_w_in(w_in, tile=256):
    d = w_in.shape[0]
    wt = w_in.T
    lr = wt[_W_IN_LR:_W_IN_GO]
    ng = wt[_W_IN_NG:_W_IN_NG + 3 * NSA_HEADS].reshape(NSA_KV_HEADS, NSA_GROUP, 3, d)
    ng = ng.transpose(0, 2, 1, 3).reshape(NSA_KV_HEADS, 3 * NSA_GROUP, d)
    pad = lambda n: jnp.zeros((n, d), w_in.dtype)
    tail = pad(LANES - GLA_GATE_RANK - 3 * NSA_GROUP)
    misc = jnp.concatenate([lr, ng[0], tail, pad(GLA_GATE_RANK), ng[1], tail], axis=0).astype(BF16)
    return pl.pallas_call(
        _regroup_kernel,
        grid=(d // tile,),
        in_specs=[pl.BlockSpec((wt.shape[0], tile), lambda i: (0, i)),
                  pl.BlockSpec((2 * LANES, tile), lambda i: (0, i))],
        out_specs=pl.BlockSpec((PROJ_PAD, tile), lambda i: (0, i)),
        out_shape=jax.ShapeDtypeStruct((PROJ_PAD, d), BF16),
        compiler_params=_params(("arbitrary",)),
        name="regroup_w_in",
    )(wt, misc)


def _overlap_t(s, nbp):
    n = s // CMP_STRIDE
    c0 = np.arange(n)[None, :] * CMP_STRIDE
    s0 = np.arange(nbp)[:, None] * SEL_BLOCK
    ov = np.clip(np.minimum(c0 + CMP_BLOCK, s0 + SEL_BLOCK) - np.maximum(c0, s0), 0, None) / CMP_STRIDE
    ov[:, n - 1] = 0.0
    ov[s // SEL_BLOCK:] = 0.0
    return jnp.asarray(ov, BF16)


def _layer(x, attn_norm_g, w_in, gla_conv_w, gla_gate_w2, gla_gate_b, gla_norm_g,
           nsa_q_norm_g, nsa_kc_norm_g, nsa_ks_norm_g, nsa_kw_norm_g,
           cmp_k_pos, cmp_k_w1, cmp_k_w2, cmp_v_pos, cmp_v_w1, cmp_v_w2,
           w_out, ffn_norm_g, w_gate, w_up, w_down):
    s = x.shape[0]
    dh = NSA_HEAD_DIM
    row = lambda v: v.reshape(1, -1)
    t = _tiles(s)

    p, qn, ksa, vsa, kwn, vwa = _in_proj(
        x, row(attn_norm_g), _regroup_w_in(w_in), row(nsa_q_norm_g), row(nsa_ks_norm_g),
        row(nsa_kw_norm_g), t.proj_rows, PROJ_PAD // 2)

    gla_out = _gla(p, gla_conv_w, gla_gate_w2, row(gla_gate_b), row(gla_norm_g), t.gla_rows)

    w1 = jnp.stack([cmp_k_w1, cmp_v_w1])
    w1s = (w1.reshape(2, 2, CMP_STRIDE, dh, CMP_HIDDEN).transpose(0, 2, 3, 1, 4)
           .reshape(2, CMP_STRIDE, dh, 2 * CMP_HIDDEN).astype(BF16))
    pos = jnp.stack([cmp_k_pos, cmp_v_pos]).reshape(2, 1, CMP_BLOCK * dh)
    w2 = jnp.stack([cmp_k_w2, cmp_v_w2]).astype(BF16)
    cmp, cmp_a = _compress(p, w1s, pos, w1, w2, row(nsa_kc_norm_g))

    nb = s // SEL_BLOCK
    nbp = -(-nb // LANES) * LANES
    gq = row(nsa_q_norm_g)
    gks = row(nsa_ks_norm_g)
    o_cmp, bias = _cmp_attn(qn, cmp, cmp_a, _overlap_t(s, nbp), p, gq, row(nsa_kc_norm_g), gks,
                            t.cmp_q, min(SEL_TOPK, nb))
    o_cw = _win_attn(qn, kwn, vwa, p, o_cmp, gq, row(nsa_kw_norm_g), t.win_q)
    nsa_out = _sel_attn(qn, bias, ksa, vsa, p, o_cw, gq, gks, t.sel_q, t.sel_k)

    h = _out_proj(x, gla_out, nsa_out, w_out, t.out_rows)
    return _ffn(h, row(ffn_norm_g), w_gate, w_up, w_down, t.ffn_rows, t.ffn_cols)


def kernel(x, attn_norm_g, w_in, gla_conv_w, gla_gate_w2, gla_gate_b, gla_norm_g, nsa_q_norm_g, nsa_kc_norm_g, nsa_ks_norm_g, nsa_kw_norm_g, cmp_k_pos, cmp_k_w1, cmp_k_w2, cmp_v_pos, cmp_v_w1, cmp_v_w2, w_out, ffn_norm_g, w_gate, w_up, w_down):
    assert x.shape[0] == 1 and attn_norm_g.shape[0] == 1
    y = _layer(x[0], attn_norm_g[0], w_in[0], gla_conv_w[0], gla_gate_w2[0], gla_gate_b[0],
               gla_norm_g[0], nsa_q_norm_g[0], nsa_kc_norm_g[0], nsa_ks_norm_g[0], nsa_kw_norm_g[0],
               cmp_k_pos[0], cmp_k_w1[0], cmp_k_w2[0], cmp_v_pos[0], cmp_v_w1[0], cmp_v_w2[0],
               w_out[0], ffn_norm_g[0], w_gate[0], w_up[0], w_down[0])
    return y[None]
```
